```python
import math
import jax, jax.numpy as jnp
from jax import lax
import numpy as np

D_MODEL = 2048
BATCH = 4
SEQ = 2048
DEPTH = 2
DEC_BATCH = 128
DEC_SEQ = 1
PAST_LEN = 16384
PAGE_SIZE = 128

RET_HEADS = 4
RET_DK = 128
RET_DV = 128
RET_QK = RET_HEADS * RET_DK
RET_W = RET_HEADS * RET_DV
RET_CHUNK = 128
ROPE_BASE = 10000.0
GDN_HEADS = 8
GDN_DK = 128
GDN_DV = 128
GDN_QK = GDN_HEADS * GDN_DK
GDN_W = GDN_HEADS * GDN_DV
GDN_CHUNK = 64
CONV_W = 4
GDN_CONV_C = 2 * GDN_QK + GDN_W
SSM_W = D_MODEL - RET_W - GDN_W
SSM_GROUP = 16
SSM_GROUPS = SSM_W // SSM_GROUP
SSM_N = 64
MIX_W = RET_W + GDN_W + SSM_W
IN_SIZES = (RET_QK, RET_QK, RET_W, RET_W, GDN_QK, GDN_QK, GDN_W, GDN_W, GDN_HEADS, GDN_HEADS, SSM_W)
IN_COLS = sum(IN_SIZES)
D_FF = -(-8 * D_MODEL // (3 * 256)) * 256
ALPHA = (2 * DEPTH) ** 0.25
BETA = (8 * DEPTH) ** -0.25
LN_EPS = 1e-5
RMS_EPS = 1e-6
L2_EPS = 1e-6

kernel_name = 'hybrid_ret_gdn_s5_adaln_deepnorm_step'


def _layernorm(x, w, b):
    xf = x.astype(jnp.float32)
    mu = xf.mean(-1, keepdims=True)
    var = jnp.square(xf - mu).mean(-1, keepdims=True)
    return ((xf - mu) * lax.rsqrt(var + LN_EPS) * w + b).astype(x.dtype)


def _l2norm(x):
    return x * lax.rsqrt(jnp.sum(x * x, -1, keepdims=True) + L2_EPS)


def _rope(x, pos):
    half = x.shape[-1] // 2
    freq = ROPE_BASE ** (-jnp.arange(half, dtype=jnp.float32) / half)
    ang = pos.astype(jnp.float32)[:, None] * freq[None, :]
    cos = jnp.cos(ang)[None, :, None, :]
    sin = jnp.sin(ang)[None, :, None, :]
    x1, x2 = x[..., :half], x[..., half:]
    return jnp.concatenate([x1 * cos - x2 * sin, x1 * sin + x2 * cos], axis=-1)


def _blocks(t, n):
    B, L, H = t.shape[:3]
    t = t.reshape((B, n, L // n, H) + t.shape[3:])
    return jnp.moveaxis(t, (1, 3), (0, 2))


def _unblocks(t):
    n, B, H, C = t.shape[:4]
    return jnp.moveaxis(t, (0, 2), (1, 3)).reshape((B, n * C, H) + t.shape[4:])


def _retention(q, k, v, s0):
    B, L, H, dk = q.shape
    C = math.gcd(L, RET_CHUNK)
    n = L // C
    log_g = jnp.log1p(-jnp.power(2.0, -5.0 - jnp.arange(H, dtype=jnp.float32)))
    idx = jnp.arange(C, dtype=jnp.float32)
    diff = idx[:, None] - idx[None, :]
    dmask = jnp.exp(jnp.where(diff >= 0, log_g[:, None, None] * diff, -jnp.inf))
    qc, kc, vc = _blocks(q, n), _blocks(k, n), _blocks(v, n)
    scores = jnp.einsum('nbhid,nbhjd->nbhij', qc, kc) * dmask
    intra = jnp.einsum('nbhij,nbhje->nbhie', scores, vc)
    k_w = kc * jnp.exp(log_g[:, None] * (C - 1 - idx))[..., None]
    kv = jnp.einsum('nbhjd,nbhje->nbhde', k_w, vc)
    g_chunk = jnp.exp(log_g * C)[:, None, None]

    def step(s, kv_i):
        return s * g_chunk + kv_i, s

    s_fin, s_prev = lax.scan(step, s0, kv)
    q_w = qc * jnp.exp(log_g[:, None] * (idx + 1))[..., None]
    cross = jnp.einsum('nbhid,nbhde->nbhie', q_w, s_prev)
    return _unblocks(intra + cross), s_fin


def _gated_delta(q, k, v, g, beta, s0):
    B, L, H, dk = q.shape
    dv = v.shape[-1]
    C = math.gcd(L, GDN_CHUNK)
    n = L // C
    qc, kc, vc = _blocks(q, n), _blocks(k, n), _blocks(v, n)
    gc = jnp.cumsum(_blocks(g, n), axis=-1)
    bc = _blocks(beta, n)[..., None]
    tri = jnp.tril(jnp.ones((C, C), dtype=bool))
    strict = jnp.tril(jnp.ones((C, C), dtype=bool), -1)
    decay = jnp.exp(jnp.where(tri, gc[..., :, None] - gc[..., None, :], -jnp.inf))
    kb = kc * bc
    a = jnp.where(strict, jnp.einsum('nbhid,nbhjd->nbhij', kb, kc) * decay, 0.0)
    m = a + jnp.eye(C, dtype=a.dtype)
    rhs = jnp.concatenate([vc * bc, kb * jnp.exp(gc)[..., None]], axis=-1)
    sol = lax.linalg.triangular_solve(m, rhs, left_side=True, lower=True, unit_diagonal=True)
    u, w = sol[..., :dv], sol[..., dv:]
    attn = jnp.where(tri, jnp.einsum('nbhid,nbhjd->nbhij', qc, kc) * decay, 0.0)
    q_dec = qc * jnp.exp(gc)[..., None]
    k_dec = kc * jnp.exp(gc[..., -1:] - gc)[..., None]
    g_last = jnp.exp(gc[..., -1])[..., None, None]

    def step(s, xs):
        u_i, w_i, a_i, qd_i, kd_i, gl_i = xs
        v_new = u_i - jnp.einsum('bhcd,bhde->bhce', w_i, s)
        o_i = jnp.einsum('bhcd,bhde->bhce', qd_i, s) + jnp.einsum('bhij,bhje->bhie', a_i, v_new)
        s = s * gl_i + jnp.einsum('bhcd,bhce->bhde', kd_i, v_new)
        return s, o_i

    s_fin, o = lax.scan(step, s0, (u, w, attn, q_dec, k_dec, g_last))
    return _unblocks(o), s_fin


def _short_conv(xin, buf, w):
    L = xin.shape[1]
    xp = jnp.concatenate([buf.astype(xin.dtype), xin], axis=1)
    y = xp[:, 0:L] * w[0]
    for i in range(1, CONV_W):
        y = y + xp[:, i:i + L] * w[i]
    return jax.nn.silu(y), xp[:, L:].astype(buf.dtype)


def _s5(u, h0_re, h0_im, a_re, a_im, log_dt, b_re, b_im, c_re, c_im, d_skip, w_glu):
    f32 = jnp.float32
    B, L, _ = u.shape
    ug = u.astype(f32).reshape(B, L, SSM_GROUPS, SSM_GROUP)
    lam = lax.complex(a_re.astype(f32), a_im.astype(f32))
    dt = jnp.exp(log_dt.astype(f32))[:, None]
    lam_bar = jnp.exp(lam * dt)
    b = lax.complex(b_re.astype(f32), b_im.astype(f32))
    b_bar = ((lam_bar - 1.0) / lam)[..., None] * b
    c = lax.complex(c_re.astype(f32), c_im.astype(f32))
    h0 = lax.complex(h0_re.astype(f32), h0_im.astype(f32))
    bu = jnp.einsum('blgp,gnp->blgn', ug, b_bar)
    bu = bu.at[:, 0].add(lam_bar[None] * h0)
    a_seq = jnp.broadcast_to(lam_bar, bu.shape)

    def comb(e1, e2):
        a1, x1 = e1
        a2, x2 = e2
        return a1 * a2, a2 * x1 + x2

    _, h = lax.associative_scan(comb, (a_seq, bu), axis=1)
    y = jnp.einsum('blgn,gpn->blgp', h, c).real + d_skip.astype(f32).reshape(SSM_GROUPS, SSM_GROUP) * ug
    y = jax.nn.gelu(y.reshape(B, L, SSM_W))
    y = y * jax.nn.sigmoid(y @ w_glu.astype(f32))
    h_last = h[:, -1]
    return y, h_last.real.astype(h0_re.dtype), h_last.imag.astype(h0_im.dtype)


def _layer(x, c, pos, state, p):
    s_ret, s_gdn, s_conv, s_re, s_im = state
    f32 = jnp.float32
    B, L, _ = x.shape
    mod = jax.nn.silu(c) @ p['w_mod'] + p['b_mod']
    sh1, sc1, g1, sh2, sc2, g2 = [m[:, None, :] for m in jnp.split(mod, 6, axis=-1)]
    h = x * (1.0 + sc1) + sh1
    proj = h @ p['w_in']
    splits = np.cumsum(IN_SIZES)[:-1].tolist()
    rq, rk, rv, rg, dq, dk_, dv_, dz, da, db, su = jnp.split(proj, splits, axis=-1)
    rq = _rope(rq.reshape(B, L, RET_HEADS, RET_DK).astype(f32), pos)
    rk = _rope(rk.reshape(B, L, RET_HEADS, RET_DK).astype(f32), pos) * RET_DK ** -0.5
    rv = rv.reshape(B, L, RET_HEADS, RET_DV).astype(f32)
    ro, s_ret_new = _retention(rq, rk, rv, s_ret.astype(f32))
    mu = ro.mean(-1, keepdims=True)
    var = jnp.square(ro - mu).mean(-1, keepdims=True)
    ro = ((ro - mu) * lax.rsqrt(var + LN_EPS)).reshape(B, L, RET_W) * p['ret_gn_w'] * jax.nn.silu(rg.astype(f32))
    conv_out, conv_new = _short_conv(jnp.concatenate([dq, dk_, dv_], axis=-1), s_conv, p['conv_w'])
    cq, ck, cv = jnp.split(conv_out.astype(f32), [GDN_QK, 2 * GDN_QK], axis=-1)
    cq = _l2norm(cq.reshape(B, L, GDN_HEADS, GDN_DK)) * GDN_DK ** -0.5
    ck = _l2norm(ck.reshape(B, L, GDN_HEADS, GDN_DK))
    cv = cv.reshape(B, L, GDN_HEADS, GDN_DV)
    g = -jnp.exp(p['gdn_a_log'].astype(f32)) * jax.nn.softplus(da.astype(f32) + p['gdn_dt_bias'].astype(f32))
    beta = jax.nn.sigmoid(db.astype(f32))
    go, s_gdn_new = _gated_delta(cq, ck, cv, g, beta, s_gdn.astype(f32))
    go = go * lax.rsqrt(jnp.mean(go * go, -1, keepdims=True) + RMS_EPS) * p['gdn_norm_w']
    go = go.reshape(B, L, GDN_W) * jax.nn.silu(dz.astype(f32))
    so, s_re_new, s_im_new = _s5(su, s_re, s_im, p['ssm_a_re'], p['ssm_a_im'], p['ssm_log_dt'],
                                 p['ssm_b_re'], p['ssm_b_im'], p['ssm_c_re'], p['ssm_c_im'],
                                 p['ssm_d'], p['ssm_w_glu'])
    mix = jnp.concatenate([ro, go, so], axis=-1).astype(x.dtype) @ p['w_out']
    x = _layernorm(ALPHA * x + g1 * mix, p['ln1_w'], p['ln1_b'])
    h = x * (1.0 + sc2) + sh2
    gate, up = jnp.split(h @ p['w_ffn_in'], 2, axis=-1)
    ff = (jax.nn.silu(gate) * up) @ p['w_ffn_out']
    x = _layernorm(ALPHA * x + g2 * ff, p['ln2_w'], p['ln2_b'])
    new_state = (s_ret_new.astype(s_ret.dtype), s_gdn_new.astype(s_gdn.dtype), conv_new, s_re_new, s_im_new)
    return x, new_state


def setup_inputs(seed: int = 0) -> dict:
    key = jax.random.key(seed)
    ks = iter(jax.random.split(key, 48))
    f32 = jnp.float32

    def nrm(shape, scale):
        return scale * jax.random.normal(next(ks), shape, f32)

    x_prompt = nrm((BATCH, SEQ, D_MODEL), 1.0)
    x_sample = nrm((DEC_BATCH, DEC_SEQ, D_MODEL), 1.0)
    c_prompt = nrm((BATCH, D_MODEL), 1.0)
    c_sample = nrm((DEC_BATCH, D_MODEL), 1.0)
    state_ret = nrm((DEPTH, DEC_BATCH, RET_HEADS, RET_DK, RET_DV), 0.1)
    state_gdn = nrm((DEPTH, DEC_BATCH, GDN_HEADS, GDN_DK, GDN_DV), 0.1)
    state_conv = nrm((DEPTH, DEC_BATCH, CONV_W - 1, GDN_CONV_C), 1.0)
    state_ssm_re = nrm((DEPTH, DEC_BATCH, SSM_GROUPS, SSM_N), 0.3)
    state_ssm_im = nrm((DEPTH, DEC_BATCH, SSM_GROUPS, SSM_N), 0.3)
    w_mod = nrm((DEPTH, D_MODEL, 6 * D_MODEL), D_MODEL ** -0.5)
    b_mod = nrm((DEPTH, 6 * D_MODEL), 0.02)
    w_in = nrm((DEPTH, D_MODEL, IN_COLS), D_MODEL ** -0.5)
    conv_w = nrm((DEPTH, CONV_W, GDN_CONV_C), CONV_W ** -0.5)
    ret_gn_w = 1.0 + nrm((DEPTH, RET_W), 0.02)
    gdn_a_log = jnp.log(jax.random.uniform(next(ks), (DEPTH, GDN_HEADS), f32, 1.0, 16.0))
    dt = jnp.exp(jax.random.uniform(next(ks), (DEPTH, GDN_HEADS), f32, math.log(1e-3), math.log(1e-1)))
    gdn_dt_bias = dt + jnp.log(-jnp.expm1(-dt))
    gdn_norm_w = 1.0 + nrm((DEPTH, GDN_DV), 0.02)
    ssm_a_re = -0.5 + nrm((DEPTH, SSM_GROUPS, SSM_N), 0.01)
    ssm_a_im = math.pi * jnp.arange(SSM_N, dtype=f32) + nrm((DEPTH, SSM_GROUPS, SSM_N), 0.01)
    ssm_log_dt = jax.random.uniform(next(ks), (DEPTH, SSM_GROUPS), f32, math.log(1e-3), math.log(1e-1))
    ssm_b_re = nrm((DEPTH, SSM_GROUPS, SSM_N, SSM_GROUP), (2 * SSM_GROUP) ** -0.5)
    ssm_b_im = nrm((DEPTH, SSM_GROUPS, SSM_N, SSM_GROUP), (2 * SSM_GROUP) ** -0.5)
    ssm_c_re = nrm((DEPTH, SSM_GROUPS, SSM_GROUP, SSM_N), 0.5 ** 0.5)
    ssm_c_im = nrm((DEPTH, SSM_GROUPS, SSM_GROUP, SSM_N), 0.5 ** 0.5)
    ssm_d = nrm((DEPTH, SSM_W), 1.0)
    ssm_w_glu = nrm((DEPTH, SSM_W, SSM_W), SSM_W ** -0.5)
    w_out = nrm((DEPTH, MIX_W, D_MODEL), BETA * MIX_W ** -0.5)
    ln1_w = 1.0 + nrm((DEPTH, D_MODEL), 0.02)
    ln1_b = nrm((DEPTH, D_MODEL), 0.02)
    w_ffn_in = nrm((DEPTH, D_MODEL, 2 * D_FF), D_MODEL ** -0.5)
    w_ffn_out = nrm((DEPTH, D_FF, D_MODEL), BETA * D_FF ** -0.5)
    ln2_w = 1.0 + nrm((DEPTH, D_MODEL), 0.02)
    ln2_b = nrm((DEPTH, D_MODEL), 0.02)
    return {'x_prompt': x_prompt, 'x_sample': x_sample, 'c_prompt': c_prompt, 'c_sample': c_sample,
            'state_ret': state_ret, 'state_gdn': state_gdn, 'state_conv': state_conv,
            'state_ssm_re': state_ssm_re, 'state_ssm_im': state_ssm_im,
            'w_mod': w_mod, 'b_mod': b_mod, 'w_in': w_in, 'conv_w': conv_w, 'ret_gn_w': ret_gn_w,
            'gdn_a_log': gdn_a_log, 'gdn_dt_bias': gdn_dt_bias, 'gdn_norm_w': gdn_norm_w,
            'ssm_a_re': ssm_a_re, 'ssm_a_im': ssm_a_im, 'ssm_log_dt': ssm_log_dt,
            'ssm_b_re': ssm_b_re, 'ssm_b_im': ssm_b_im, 'ssm_c_re': ssm_c_re, 'ssm_c_im': ssm_c_im,
            'ssm_d': ssm_d, 'ssm_w_glu': ssm_w_glu, 'w_out': w_out, 'ln1_w': ln1_w, 'ln1_b': ln1_b,
            'w_ffn_in': w_ffn_in, 'w_ffn_out': w_ffn_out, 'ln2_w': ln2_w, 'ln2_b': ln2_b}


def reference(x_prompt, x_sample, c_prompt, c_sample, state_ret, state_gdn, state_conv,
              state_ssm_re, state_ssm_im, w_mod, b_mod, w_in, conv_w, ret_gn_w, gdn_a_log,
              gdn_dt_bias, gdn_norm_w, ssm_a_re, ssm_a_im, ssm_log_dt, ssm_b_re, ssm_b_im,
              ssm_c_re, ssm_c_im, ssm_d, ssm_w_glu, w_out, ln1_w, ln1_b, w_ffn_in, w_ffn_out,
              ln2_w, ln2_b):
    Bp = x_prompt.shape[0]
    pos_p = jnp.arange(x_prompt.shape[1])
    pos_s = PAST_LEN + jnp.arange(x_sample.shape[1])
    zero_state = (jnp.zeros((Bp, RET_HEADS, RET_DK, RET_DV), state_ret.dtype),
                  jnp.zeros((Bp, GDN_HEADS, GDN_DK, GDN_DV), state_gdn.dtype),
                  jnp.zeros((Bp, CONV_W - 1, GDN_CONV_C), state_conv.dtype),
                  jnp.zeros((Bp, SSM_GROUPS, SSM_N), state_ssm_re.dtype),
                  jnp.zeros((Bp, SSM_GROUPS, SSM_N), state_ssm_im.dtype))
    yp, ys = x_prompt, x_sample
    new_p, new_s = [], []
    for l in range(DEPTH):
        p = dict(w_mod=w_mod[l], b_mod=b_mod[l], w_in=w_in[l], conv_w=conv_w[l], ret_gn_w=ret_gn_w[l],
                 gdn_a_log=gdn_a_log[l], gdn_dt_bias=gdn_dt_bias[l], gdn_norm_w=gdn_norm_w[l],
                 ssm_a_re=ssm_a_re[l], ssm_a_im=ssm_a_im[l], ssm_log_dt=ssm_log_dt[l],
                 ssm_b_re=ssm_b_re[l], ssm_b_im=ssm_b_im[l], ssm_c_re=ssm_c_re[l], ssm_c_im=ssm_c_im[l],
                 ssm_d=ssm_d[l], ssm_w_glu=ssm_w_glu[l], w_out=w_out[l], ln1_w=ln1_w[l], ln1_b=ln1_b[l],
                 w_ffn_in=w_ffn_in[l], w_ffn_out=w_ffn_out[l], ln2_w=ln2_w[l], ln2_b=ln2_b[l])
        yp, st_p = _layer(yp, c_prompt, pos_p, zero_state, p)
        past = (state_ret[l], state_gdn[l], state_conv[l], state_ssm_re[l], state_ssm_im[l])
        ys, st_s = _layer(ys, c_sample, pos_s, past, p)
        new_p.append(st_p)
        new_s.append(st_s)
    ret_p, gdn_p, conv_p, ssm_re_p, ssm_im_p = [jnp.stack(t) for t in zip(*new_p)]
    ret_s, gdn_s, conv_s, ssm_re_s, ssm_im_s = [jnp.stack(t) for t in zip(*new_s)]
    return (yp, ys, ret_p, gdn_p, conv_p, ssm_re_p, ssm_im_p, ret_s, gdn_s, conv_s, ssm_re_s, ssm_im_s)
```

```python
import functools
import math

import numpy as np
import jax
import jax.numpy as jnp
from jax import lax
from jax.experimental import pallas as pl
from jax.experimental.pallas import tpu as pltpu

F32 = jnp.float32
BF16 = jnp.bfloat16

D_MODEL = 2048
DEPTH = 2
PAST_LEN = 16384
RET_HEADS = 4
RET_DK = 128
RET_W = 512
RET_CHUNK = 128
ROPE_BASE = 10000.0
GDN_HEADS = 8
GDN_DK = 128
GDN_W = 1024
GDN_CHUNK = 64
CONV_W = 4
GDN_CONV_C = 3072
SSM_W = 512
SSM_GROUP = 16
SSM_GROUPS = 32
SSM_N = 64
D_FF = 5632
ALPHA = (2 * DEPTH) ** 0.25
LN_EPS = 1e-5
RMS_EPS = 1e-6
L2_EPS = 1e-6

COL_SU = 6144
COL_AB = 6656
N_IN = 6672
N_PROJ = 6912

SSM_TILES = 4
SSM_TILE_G = 8
SSM_HALF = SSM_TILE_G * SSM_N
SSM_TILE_W = 2 * SSM_HALF
SSM_SEG = 256
SSM_PITCH = 260

VMEM_LIMIT = 56 * 1024 * 1024

RET_LOG_G = [math.log1p(-(2.0 ** (-5.0 - h))) for h in range(RET_HEADS)]

NT = (((1,), (1,)), ((), ()))
TN = (((0,), (0,)), ((), ()))
HI = lax.Precision.HIGHEST


def _params(sem):
    return pltpu.CompilerParams(dimension_semantics=sem, vmem_limit_bytes=VMEM_LIMIT)


def _silu(x):
    return x * jax.nn.sigmoid(x)


def _bdot(a, b):
    return jnp.dot(a.astype(BF16), b.astype(BF16), preferred_element_type=F32)


def _hdot(a, b):
    return jnp.dot(a, b, precision=HI, preferred_element_type=F32)


def _bdot_nt(a, b):
    return lax.dot_general(a.astype(BF16), b.astype(BF16), NT, preferred_element_type=F32)


def _bdot_tn(a, b):
    return lax.dot_general(a.astype(BF16), b.astype(BF16), TN, preferred_element_type=F32)


def _layernorm(z, w, b):
    mu = jnp.mean(z, axis=-1, keepdims=True)
    zc = z - mu
    var = jnp.mean(zc * zc, axis=-1, keepdims=True)
    return zc * lax.rsqrt(var + LN_EPS) * w + b


def _mod_body(c_ref, w_ref, b_ref, o_ref):
    c = c_ref[...]
    o_ref[0] = _bdot(_silu(c), w_ref[0]) + b_ref[0]


def _modulation(c_all, w_mod, b_mod):
    rows = c_all.shape[0]
    tn = 1024
    n = 6 * D_MODEL
    return pl.pallas_call(
        _mod_body,
        grid=(DEPTH, n // tn),
        in_specs=[pl.BlockSpec((rows, D_MODEL), lambda l, j: (0, 0)),
                  pl.BlockSpec((1, D_MODEL, tn), lambda l, j: (l, 0, j)),
                  pl.BlockSpec((1, 1, tn), lambda l, j: (l, 0, j))],
        out_specs=pl.BlockSpec((1, rows, tn), lambda l, j: (l, 0, j)),
        out_shape=jax.ShapeDtypeStruct((DEPTH, rows, n), F32),
        compiler_params=_params(("parallel", "parallel")),
        name="modulation",
    )(c_all, w_mod, b_mod.reshape(DEPTH, 1, n))


def _mod_spec(mod3, tm, k, grid_rank):
    per_token = mod3.shape[1] != 1
    lmb = tm if per_token else 1
    if grid_rank == 3:
        return pl.BlockSpec((1, lmb, D_MODEL), lambda b, i, j: (b, i if per_token else 0, k))
    return pl.BlockSpec((1, lmb, D_MODEL), lambda b, i: (b, i if per_token else 0, k))


def _inproj_body(x_ref, sc_ref, sh_ref, w_ref, o_ref, h_ref):
    @pl.when(pl.program_id(2) == 0)
    def _():
        h_ref[...] = (x_ref[0] * (1.0 + sc_ref[0]) + sh_ref[0]).astype(BF16)

    o_ref[0] = jnp.dot(h_ref[...], w_ref[...], preferred_element_type=F32)


def _inproj(x, mod3, w, tm, tn):
    B, L, _ = x.shape
    n = w.shape[1]
    return pl.pallas_call(
        _inproj_body,
        grid=(B, L // tm, n // tn),
        in_specs=[pl.BlockSpec((1, tm, D_MODEL), lambda b, i, j: (b, i, 0)),
                  _mod_spec(mod3, tm, 1, 3),
                  _mod_spec(mod3, tm, 0, 3),
                  pl.BlockSpec((D_MODEL, tn), lambda b, i, j: (0, j))],
        out_specs=pl.BlockSpec((1, tm, tn), lambda b, i, j: (b, i, j)),
        out_shape=jax.ShapeDtypeStruct((B, L, n), F32),
        scratch_shapes=[pltpu.VMEM((tm, D_MODEL), BF16)],
        compiler_params=_params(("parallel", "parallel", "arbitrary")),
        name="inproj",
    )(x, mod3, mod3, w)


def _outproj_body(ro_ref, go_ref, y_ref, x_ref, g1_ref, wglu_ref, wout_ref, lnw_ref, lnb_ref, o_ref):
    y = y_ref[0]
    so = y * jax.nn.sigmoid(_bdot(y, wglu_ref[...]))
    acc = jnp.dot(ro_ref[0], wout_ref[0:RET_W, :], preferred_element_type=F32)
    acc += jnp.dot(go_ref[0], wout_ref[RET_W:RET_W + GDN_W, :], preferred_element_type=F32)
    acc += jnp.dot(so.astype(BF16), wout_ref[RET_W + GDN_W:, :], preferred_element_type=F32)
    z = ALPHA * x_ref[0] + g1_ref[0] * acc
    o_ref[0] = _layernorm(z, lnw_ref[...], lnb_ref[...])


def _outproj(ro, go, y, x, mod3, wglu, wout, lnw, lnb, tm):
    B, L, _ = x.shape
    return pl.pallas_call(
        _outproj_body,
        grid=(B, L // tm),
        in_specs=[pl.BlockSpec((1, tm, RET_W), lambda b, i: (b, i, 0)),
                  pl.BlockSpec((1, tm, GDN_W), lambda b, i: (b, i, 0)),
                  pl.BlockSpec((1, tm, SSM_W), lambda b, i: (b, i, 0)),
                  pl.BlockSpec((1, tm, D_MODEL), lambda b, i: (b, i, 0)),
                  _mod_spec(mod3, tm, 2, 2),
                  pl.BlockSpec((SSM_W, SSM_W), lambda b, i: (0, 0)),
                  pl.BlockSpec((D_MODEL, D_MODEL), lambda b, i: (0, 0)),
                  pl.BlockSpec((1, D_MODEL), lambda b, i: (0, 0)),
                  pl.BlockSpec((1, D_MODEL), lambda b, i: (0, 0))],
        out_specs=pl.BlockSpec((1, tm, D_MODEL), lambda b, i: (b, i, 0)),
        out_shape=jax.ShapeDtypeStruct((B, L, D_MODEL), F32),
        compiler_params=_params(("parallel", "parallel")),
        name="outproj_ln1",
    )(ro, go, y, x, mod3, wglu, wout, lnw, lnb)


def _ffn_body(x_ref, sc_ref, sh_ref, g2_ref, wg_ref, wu_ref, wo_ref, lnw_ref, lnb_ref, o_ref, h_ref, acc_ref):
    f = pl.program_id(2)

    @pl.when(f == 0)
    def _():
        h_ref[...] = (x_ref[0] * (1.0 + sc_ref[0]) + sh_ref[0]).astype(BF16)
        acc_ref[...] = jnp.zeros_like(acc_ref)

    h = h_ref[...]
    gate = jnp.dot(h, wg_ref[...], preferred_element_type=F32)
    up = jnp.dot(h, wu_ref[...], preferred_element_type=F32)
    act = (_silu(gate) * up).astype(BF16)
    acc_ref[...] += jnp.dot(act, wo_ref[...], preferred_element_type=F32)

    @pl.when(f == pl.num_programs(2) - 1)
    def _():
        z = ALPHA * x_ref[0] + g2_ref[0] * acc_ref[...]
        o_ref[0] = _layernorm(z, lnw_ref[...], lnb_ref[...])


def _ffn(x, mod3, w_in, w_out, lnw, lnb, tm, tf):
    B, L, _ = x.shape
    nf = D_FF // tf
    return pl.pallas_call(
        _ffn_body,
        grid=(B, L // tm, nf),
        in_specs=[pl.BlockSpec((1, tm, D_MODEL), lambda b, i, f: (b, i, 0)),
                  _mod_spec(mod3, tm, 4, 3),
                  _mod_spec(mod3, tm, 3, 3),
                  _mod_spec(mod3, tm, 5, 3),
                  pl.BlockSpec((D_MODEL, tf), lambda b, i, f: (0, f)),
                  pl.BlockSpec((D_MODEL, tf), lambda b, i, f: (0, f + nf)),
                  pl.BlockSpec((tf, D_MODEL), lambda b, i, f: (f, 0)),
                  pl.BlockSpec((1, D_MODEL), lambda b, i, f: (0, 0)),
                  pl.BlockSpec((1, D_MODEL), lambda b, i, f: (0, 0))],
        out_specs=pl.BlockSpec((1, tm, D_MODEL), lambda b, i, f: (b, i, 0)),
        out_shape=jax.ShapeDtypeStruct((B, L, D_MODEL), F32),
        scratch_shapes=[pltpu.VMEM((tm, D_MODEL), BF16), pltpu.VMEM((tm, D_MODEL), F32)],
        compiler_params=_params(("parallel", "parallel", "arbitrary")),
        name="ffn_ln2",
    )(x, mod3, mod3, mod3, w_in, w_in, w_out, lnw, lnb)


def _rope_body(cp_ref, sp_ref, cs_ref, ss_ref):
    half = RET_DK // 2
    for c_ref, s_ref, base, step in ((cp_ref, sp_ref, 0, 1), (cs_ref, ss_ref, PAST_LEN, 0)):
        shape = c_ref.shape
        lane = lax.broadcasted_iota(jnp.int32, shape, 1)
        row = lax.broadcasted_iota(jnp.int32, shape, 0)
        freq = jnp.exp(jnp.where(lane < half, lane, lane - half).astype(F32) * (-math.log(ROPE_BASE) / half))
        ang = (row * step + base).astype(F32) * freq
        c_ref[...] = jnp.cos(ang)
        s = jnp.sin(ang)
        s_ref[...] = jnp.where(lane < RET_DK // 2, -s, s)


def _rope_tables(seq):
    return pl.pallas_call(
        _rope_body,
        out_shape=[jax.ShapeDtypeStruct((seq, RET_DK), F32), jax.ShapeDtypeStruct((seq, RET_DK), F32),
                   jax.ShapeDtypeStruct((8, RET_DK), F32), jax.ShapeDtypeStruct((8, RET_DK), F32)],
        name="rope_tables",
    )()


def _rope(x, cos, sin_signed):
    return x * cos + pltpu.roll(x, RET_DK // 2, 1) * sin_signed


def _ret_prompt_body(q_ref, k_ref, v_ref, g_ref, cos_ref, sin_ref, gnw_ref, o_ref, s_ref, S):
    c = pl.program_id(1)
    C = RET_CHUNK

    @pl.when(c == 0)
    def _():
        S[...] = jnp.zeros_like(S)

    cos = cos_ref[...]
    sin = sin_ref[...]
    row = lax.broadcasted_iota(jnp.int32, (C, C), 0).astype(F32)
    col = lax.broadcasted_iota(jnp.int32, (C, C), 1).astype(F32)
    diff = row - col
    for h in range(RET_HEADS):
        lg = RET_LOG_G[h]
        sl = slice(h * RET_DK, (h + 1) * RET_DK)
        q = _rope(q_ref[0, :, sl], cos, sin)
        k = _rope(k_ref[0, :, sl], cos, sin) * (RET_DK ** -0.5)
        v = v_ref[0, :, sl]
        dmask = jnp.exp(jnp.where(diff >= 0, lg * diff, -jnp.inf))
        scores = _bdot_nt(q, k) * dmask
        intra = _bdot(scores, v)
        kv = _bdot_tn(k * jnp.exp(lg * (C - 1 - row)), v)
        s_prev = S[h]
        cross = _bdot(q * jnp.exp(lg * (row + 1.0)), s_prev)
        S[h] = s_prev * math.exp(lg * C) + kv
        ro = intra + cross
        mu = jnp.mean(ro, axis=-1, keepdims=True)
        rc = ro - mu
        var = jnp.mean(rc * rc, axis=-1, keepdims=True)
        rn = rc * lax.rsqrt(var + LN_EPS) * gnw_ref[:, sl] * _silu(g_ref[0, :, sl])
        o_ref[0, :, sl] = rn.astype(BF16)

    @pl.when(c == pl.num_programs(1) - 1)
    def _():
        s_ref[0] = S[...]


def _ret_prompt(proj, cos, sin, gnw):
    B, L, _ = proj.shape
    C = RET_CHUNK
    blk = lambda k: pl.BlockSpec((1, C, RET_W), lambda b, c: (b, c, k))
    return pl.pallas_call(
        _ret_prompt_body,
        grid=(B, L // C),
        in_specs=[blk(0), blk(1), blk(2), blk(3),
                  pl.BlockSpec((C, RET_DK), lambda b, c: (c, 0)),
                  pl.BlockSpec((C, RET_DK), lambda b, c: (c, 0)),
                  pl.BlockSpec((1, RET_W), lambda b, c: (0, 0))],
        out_specs=[pl.BlockSpec((1, C, RET_W), lambda b, c: (b, c, 0)),
                   pl.BlockSpec((1, RET_HEADS, RET_DK, RET_DK), lambda b, c: (b, 0, 0, 0))],
        out_shape=[jax.ShapeDtypeStruct((B, L, RET_W), BF16),
                   jax.ShapeDtypeStruct((B, RET_HEADS, RET_DK, RET_DK), F32)],
        scratch_shapes=[pltpu.VMEM((RET_HEADS, RET_DK, RET_DK), F32)],
        compiler_params=_params(("parallel", "arbitrary")),
        name="ret_prompt",
    )(proj, proj, proj, proj, cos, sin, gnw)


def _ret_sample_body(q_ref, k_ref, v_ref, g_ref, cos_ref, sin_ref, gnw_ref, dec_ref, s0_ref, o_ref, s_ref, obuf):
    nb = q_ref.shape[0]
    cos = cos_ref[0:1, :]
    sin = sin_ref[0:1, :]
    gdec = dec_ref[0]
    q = _rope(q_ref[...], cos, sin)
    k = _rope(k_ref[...], cos, sin) * (RET_DK ** -0.5)
    v = v_ref[...]
    qk = jnp.sum(q * k, axis=1, keepdims=True)
    qT = q.T
    kT = k.T
    for b in range(nb):
        s0 = s0_ref[b, 0]
        qcol = qT[:, b:b + 1]
        kcol = kT[:, b:b + 1]
        vrow = v[b:b + 1, :]
        cross = gdec * jnp.sum(s0 * qcol, axis=0, keepdims=True)
        s_ref[b, 0] = s0 * gdec + kcol * vrow
        obuf[b:b + 1, :] = qk[b:b + 1, :] * vrow + cross
    ro = obuf[...]
    mu = jnp.mean(ro, axis=-1, keepdims=True)
    rc = ro - mu
    var = jnp.mean(rc * rc, axis=-1, keepdims=True)
    o_ref[...] = (rc * lax.rsqrt(var + LN_EPS) * gnw_ref[...] * _silu(g_ref[...])).astype(BF16)


def _ret_sample(proj2, cos, sin, gnw, s0):
    nb = proj2.shape[0]
    H = RET_HEADS
    dec = jnp.asarray(np.broadcast_to(np.exp(np.array(RET_LOG_G))[:, None, None], (H, 1, RET_DK)), F32)
    blk = lambda k: pl.BlockSpec((nb, RET_DK), lambda h: (0, k * H + h))
    sblk = pl.BlockSpec((nb, 1, RET_DK, RET_DK), lambda h: (0, h, 0, 0))
    return pl.pallas_call(
        _ret_sample_body,
        grid=(H,),
        in_specs=[blk(0), blk(1), blk(2), blk(3),
                  pl.BlockSpec((8, RET_DK), lambda h: (0, 0)),
                  pl.BlockSpec((8, RET_DK), lambda h: (0, 0)),
                  pl.BlockSpec((1, RET_DK), lambda h: (0, h)),
                  pl.BlockSpec((1, 1, RET_DK), lambda h: (h, 0, 0)),
                  sblk],
        out_specs=[pl.BlockSpec((nb, RET_DK), lambda h: (0, h)), sblk],
        out_shape=[jax.ShapeDtypeStruct((nb, RET_W), BF16),
                   jax.ShapeDtypeStruct(s0.shape, F32)],
        scratch_shapes=[pltpu.VMEM((nb, RET_DK), F32)],
        compiler_params=_params(("arbitrary",)),
        name="ret_sample",
    )(proj2, proj2, proj2, proj2, cos, sin, gnw, dec, s0)


def _gdn_prompt_body(dq_ref, dk_ref, dv_ref, dz_ref, ab_ref, cw_ref, alog_ref, dtb_ref, nw_ref,
                     o_ref, s_ref, conv_ref, xbuf, S):
    c = pl.program_id(1)
    C = GDN_CHUNK
    H = GDN_HEADS
    dk = GDN_DK
    nbt = dq_ref.shape[0]

    @pl.when(c == 0)
    def _():
        S[...] = jnp.zeros_like(S)
        xbuf[:, 0:8, :] = jnp.zeros((nbt, 8, GDN_CONV_C), F32)

    row = lax.broadcasted_iota(jnp.int32, (C, C), 0)
    col = lax.broadcasted_iota(jnp.int32, (C, C), 1)
    tri = row >= col
    strict = row > col
    tril = tri.astype(F32)
    eye = (row == col).astype(F32)
    er = lax.broadcasted_iota(jnp.int32, (128, H * dk), 0)
    ec = lax.broadcasted_iota(jnp.int32, (128, H * dk), 1) // dk
    expand_a = (er == ec).astype(F32)
    expand_b = (er == ec + H).astype(F32)
    lane0 = (lax.broadcasted_iota(jnp.int32, (C, dk), 1) == 0).astype(F32)
    cw = cw_ref[...]

    for bt in range(nbt):
        xbuf[bt, 8:8 + C, 0:GDN_W] = dq_ref[bt]
        xbuf[bt, 8:8 + C, GDN_W:2 * GDN_W] = dk_ref[bt]
        xbuf[bt, 8:8 + C, 2 * GDN_W:3 * GDN_W] = dv_ref[bt]

        def conv(off, bt=bt):
            y = xbuf[bt, 5:5 + C, off:off + dk] * cw[0:1, off:off + dk]
            for i in range(1, CONV_W):
                y = y + xbuf[bt, 5 + i:5 + i + C, off:off + dk] * cw[i:i + 1, off:off + dk]
            return _silu(y)

        ab = ab_ref[bt]
        a_bc = jnp.dot(ab, expand_a, precision=HI, preferred_element_type=F32)
        b_bc = jnp.dot(ab, expand_b, precision=HI, preferred_element_type=F32)
        g_bc = -jnp.exp(alog_ref[...]) * jax.nn.softplus(a_bc + dtb_ref[...])
        gc_bc = jnp.dot(tril, g_bc, precision=HI, preferred_element_type=F32)
        beta_bc = jax.nn.sigmoid(b_bc)

        for h in range(H):
            sl = slice(h * dk, (h + 1) * dk)
            q = conv(h * dk)
            k = conv(GDN_W + h * dk)
            v = conv(2 * GDN_W + h * dk)
            q = q * lax.rsqrt(jnp.sum(q * q, axis=-1, keepdims=True) + L2_EPS) * (dk ** -0.5)
            k = k * lax.rsqrt(jnp.sum(k * k, axis=-1, keepdims=True) + L2_EPS)
            gcb = gc_bc[:, sl]
            beta = beta_bc[:, sl]
            gc_row = lax.dot_general(lane0, gcb, NT, precision=HI, preferred_element_type=F32)
            decay = jnp.exp(jnp.where(tri, gcb[:, 0:C] - gc_row, -jnp.inf))
            kb = k * beta
            a = jnp.where(strict, _bdot_nt(kb, k) * decay, 0.0)
            p = -a
            t = eye + p
            for _ in range(5):
                p = _hdot(p, p)
                t = t + _hdot(p, t)
            egc = jnp.exp(gcb)
            u = _hdot(t, v * beta)
            w = _hdot(t, kb * egc)
            attn = jnp.where(tri, _bdot_nt(q, k) * decay, 0.0)
            gc_last = gcb[C - 1:C, :]
            s_prev = S[bt, h]
            v_new = u - _bdot(w, s_prev)
            o = _bdot(q * egc, s_prev) + _bdot(attn, v_new)
            S[bt, h] = s_prev * jnp.exp(gc_last) + _bdot_tn(k * jnp.exp(gc_last - gcb), v_new)
            o = o * lax.rsqrt(jnp.mean(o * o, axis=-1, keepdims=True) + RMS_EPS) * nw_ref[...]
            o_ref[bt, :, sl] = (o * _silu(dz_ref[bt, :, sl])).astype(BF16)

        xbuf[bt, 0:8, :] = xbuf[bt, C:C + 8, :]

    @pl.when(c == pl.num_programs(1) - 1)
    def _():
        s_ref[...] = S[...]
        conv_ref[...] = xbuf[:, C + 8 - (CONV_W - 1):C + 8, :]


def _gdn_prompt(proj, cw, alog_bc, dtb_bc, nw, nbt):
    B, L, _ = proj.shape
    C = GDN_CHUNK
    H = GDN_HEADS
    blk = lambda k: pl.BlockSpec((nbt, C, GDN_W), lambda b, c: (b, c, k))
    full = lambda shape: pl.BlockSpec(shape, lambda b, c: tuple(0 for _ in shape))
    return pl.pallas_call(
        _gdn_prompt_body,
        grid=(B // nbt, L // C),
        in_specs=[blk(2), blk(3), blk(4), blk(5),
                  pl.BlockSpec((nbt, C, 128), lambda b, c: (b, c, COL_AB // 128)),
                  full((CONV_W, GDN_CONV_C)), full((1, GDN_W)), full((1, GDN_W)), full((1, GDN_DK))],
        out_specs=[pl.BlockSpec((nbt, C, GDN_W), lambda b, c: (b, c, 0)),
                   pl.BlockSpec((nbt, H, GDN_DK, GDN_DK), lambda b, c: (b, 0, 0, 0)),
                   pl.BlockSpec((nbt, CONV_W - 1, GDN_CONV_C), lambda b, c: (b, 0, 0))],
        out_shape=[jax.ShapeDtypeStruct((B, L, GDN_W), BF16),
                   jax.ShapeDtypeStruct((B, H, GDN_DK, GDN_DK), F32),
                   jax.ShapeDtypeStruct((B, CONV_W - 1, GDN_CONV_C), F32)],
        scratch_shapes=[pltpu.VMEM((nbt, C + 8, GDN_CONV_C), F32),
                        pltpu.VMEM((nbt, H, GDN_DK, GDN_DK), F32)],
        compiler_params=_params(("parallel", "arbitrary")),
        name="gdn_prompt",
    )(proj, proj, proj, proj, proj, cw, alog_bc, dtb_bc, nw)


def _gdn_sample_body(dq_ref, dk_ref, dv_ref, dz_ref, ab_ref, csq_ref, csk_ref, csv_ref, cwq_ref, cwk_ref, cwv_ref,
                     alog_ref, dtb_ref, nw_ref, s0_ref, o_ref, s_ref, nq_ref, nk_ref, nv_ref, obuf):
    h = pl.program_id(0)
    nb = dq_ref.shape[0]
    dk = GDN_DK

    def conv(x_ref, cs_ref, w_ref, n_ref):
        x = x_ref[...]
        y = cs_ref[0] * w_ref[0:1, :] + cs_ref[1] * w_ref[1:2, :] + cs_ref[2] * w_ref[2:3, :] + x * w_ref[3:4, :]
        n_ref[0] = cs_ref[1]
        n_ref[1] = cs_ref[2]
        n_ref[2] = x
        return _silu(y)

    q = conv(dq_ref, csq_ref, cwq_ref, nq_ref)
    k = conv(dk_ref, csk_ref, cwk_ref, nk_ref)
    v = conv(dv_ref, csv_ref, cwv_ref, nv_ref)
    q = q * lax.rsqrt(jnp.sum(q * q, axis=-1, keepdims=True) + L2_EPS) * (dk ** -0.5)
    k = k * lax.rsqrt(jnp.sum(k * k, axis=-1, keepdims=True) + L2_EPS)
    er = lax.broadcasted_iota(jnp.int32, (128, dk), 0)
    ab = ab_ref[...]
    a_bc = jnp.dot(ab, (er == h).astype(F32), precision=HI, preferred_element_type=F32)
    b_bc = jnp.dot(ab, (er == h + GDN_HEADS).astype(F32), precision=HI, preferred_element_type=F32)
    g = -jnp.exp(alog_ref[0]) * jax.nn.softplus(a_bc + dtb_ref[0])
    eg = jnp.exp(g)
    beta = jax.nn.sigmoid(b_bc)
    qk = jnp.sum(q * k, axis=1, keepdims=True)
    qT = q.T
    kT = k.T
    for b in range(nb):
        s0 = s0_ref[b, 0]
        qcol = qT[:, b:b + 1]
        kcol = kT[:, b:b + 1]
        ks = jnp.sum(s0 * kcol, axis=0, keepdims=True)
        qs = jnp.sum(s0 * qcol, axis=0, keepdims=True)
        bb = beta[b:b + 1, :]
        egb = eg[b:b + 1, :]
        v_new = bb * v[b:b + 1, :] - bb * egb * ks
        s_ref[b, 0] = s0 * egb + kcol * v_new
        obuf[b:b + 1, :] = egb * qs + qk[b:b + 1, :] * v_new
    o = obuf[...]
    o = o * lax.rsqrt(jnp.mean(o * o, axis=-1, keepdims=True) + RMS_EPS) * nw_ref[...]
    o_ref[...] = (o * _silu(dz_ref[...])).astype(BF16)


def _gdn_sample(proj2, cs_t, cw, alog_t, dtb_t, nw, s0):
    nb = proj2.shape[0]
    H = GDN_HEADS
    dk = GDN_DK
    blk = lambda k: pl.BlockSpec((nb, dk), lambda h: (0, k * H + h))
    csblk = lambda k: pl.BlockSpec((CONV_W - 1, nb, dk), lambda h: (0, 0, k * H + h))
    cwblk = lambda k: pl.BlockSpec((CONV_W, dk), lambda h: (0, k * H + h))
    tab = pl.BlockSpec((1, 1, dk), lambda h: (h, 0, 0))
    sblk = pl.BlockSpec((nb, 1, dk, dk), lambda h: (0, h, 0, 0))
    nblk = pl.BlockSpec((CONV_W - 1, nb, dk), lambda h: (0, 0, h))
    nshape = jax.ShapeDtypeStruct((CONV_W - 1, nb, GDN_W), F32)
    return pl.pallas_call(
        _gdn_sample_body,
        grid=(H,),
        in_specs=[blk(2), blk(3), blk(4), blk(5),
                  pl.BlockSpec((nb, 128), lambda h: (0, COL_AB // 128)),
                  csblk(0), csblk(1), csblk(2), cwblk(0), cwblk(1), cwblk(2),
                  tab, tab, pl.BlockSpec((1, dk), lambda h: (0, 0)), sblk],
        out_specs=[pl.BlockSpec((nb, dk), lambda h: (0, h)), sblk, nblk, nblk, nblk],
        out_shape=[jax.ShapeDtypeStruct((nb, GDN_W), BF16), jax.ShapeDtypeStruct(s0.shape, F32),
                   nshape, nshape, nshape],
        scratch_shapes=[pltpu.VMEM((nb, dk), F32)],
        compiler_params=_params(("arbitrary",)),
        name="gdn_sample",
    )(proj2, proj2, proj2, proj2, proj2, cs_t, cs_t, cs_t, cw, cw, cw, alog_t, dtb_t, nw, s0)


def _ssm_prep_body(are_ref, aim_ref, ldt_ref, bre_ref, bim_ref, lre_ref, lim_ref, obre_ref, obim_ref):
    ar = are_ref[0]
    ai = aim_ref[0]
    dt = jnp.exp(ldt_ref[0])
    er = jnp.exp(ar * dt)
    lr = er * jnp.cos(ai * dt)
    li = er * jnp.sin(ai * dt)
    lre_ref[0] = lr
    lim_ref[0] = li
    xr = lr - 1.0
    den = ar * ar + ai * ai
    cr = ((xr * ar + li * ai) / den)[:, None, :]
    ci = ((li * ar - xr * ai) / den)[:, None, :]
    br = bre_ref[0]
    bi = bim_ref[0]
    obre_ref[0] = cr * br - ci * bi
    obim_ref[0] = cr * bi + ci * br


def _ssm_prep(a_re, a_im, log_dt, b_re, b_im):
    G, N, P = SSM_GROUPS, SSM_N, SSM_GROUP
    ldt = jnp.broadcast_to(log_dt[:, :, None], (DEPTH, G, N))
    bt_re = jnp.transpose(b_re, (0, 1, 3, 2))
    bt_im = jnp.transpose(b_im, (0, 1, 3, 2))
    s2 = pl.BlockSpec((1, G, N), lambda l: (l, 0, 0))
    s3 = pl.BlockSpec((1, G, P, N), lambda l: (l, 0, 0, 0))
    return pl.pallas_call(
        _ssm_prep_body,
        grid=(DEPTH,),
        in_specs=[s2, s2, s2, s3, s3],
        out_specs=[s2, s2, s3, s3],
        out_shape=[jax.ShapeDtypeStruct((DEPTH, G, N), F32), jax.ShapeDtypeStruct((DEPTH, G, N), F32),
                   jax.ShapeDtypeStruct((DEPTH, G, P, N), F32), jax.ShapeDtypeStruct((DEPTH, G, P, N), F32)],
        name="ssm_prep",
    )(a_re, a_im, ldt, bt_re, bt_im)


def _ssm_tile_blockdiag(re, im):
    x = jnp.stack([re, im]).reshape(2, SSM_TILES, SSM_TILE_G, SSM_GROUP, SSM_N)
    eye = jnp.eye(SSM_TILE_G, dtype=x.dtype)
    y = x[:, :, :, :, None, :] * eye[None, None, :, None, :, None]
    y = jnp.transpose(y, (1, 2, 3, 0, 4, 5))
    return y.reshape(SSM_TILES, SSM_TILE_G * SSM_GROUP, SSM_TILE_W)


def _ssm_state_to_tiles(re, im):
    B = re.shape[0]
    x = jnp.stack([re, im], axis=1).reshape(B, 2, SSM_TILES, SSM_HALF)
    return jnp.transpose(x, (0, 2, 1, 3)).reshape(B, SSM_TILES * SSM_TILE_W)


def _ssm_tiles_to_state(x):
    B = x.shape[0]
    x = jnp.transpose(x.reshape(B, SSM_TILES, 2, SSM_HALF), (0, 2, 1, 3)).reshape(B, 2, SSM_GROUPS, SSM_N)
    return x[:, 0], x[:, 1]


def _cmul(ar, ai, br, bi):
    return ar * br - ai * bi, ar * bi + ai * br


def _s5_prompt_body(u_ref, bt_ref, ct_ref, lam_ref, d_ref, y_ref, hl_ref, buf):
    L = u_ref.shape[1]
    nseg = L // SSM_SEG
    ns = SSM_HALF // 128
    bmat = bt_ref[0].astype(BF16)
    for i in range(nseg):
        bu = jnp.dot(u_ref[0, i * SSM_SEG:(i + 1) * SSM_SEG, :].astype(BF16), bmat, preferred_element_type=F32)
        for s in range(2 * ns):
            buf[s, i * SSM_PITCH:i * SSM_PITCH + SSM_SEG, :] = bu[:, s * 128:(s + 1) * 128]

    lam = lam_ref[0]
    a_re = [jnp.broadcast_to(lam[:, s * 128:(s + 1) * 128], (nseg, 128)) for s in range(ns)]
    a_im = [jnp.broadcast_to(lam[:, SSM_HALF + s * 128:SSM_HALF + (s + 1) * 128], (nseg, 128)) for s in range(ns)]

    def scan_step(t, carry):
        out = []
        for s in range(ns):
            hr, hi = carry[2 * s], carry[2 * s + 1]
            pr, pi = _cmul(a_re[s], a_im[s], hr, hi)
            hr = pr + buf[s, pl.ds(t, nseg, stride=SSM_PITCH), :]
            hi = pi + buf[ns + s, pl.ds(t, nseg, stride=SSM_PITCH), :]
            buf[s, pl.ds(t, nseg, stride=SSM_PITCH), :] = hr
            buf[ns + s, pl.ds(t, nseg, stride=SSM_PITCH), :] = hi
            out += [hr, hi]
        return tuple(out)

    zero = jnp.zeros((nseg, 128), F32)
    ends = lax.fori_loop(0, SSM_SEG, scan_step, tuple(zero for _ in range(2 * ns)))

    sub = lax.broadcasted_iota(jnp.int32, (nseg, 128), 0)
    carries = []
    for s in range(ns):
        pr, pi = a_re[s], a_im[s]
        for _ in range(int(math.log2(SSM_SEG))):
            pr, pi = _cmul(pr, pi, pr, pi)
        xr, xi = ends[2 * s], ends[2 * s + 1]
        shift = 1
        while shift < nseg:
            sr = jnp.where(sub >= shift, pltpu.roll(xr, shift, 0), 0.0)
            si = jnp.where(sub >= shift, pltpu.roll(xi, shift, 0), 0.0)
            mr, mi = _cmul(pr, pi, sr, si)
            xr, xi = xr + mr, xi + mi
            pr, pi = _cmul(pr, pi, pr, pi)
            shift *= 2
        hl_ref[0, 0, :, s * 128:(s + 1) * 128] = xr[nseg - 1:nseg, :]
        hl_ref[0, 0, :, SSM_HALF + s * 128:SSM_HALF + (s + 1) * 128] = xi[nseg - 1:nseg, :]
        carries += [jnp.where(sub >= 1, pltpu.roll(xr, 1, 0), 0.0), jnp.where(sub >= 1, pltpu.roll(xi, 1, 0), 0.0)]

    def fix_step(t, pw):
        out = []
        for s in range(ns):
            pr, pi = pw[2 * s], pw[2 * s + 1]
            fr, fi = _cmul(pr, pi, carries[2 * s], carries[2 * s + 1])
            buf[s, pl.ds(t, nseg, stride=SSM_PITCH), :] += fr
            buf[ns + s, pl.ds(t, nseg, stride=SSM_PITCH), :] += fi
            nr, ni = _cmul(pr, pi, a_re[s], a_im[s])
            out += [nr, ni]
        return tuple(out)

    pw0 = []
    for s in range(ns):
        pw0 += [a_re[s], a_im[s]]
    lax.fori_loop(0, SSM_SEG, fix_step, tuple(pw0))

    cmat = ct_ref[0].astype(BF16)
    for i in range(nseg):
        rows = slice(i * SSM_PITCH, i * SSM_PITCH + SSM_SEG)
        y = jnp.zeros((SSM_SEG, 128), F32)
        for s in range(ns):
            y += lax.dot_general(buf[s, rows, :].astype(BF16), cmat[:, s * 128:(s + 1) * 128], NT,
                                 preferred_element_type=F32)
            y -= lax.dot_general(buf[ns + s, rows, :].astype(BF16),
                                 cmat[:, SSM_HALF + s * 128:SSM_HALF + (s + 1) * 128], NT,
                                 preferred_element_type=F32)
        u = u_ref[0, i * SSM_SEG:(i + 1) * SSM_SEG, :]
        y_ref[0, i * SSM_SEG:(i + 1) * SSM_SEG, :] = jax.nn.gelu(y + d_ref[...] * u)


def _s5_prompt(proj, bt, ct, lam, d):
    B, L, _ = proj.shape
    nseg = L // SSM_SEG
    tile = lambda shape: pl.BlockSpec(shape, lambda b, j: (j,) + tuple(0 for _ in shape[1:]))
    return pl.pallas_call(
        _s5_prompt_body,
        grid=(B, SSM_TILES),
        in_specs=[pl.BlockSpec((1, L, 128), lambda b, j: (b, 0, COL_SU // 128 + j)),
                  tile((1, 128, SSM_TILE_W)), tile((1, 128, SSM_TILE_W)), tile((1, 1, SSM_TILE_W)),
                  pl.BlockSpec((1, 128), lambda b, j: (0, j))],
        out_specs=[pl.BlockSpec((1, L, 128), lambda b, j: (b, 0, j)),
                   pl.BlockSpec((1, 1, 1, SSM_TILE_W), lambda b, j: (b, j, 0, 0))],
        out_shape=[jax.ShapeDtypeStruct((B, L, SSM_W), F32),
                   jax.ShapeDtypeStruct((B, SSM_TILES, 1, SSM_TILE_W), F32)],
        scratch_shapes=[pltpu.VMEM((SSM_TILE_W // 128, nseg * SSM_PITCH, 128), F32)],
        compiler_params=_params(("parallel", "parallel")),
        name="s5_prompt",
    )(proj, bt, ct, lam, d)


def _s5_sample_body(u_ref, bt_ref, ct_ref, lam_ref, d_ref, h0_ref, y_ref, h_ref):
    u = u_ref[...]
    bu = jnp.dot(u, bt_ref[0], precision=HI, preferred_element_type=F32)
    lam = lam_ref[0]
    lr, li = lam[:, 0:SSM_HALF], lam[:, SSM_HALF:]
    h0r, h0i = h0_ref[:, 0:SSM_HALF], h0_ref[:, SSM_HALF:]
    pr, pi = _cmul(lr, li, h0r, h0i)
    hr = pr + bu[:, 0:SSM_HALF]
    hi = pi + bu[:, SSM_HALF:]
    h_ref[:, 0:SSM_HALF] = hr
    h_ref[:, SSM_HALF:] = hi
    ct = ct_ref[0]
    y = lax.dot_general(hr, ct[:, 0:SSM_HALF], NT, precision=HI, preferred_element_type=F32)
    y -= lax.dot_general(hi, ct[:, SSM_HALF:], NT, precision=HI, preferred_element_type=F32)
    y_ref[...] = jax.nn.gelu(y + d_ref[...] * u)


def _s5_sample(proj2, bt, ct, lam, d, h0):
    nb = proj2.shape[0]
    tile = lambda shape: pl.BlockSpec(shape, lambda j: (j,) + tuple(0 for _ in shape[1:]))
    hblk = pl.BlockSpec((nb, SSM_TILE_W), lambda j: (0, j))
    return pl.pallas_call(
        _s5_sample_body,
        grid=(SSM_TILES,),
        in_specs=[pl.BlockSpec((nb, 128), lambda j: (0, COL_SU // 128 + j)),
                  tile((1, 128, SSM_TILE_W)), tile((1, 128, SSM_TILE_W)), tile((1, 1, SSM_TILE_W)),
                  pl.BlockSpec((1, 128), lambda j: (0, j)), hblk],
        out_specs=[pl.BlockSpec((nb, 128), lambda j: (0, j)), hblk],
        out_shape=[jax.ShapeDtypeStruct((nb, SSM_W), F32), jax.ShapeDtypeStruct(h0.shape, F32)],
        compiler_params=_params(("parallel",)),
        name="s5_sample",
    )(proj2, bt, ct, lam, d, h0)


def kernel(x_prompt, x_sample, c_prompt, c_sample, state_ret, state_gdn, state_conv, state_ssm_re, state_ssm_im, w_mod, b_mod, w_in, conv_w, ret_gn_w, gdn_a_log, gdn_dt_bias, gdn_norm_w, ssm_a_re, ssm_a_im, ssm_log_dt, ssm_b_re, ssm_b_im, ssm_c_re, ssm_c_im, ssm_d, ssm_w_glu, w_out, ln1_w, ln1_b, w_ffn_in, w_ffn_out, ln2_w, ln2_b):
    Bp, Lp, _ = x_prompt.shape
    Bs = x_sample.shape[0]
    assert x_sample.shape[1] == 1 and Bs % 8 == 0

    pad_rows = (-(Bp + Bs)) % 8
    c_all = jnp.concatenate([c_prompt, c_sample, jnp.zeros((pad_rows, D_MODEL), F32)], axis=0)
    mod_all = _modulation(c_all, w_mod, b_mod)
    lam_re, lam_im, bbar_re, bbar_im = _ssm_prep(ssm_a_re, ssm_a_im, ssm_log_dt, ssm_b_re, ssm_b_im)
    cos_p, sin_p, cos_s, sin_s = _rope_tables(Lp)

    yp = x_prompt
    ys = x_sample.reshape(1, Bs, D_MODEL)
    outs_p, outs_s = [], []
    for l in range(DEPTH):
        w_in_l = jnp.concatenate(
            [w_in[l][:, :COL_SU], w_in[l][:, COL_SU + 2 * GDN_HEADS:], w_in[l][:, COL_SU:COL_SU + 2 * GDN_HEADS],
             jnp.zeros((D_MODEL, N_PROJ - N_IN), F32)], axis=1).astype(BF16)
        w_out_l = w_out[l].astype(BF16)
        w_glu_l = ssm_w_glu[l].astype(BF16)
        w_ffi_l = w_ffn_in[l].astype(BF16)
        w_ffo_l = w_ffn_out[l].astype(BF16)
        mod_p = mod_all[l, :Bp][:, None, :]
        mod_s = mod_all[l, Bp:Bp + Bs][None]
        gnw = ret_gn_w[l][None, :]
        nw = gdn_norm_w[l][None, :]
        alog_bc = jnp.repeat(gdn_a_log[l], GDN_DK)[None, :]
        dtb_bc = jnp.repeat(gdn_dt_bias[l], GDN_DK)[None, :]
        alog_t = jnp.broadcast_to(gdn_a_log[l][:, None, None], (GDN_HEADS, 1, GDN_DK))
        dtb_t = jnp.broadcast_to(gdn_dt_bias[l][:, None, None], (GDN_HEADS, 1, GDN_DK))
        ssm_bt = _ssm_tile_blockdiag(bbar_re[l], bbar_im[l])
        ssm_ct = _ssm_tile_blockdiag(ssm_c_re[l], ssm_c_im[l])
        ssm_lam = jnp.concatenate([lam_re[l].reshape(SSM_TILES, 1, SSM_HALF),
                                   lam_im[l].reshape(SSM_TILES, 1, SSM_HALF)], axis=-1)
        ssm_dl = ssm_d[l][None, :]
        lnw1, lnb1 = ln1_w[l][None, :], ln1_b[l][None, :]
        lnw2, lnb2 = ln2_w[l][None, :], ln2_b[l][None, :]

        proj = _inproj(yp, mod_p, w_in_l, tm=1024, tn=768)
        ro, ret_p = _ret_prompt(proj, cos_p, sin_p, gnw)
        go, gdn_p, conv_p = _gdn_prompt(proj, conv_w[l], alog_bc, dtb_bc, nw, nbt=2)
        so, hl = _s5_prompt(proj, ssm_bt, ssm_ct, ssm_lam, ssm_dl)
        re_p, im_p = _ssm_tiles_to_state(hl.reshape(Bp, SSM_TILES * SSM_TILE_W))
        x1 = _outproj(ro, go, so, yp, mod_p, w_glu_l, w_out_l, lnw1, lnb1, tm=512)
        yp = _ffn(x1, mod_p, w_ffi_l, w_ffo_l, lnw2, lnb2, tm=512, tf=512)
        outs_p.append((ret_p, gdn_p, conv_p, re_p, im_p))

        proj = _inproj(ys, mod_s, w_in_l, tm=Bs, tn=768)
        proj2 = proj.reshape(Bs, N_PROJ)
        ro, ret_s = _ret_sample(proj2, cos_s, sin_s, gnw, state_ret[l])
        cs_t = jnp.transpose(state_conv[l], (1, 0, 2))
        go, gdn_s, nq, nk, nv = _gdn_sample(proj2, cs_t, conv_w[l], alog_t, dtb_t, nw, state_gdn[l])
        conv_s = jnp.transpose(jnp.concatenate([nq, nk, nv], axis=-1), (1, 0, 2))
        h0 = _ssm_state_to_tiles(state_ssm_re[l], state_ssm_im[l])
        so, hn = _s5_sample(proj2, ssm_bt, ssm_ct, ssm_lam, ssm_dl, h0)
        re_s, im_s = _ssm_tiles_to_state(hn)
        x1 = _outproj(ro[None], go[None], so[None], ys, mod_s, w_glu_l, w_out_l, lnw1, lnb1, tm=Bs)
        ys = _ffn(x1, mod_s, w_ffi_l, w_ffo_l, lnw2, lnb2, tm=Bs, tf=512)
        outs_s.append((ret_s, gdn_s, conv_s, re_s, im_s))

    ret_p, gdn_p, conv_p, re_p, im_p = [jnp.stack(t) for t in zip(*outs_p)]
    ret_s, gdn_s, conv_s, re_s, im_s = [jnp.stack(t) for t in zip(*outs_s)]
    return (yp, ys.reshape(Bs, 1, D_MODEL), ret_p, gdn_p, conv_p, re_p, im_p,
            ret_s, gdn_s, conv_s, re_s, im_s)
```

```python
import functools
import math

import numpy as np
import jax
import jax.numpy as jnp
from jax import lax
from jax.experimental import pallas as pl
from jax.experimental.pallas import tpu as pltpu

F32 = jnp.float32
BF16 = jnp.bfloat16

D_MODEL = 2048
DEPTH = 2
PAST_LEN = 16384
RET_HEADS = 4
RET_DK = 128
RET_W = 512
RET_CHUNK = 128
ROPE_BASE = 10000.0
GDN_HEADS = 8
GDN_DK = 128
GDN_W = 1024
GDN_CHUNK = 64
CONV_W = 4
GDN_CONV_C = 3072
SSM_W = 512
SSM_GROUP = 16
SSM_GROUPS = 32
SSM_N = 64
D_FF = 5632
ALPHA = (2 * DEPTH) ** 0.25
LN_EPS = 1e-5
RMS_EPS = 1e-6
L2_EPS = 1e-6

COL_SU = 6144
COL_AB = 6656
N_IN = 6672
N_PROJ = 6912

SSM_TILES = 4
SSM_TILE_G = 8
SSM_HALF = SSM_TILE_G * SSM_N
SSM_TILE_W = 2 * SSM_HALF
SSM_SEG = 256
SSM_PITCH = 260

VMEM_LIMIT = 56 * 1024 * 1024

RET_LOG_G = [math.log1p(-(2.0 ** (-5.0 - h))) for h in range(RET_HEADS)]

NT = (((1,), (1,)), ((), ()))
TN = (((0,), (0,)), ((), ()))
HI = lax.Precision.HIGHEST


def _params(sem):
    return pltpu.CompilerParams(dimension_semantics=sem, vmem_limit_bytes=VMEM_LIMIT)


def _silu(x):
    return x * jax.nn.sigmoid(x)


def _bdot(a, b):
    return jnp.dot(a.astype(BF16), b.astype(BF16), preferred_element_type=F32)


def _hdot(a, b):
    return jnp.dot(a, b, precision=HI, preferred_element_type=F32)


def _bdot_nt(a, b):
    return lax.dot_general(a.astype(BF16), b.astype(BF16), NT, preferred_element_type=F32)


def _bdot_tn(a, b):
    return lax.dot_general(a.astype(BF16), b.astype(BF16), TN, preferred_element_type=F32)


def _layernorm(z, w, b):
    mu = jnp.mean(z, axis=-1, keepdims=True)
    zc = z - mu
    var = jnp.mean(zc * zc, axis=-1, keepdims=True)
    return zc * lax.rsqrt(var + LN_EPS) * w + b


def _mod_body(c_ref, w_ref, b_ref, o_ref):
    c = c_ref[...]
    o_ref[0] = _bdot(_silu(c), w_ref[0]) + b_ref[0]


def _modulation(c_all, w_mod, b_mod):
    rows = c_all.shape[0]
    tn = 1024
    n = 6 * D_MODEL
    return pl.pallas_call(
        _mod_body,
        grid=(DEPTH, n // tn),
        in_specs=[pl.BlockSpec((rows, D_MODEL), lambda l, j: (0, 0)),
                  pl.BlockSpec((1, D_MODEL, tn), lambda l, j: (l, 0, j)),
                  pl.BlockSpec((1, 1, tn), lambda l, j: (l, 0, j))],
        out_specs=pl.BlockSpec((1, rows, tn), lambda l, j: (l, 0, j)),
        out_shape=jax.ShapeDtypeStruct((DEPTH, rows, n), F32),
        compiler_params=_params(("parallel", "parallel")),
        name="modulation",
    )(c_all, w_mod, b_mod.reshape(DEPTH, 1, n))


def _mod_spec(mod3, tm, k, grid_rank):
    per_token = mod3.shape[1] != 1
    lmb = tm if per_token else 1
    if grid_rank == 3:
        return pl.BlockSpec((1, lmb, D_MODEL), lambda b, i, j: (b, i if per_token else 0, k))
    return pl.BlockSpec((1, lmb, D_MODEL), lambda b, i: (b, i if per_token else 0, k))


def _inproj_body(x_ref, sc_ref, sh_ref, w_ref, o_ref, h_ref):
    @pl.when(pl.program_id(2) == 0)
    def _():
        h_ref[...] = (x_ref[0] * (1.0 + sc_ref[0]) + sh_ref[0]).astype(BF16)

    o_ref[0] = jnp.dot(h_ref[...], w_ref[...], preferred_element_type=F32)


def _inproj(x, mod3, w, tm, tn):
    B, L, _ = x.shape
    n = w.shape[1]
    return pl.pallas_call(
        _inproj_body,
        grid=(B, L // tm, n // tn),
        in_specs=[pl.BlockSpec((1, tm, D_MODEL), lambda b, i, j: (b, i, 0)),
                  _mod_spec(mod3, tm, 1, 3),
                  _mod_spec(mod3, tm, 0, 3),
                  pl.BlockSpec((D_MODEL, tn), lambda b, i, j: (0, j))],
        out_specs=pl.BlockSpec((1, tm, tn), lambda b, i, j: (b, i, j)),
        out_shape=jax.ShapeDtypeStruct((B, L, n), F32),
        scratch_shapes=[pltpu.VMEM((tm, D_MODEL), BF16)],
        compiler_params=_params(("parallel", "parallel", "arbitrary")),
        name="inproj",
    )(x, mod3, mod3, w)


def _outproj_body(ro_ref, go_ref, y_ref, x_ref, g1_ref, wglu_ref, wout_ref, lnw_ref, lnb_ref, o_ref):
    y = y_ref[0]
    so = y * jax.nn.sigmoid(_bdot(y, wglu_ref[...]))
    acc = jnp.dot(ro_ref[0], wout_ref[0:RET_W, :], preferred_element_type=F32)
    acc += jnp.dot(go_ref[0], wout_ref[RET_W:RET_W + GDN_W, :], preferred_element_type=F32)
    acc += jnp.dot(so.astype(BF16), wout_ref[RET_W + GDN_W:, :], preferred_element_type=F32)
    z = ALPHA * x_ref[0] + g1_ref[0] * acc
    o_ref[0] = _layernorm(z, lnw_ref[...], lnb_ref[...])


def _outproj(ro, go, y, x, mod3, wglu, wout, lnw, lnb, tm):
    B, L, _ = x.shape
    return pl.pallas_call(
        _outproj_body,
        grid=(B, L // tm),
        in_specs=[pl.BlockSpec((1, tm, RET_W), lambda b, i: (b, i, 0)),
                  pl.BlockSpec((1, tm, GDN_W), lambda b, i: (b, i, 0)),
                  pl.BlockSpec((1, tm, SSM_W), lambda b, i: (b, i, 0)),
                  pl.BlockSpec((1, tm, D_MODEL), lambda b, i: (b, i, 0)),
                  _mod_spec(mod3, tm, 2, 2),
                  pl.BlockSpec((SSM_W, SSM_W), lambda b, i: (0, 0)),
                  pl.BlockSpec((D_MODEL, D_MODEL), lambda b, i: (0, 0)),
                  pl.BlockSpec((1, D_MODEL), lambda b, i: (0, 0)),
                  pl.BlockSpec((1, D_MODEL), lambda b, i: (0, 0))],
        out_specs=pl.BlockSpec((1, tm, D_MODEL), lambda b, i: (b, i, 0)),
        out_shape=jax.ShapeDtypeStruct((B, L, D_MODEL), F32),
        compiler_params=_params(("parallel", "parallel")),
        name="outproj_ln1",
    )(ro, go, y, x, mod3, wglu, wout, lnw, lnb)


def _ffn_body(x_ref, sc_ref, sh_ref, g2_ref, wg_ref, wu_ref, wo_ref, lnw_ref, lnb_ref, o_ref, h_ref, acc_ref):
    f = pl.program_id(2)

    @pl.when(f == 0)
    def _():
        h_ref[...] = (x_ref[0] * (1.0 + sc_ref[0]) + sh_ref[0]).astype(BF16)
        acc_ref[...] = jnp.zeros_like(acc_ref)

    h = h_ref[...]
    gate = jnp.dot(h, wg_ref[...], preferred_element_type=F32)
    up = jnp.dot(h, wu_ref[...], preferred_element_type=F32)
    act = (_silu(gate) * up).astype(BF16)
    acc_ref[...] += jnp.dot(act, wo_ref[...], preferred_element_type=F32)

    @pl.when(f == pl.num_programs(2) - 1)
    def _():
        z = ALPHA * x_ref[0] + g2_ref[0] * acc_ref[...]
        o_ref[0] = _layernorm(z, lnw_ref[...], lnb_ref[...])


def _ffn(x, mod3, w_in, w_out, lnw, lnb, tm, tf):
    B, L, _ = x.shape
    nf = D_FF // tf
    return pl.pallas_call(
        _ffn_body,
        grid=(B, L // tm, nf),
        in_specs=[pl.BlockSpec((1, tm, D_MODEL), lambda b, i, f: (b, i, 0)),
                  _mod_spec(mod3, tm, 4, 3),
                  _mod_spec(mod3, tm, 3, 3),
                  _mod_spec(mod3, tm, 5, 3),
                  pl.BlockSpec((D_MODEL, tf), lambda b, i, f: (0, f)),
                  pl.BlockSpec((D_MODEL, tf), lambda b, i, f: (0, f + nf)),
                  pl.BlockSpec((tf, D_MODEL), lambda b, i, f: (f, 0)),
                  pl.BlockSpec((1, D_MODEL), lambda b, i, f: (0, 0)),
                  pl.BlockSpec((1, D_MODEL), lambda b, i, f: (0, 0))],
        out_specs=pl.BlockSpec((1, tm, D_MODEL), lambda b, i, f: (b, i, 0)),
        out_shape=jax.ShapeDtypeStruct((B, L, D_MODEL), F32),
        scratch_shapes=[pltpu.VMEM((tm, D_MODEL), BF16), pltpu.VMEM((tm, D_MODEL), F32)],
        compiler_params=_params(("parallel", "parallel", "arbitrary")),
        name="ffn_ln2",
    )(x, mod3, mod3, mod3, w_in, w_in, w_out, lnw, lnb)


def _rope_body(cp_ref, sp_ref, cs_ref, ss_ref):
    half = RET_DK // 2
    for c_ref, s_ref, base, step in ((cp_ref, sp_ref, 0, 1), (cs_ref, ss_ref, PAST_LEN, 0)):
        shape = c_ref.shape
        lane = lax.broadcasted_iota(jnp.int32, shape, 1)
        row = lax.broadcasted_iota(jnp.int32, shape, 0)
        freq = jnp.exp(jnp.where(lane < half, lane, lane - half).astype(F32) * (-math.log(ROPE_BASE) / half))
        ang = (row * step + base).astype(F32) * freq
        c_ref[...] = jnp.cos(ang)
        s = jnp.sin(ang)
        s_ref[...] = jnp.where(lane < RET_DK // 2, -s, s)


def _rope_tables(seq):
    return pl.pallas_call(
        _rope_body,
        out_shape=[jax.ShapeDtypeStruct((seq, RET_DK), F32), jax.ShapeDtypeStruct((seq, RET_DK), F32),
                   jax.ShapeDtypeStruct((8, RET_DK), F32), jax.ShapeDtypeStruct((8, RET_DK), F32)],
        name="rope_tables",
    )()


def _rope(x, cos, sin_signed):
    return x * cos + pltpu.roll(x, RET_DK // 2, 1) * sin_signed


def _ret_prompt_body(q_ref, k_ref, v_ref, g_ref, cos_ref, sin_ref, gnw_ref, o_ref, s_ref, S):
    c = pl.program_id(1)
    C = RET_CHUNK

    @pl.when(c == 0)
    def _():
        S[...] = jnp.zeros_like(S)

    cos = cos_ref[...]
    sin = sin_ref[...]
    row = lax.broadcasted_iota(jnp.int32, (C, C), 0).astype(F32)
    col = lax.broadcasted_iota(jnp.int32, (C, C), 1).astype(F32)
    diff = row - col
    for h in range(RET_HEADS):
        lg = RET_LOG_G[h]
        sl = slice(h * RET_DK, (h + 1) * RET_DK)
        q = _rope(q_ref[0, :, sl], cos, sin)
        k = _rope(k_ref[0, :, sl], cos, sin) * (RET_DK ** -0.5)
        v = v_ref[0, :, sl]
        dmask = jnp.exp(jnp.where(diff >= 0, lg * diff, -jnp.inf))
        scores = _bdot_nt(q, k) * dmask
        intra = _bdot(scores, v)
        kv = _bdot_tn(k * jnp.exp(lg * (C - 1 - row)), v)
        s_prev = S[h]
        cross = _bdot(q * jnp.exp(lg * (row + 1.0)), s_prev)
        S[h] = s_prev * math.exp(lg * C) + kv
        ro = intra + cross
        mu = jnp.mean(ro, axis=-1, keepdims=True)
        rc = ro - mu
        var = jnp.mean(rc * rc, axis=-1, keepdims=True)
        rn = rc * lax.rsqrt(var + LN_EPS) * gnw_ref[:, sl] * _silu(g_ref[0, :, sl])
        o_ref[0, :, sl] = rn.astype(BF16)

    @pl.when(c == pl.num_programs(1) - 1)
    def _():
        s_ref[0] = S[...]


def _ret_prompt(proj, cos, sin, gnw):
    B, L, _ = proj.shape
    C = RET_CHUNK
    blk = lambda k: pl.BlockSpec((1, C, RET_W), lambda b, c: (b, c, k))
    return pl.pallas_call(
        _ret_prompt_body,
        grid=(B, L // C),
        in_specs=[blk(0), blk(1), blk(2), blk(3),
                  pl.BlockSpec((C, RET_DK), lambda b, c: (c, 0)),
                  pl.BlockSpec((C, RET_DK), lambda b, c: (c, 0)),
                  pl.BlockSpec((1, RET_W), lambda b, c: (0, 0))],
        out_specs=[pl.BlockSpec((1, C, RET_W), lambda b, c: (b, c, 0)),
                   pl.BlockSpec((1, RET_HEADS, RET_DK, RET_DK), lambda b, c: (b, 0, 0, 0))],
        out_shape=[jax.ShapeDtypeStruct((B, L, RET_W), BF16),
                   jax.ShapeDtypeStruct((B, RET_HEADS, RET_DK, RET_DK), F32)],
        scratch_shapes=[pltpu.VMEM((RET_HEADS, RET_DK, RET_DK), F32)],
        compiler_params=_params(("parallel", "arbitrary")),
        name="ret_prompt",
    )(proj, proj, proj, proj, cos, sin, gnw)


def _ret_sample_body(q_ref, k_ref, v_ref, g_ref, cos_ref, sin_ref, gnw_ref, dec_ref, s0_ref, o_ref, s_ref, obuf):
    nb = q_ref.shape[0]
    cos = cos_ref[0:1, :]
    sin = sin_ref[0:1, :]
    gdec = dec_ref[0]
    q = _rope(q_ref[...], cos, sin)
    k = _rope(k_ref[...], cos, sin) * (RET_DK ** -0.5)
    v = v_ref[...]
    qk = jnp.sum(q * k, axis=1, keepdims=True)
    qT = q.T
    kT = k.T
    for b in range(nb):
        s0 = s0_ref[b, 0]
        qcol = qT[:, b:b + 1]
        kcol = kT[:, b:b + 1]
        vrow = v[b:b + 1, :]
        cross = gdec * jnp.sum(s0 * qcol, axis=0, keepdims=True)
        s_ref[b, 0] = s0 * gdec + kcol * vrow
        obuf[b:b + 1, :] = qk[b:b + 1, :] * vrow + cross
    ro = obuf[...]
    mu = jnp.mean(ro, axis=-1, keepdims=True)
    rc = ro - mu
    var = jnp.mean(rc * rc, axis=-1, keepdims=True)
    o_ref[...] = (rc * lax.rsqrt(var + LN_EPS) * gnw_ref[...] * _silu(g_ref[...])).astype(BF16)


def _ret_sample(proj2, cos, sin, gnw, s0):
    nb = proj2.shape[0]
    H = RET_HEADS
    dec = jnp.asarray(np.broadcast_to(np.exp(np.array(RET_LOG_G))[:, None, None], (H, 1, RET_DK)), F32)
    blk = lambda k: pl.BlockSpec((nb, RET_DK), lambda h: (0, k * H + h))
    sblk = pl.BlockSpec((nb, 1, RET_DK, RET_DK), lambda h: (0, h, 0, 0))
    return pl.pallas_call(
        _ret_sample_body,
        grid=(H,),
        in_specs=[blk(0), blk(1), blk(2), blk(3),
                  pl.BlockSpec((8, RET_DK), lambda h: (0, 0)),
                  pl.BlockSpec((8, RET_DK), lambda h: (0, 0)),
                  pl.BlockSpec((1, RET_DK), lambda h: (0, h)),
                  pl.BlockSpec((1, 1, RET_DK), lambda h: (h, 0, 0)),
                  sblk],
        out_specs=[pl.BlockSpec((nb, RET_DK), lambda h: (0, h)), sblk],
        out_shape=[jax.ShapeDtypeStruct((nb, RET_W), BF16),
                   jax.ShapeDtypeStruct(s0.shape, F32)],
        scratch_shapes=[pltpu.VMEM((nb, RET_DK), F32)],
        compiler_params=_params(("arbitrary",)),
        name="ret_sample",
    )(proj2, proj2, proj2, proj2, cos, sin, gnw, dec, s0)


def _gdn_prompt_body(dq_ref, dk_ref, dv_ref, dz_ref, ab_ref, cw_ref, alog_ref, dtb_ref, nw_ref,
                     o_ref, s_ref, conv_ref, xbuf, S):
    c = pl.program_id(1)
    C = GDN_CHUNK
    H = GDN_HEADS
    dk = GDN_DK
    nbt = dq_ref.shape[0]

    @pl.when(c == 0)
    def _():
        S[...] = jnp.zeros_like(S)
        xbuf[:, 0:8, :] = jnp.zeros((nbt, 8, GDN_CONV_C), F32)

    row = lax.broadcasted_iota(jnp.int32, (C, dk), 0)
    lane = lax.broadcasted_iota(jnp.int32, (C, dk), 1)
    tri = lane <= row
    strict = lane < row
    right = lane >= C
    eye_right = (lane == row + C).astype(F32)
    rt = lax.broadcasted_iota(jnp.int32, (C, H * dk), 0)
    jt = lax.broadcasted_iota(jnp.int32, (C, H * dk), 1) % dk
    upper = (rt <= jnp.where(jt < C, jt, -1)).astype(F32)
    r3 = lax.broadcasted_iota(jnp.int32, (C, 3 * C), 0)
    c3 = lax.broadcasted_iota(jnp.int32, (C, 3 * C), 1) % C
    tril_b = jnp.where(r3 >= c3, 1.0, 0.0).astype(BF16)
    ones_b = jnp.ones((C, 3 * C), BF16)

    def cumsum_dot(m3, x):
        hi = x.astype(BF16)
        r1 = x - hi.astype(F32)
        mid = r1.astype(BF16)
        lo = (r1 - mid.astype(F32)).astype(BF16)
        return jnp.dot(m3, jnp.concatenate([hi, mid, lo], axis=0), preferred_element_type=F32)

    cw = cw_ref[...]
    zrow_b = jnp.zeros((C, dk), BF16)

    def split(x):
        hi = x.astype(BF16)
        return hi, (x - hi.astype(F32)).astype(BF16)

    items = []
    for bt in range(nbt):
        xbuf[bt, 8:8 + C, 0:GDN_W] = dq_ref[bt]
        xbuf[bt, 8:8 + C, GDN_W:2 * GDN_W] = dk_ref[bt]
        xbuf[bt, 8:8 + C, 2 * GDN_W:3 * GDN_W] = dv_ref[bt]

        def conv(off, bt=bt):
            y = xbuf[bt, 5:5 + C, off:off + dk] * cw[0:1, off:off + dk]
            for i in range(1, CONV_W):
                y = y + xbuf[bt, 5 + i:5 + i + C, off:off + dk] * cw[i:i + 1, off:off + dk]
            return _silu(y)

        ab = ab_ref[bt]
        a_bc = jnp.concatenate([jnp.broadcast_to(ab[:, h:h + 1], (C, dk)) for h in range(H)], axis=1)
        b_bc = jnp.concatenate([jnp.broadcast_to(ab[:, H + h:H + h + 1], (C, dk)) for h in range(H)], axis=1)
        g_bc = -jnp.exp(alog_ref[...]) * jax.nn.softplus(a_bc + dtb_ref[...])
        gc_col = cumsum_dot(tril_b, g_bc)
        gc_row = cumsum_dot(ones_b, g_bc * upper)
        dfull = gc_col - gc_row
        egc_all = jnp.exp(gc_col)
        beta_bc = jax.nn.sigmoid(b_bc)

        for h in range(H):
            sl = slice(h * dk, (h + 1) * dk)
            q = conv(h * dk)
            k = conv(GDN_W + h * dk)
            v = conv(2 * GDN_W + h * dk)
            q = q * lax.rsqrt(jnp.sum(q * q, axis=-1, keepdims=True) + L2_EPS) * (dk ** -0.5)
            k = k * lax.rsqrt(jnp.sum(k * k, axis=-1, keepdims=True) + L2_EPS)
            gcb = gc_col[:, sl]
            beta = beta_bc[:, sl]
            egc = egc_all[:, sl]
            kb = k * beta
            kq = jnp.concatenate([kb, q], axis=0).astype(BF16)
            k2 = jnp.concatenate([k.astype(BF16), zrow_b], axis=0)
            kkqk = lax.dot_general(kq, k2, NT, preferred_element_type=F32)
            decay = jnp.exp(jnp.where(tri, dfull[:, sl], -jnp.inf))
            w0 = jnp.where(strict, -(kkqk[0:C] * decay), eye_right)
            attn = (kkqk[C:2 * C] * decay)[:, 0:C].astype(BF16)
            rhs = jnp.concatenate([v * beta, kb * egc], axis=1)
            gl = gcb[C - 1:C, :]
            items.append(dict(bt=bt, h=h, w=w0, rhs=rhs, attn=attn, qd=(q * egc).astype(BF16),
                              kd=(k * jnp.exp(gl - gcb)).astype(BF16), egl=jnp.exp(gl)))

    def stacked_lhs(w):
        hi, lo = split(w)
        return jnp.concatenate([hi, lo, hi], axis=1)

    def stacked_rhs(top, bottom_zero):
        hi, lo = split(top)
        z = jnp.zeros_like(hi)
        parts = []
        for piece in (hi, hi, lo):
            parts += [piece, z] if bottom_zero else [z, piece]
        return jnp.concatenate(parts, axis=0)

    for _ in range(6):
        for it in items:
            w = it["w"]
            it["w"] = (jnp.dot(stacked_lhs(w), stacked_rhs(w, True), preferred_element_type=F32)
                       + jnp.where(right, w, 0.0))
    for it in items:
        it["sol"] = jnp.dot(stacked_lhs(it["w"]), stacked_rhs(it["rhs"], False), preferred_element_type=F32)

    for it in items:
        bt, h = it["bt"], it["h"]
        sl = slice(h * dk, (h + 1) * dk)
        u = it["sol"][:, 0:dk]
        w = it["sol"][:, dk:2 * dk]
        s_prev = S[bt, h]
        wq = jnp.concatenate([w.astype(BF16), it["qd"]], axis=0)
        r = jnp.dot(wq, s_prev.astype(BF16), preferred_element_type=F32)
        v_new = u - r[0:C]
        o = r[C:2 * C] + jnp.dot(it["attn"], v_new.astype(BF16), preferred_element_type=F32)
        S[bt, h] = s_prev * it["egl"] + lax.dot_general(it["kd"], v_new.astype(BF16), TN, preferred_element_type=F32)
        o = o * lax.rsqrt(jnp.mean(o * o, axis=-1, keepdims=True) + RMS_EPS) * nw_ref[...]
        o_ref[bt, :, sl] = (o * _silu(dz_ref[bt, :, sl])).astype(BF16)

    for bt in range(nbt):
        xbuf[bt, 0:8, :] = xbuf[bt, C:C + 8, :]

    @pl.when(c == pl.num_programs(1) - 1)
    def _():
        s_ref[...] = S[...]
        conv_ref[...] = xbuf[:, C + 8 - (CONV_W - 1):C + 8, :]


def _gdn_prompt(proj, cw, alog_bc, dtb_bc, nw, nbt):
    B, L, _ = proj.shape
    C = GDN_CHUNK
    H = GDN_HEADS
    blk = lambda k: pl.BlockSpec((nbt, C, GDN_W), lambda b, c: (b, c, k))
    full = lambda shape: pl.BlockSpec(shape, lambda b, c: tuple(0 for _ in shape))
    return pl.pallas_call(
        _gdn_prompt_body,
        grid=(B // nbt, L // C),
        in_specs=[blk(2), blk(3), blk(4), blk(5),
                  pl.BlockSpec((nbt, C, 128), lambda b, c: (b, c, COL_AB // 128)),
                  full((CONV_W, GDN_CONV_C)), full((1, GDN_W)), full((1, GDN_W)), full((1, GDN_DK))],
        out_specs=[pl.BlockSpec((nbt, C, GDN_W), lambda b, c: (b, c, 0)),
                   pl.BlockSpec((nbt, H, GDN_DK, GDN_DK), lambda b, c: (b, 0, 0, 0)),
                   pl.BlockSpec((nbt, CONV_W - 1, GDN_CONV_C), lambda b, c: (b, 0, 0))],
        out_shape=[jax.ShapeDtypeStruct((B, L, GDN_W), BF16),
                   jax.ShapeDtypeStruct((B, H, GDN_DK, GDN_DK), F32),
                   jax.ShapeDtypeStruct((B, CONV_W - 1, GDN_CONV_C), F32)],
        scratch_shapes=[pltpu.VMEM((nbt, C + 8, GDN_CONV_C), F32),
                        pltpu.VMEM((nbt, H, GDN_DK, GDN_DK), F32)],
        compiler_params=_params(("parallel", "arbitrary")),
        name="gdn_prompt",
    )(proj, proj, proj, proj, proj, cw, alog_bc, dtb_bc, nw)


def _gdn_sample_body(dq_ref, dk_ref, dv_ref, dz_ref, ab_ref, csq_ref, csk_ref, csv_ref, cwq_ref, cwk_ref, cwv_ref,
                     alog_ref, dtb_ref, nw_ref, s0_ref, o_ref, s_ref, nq_ref, nk_ref, nv_ref, obuf):
    h = pl.program_id(0)
    nb = dq_ref.shape[0]
    dk = GDN_DK

    def conv(x_ref, cs_ref, w_ref, n_ref):
        x = x_ref[...]
        y = cs_ref[0] * w_ref[0:1, :] + cs_ref[1] * w_ref[1:2, :] + cs_ref[2] * w_ref[2:3, :] + x * w_ref[3:4, :]
        n_ref[0] = cs_ref[1]
        n_ref[1] = cs_ref[2]
        n_ref[2] = x
        return _silu(y)

    q = conv(dq_ref, csq_ref, cwq_ref, nq_ref)
    k = conv(dk_ref, csk_ref, cwk_ref, nk_ref)
    v = conv(dv_ref, csv_ref, cwv_ref, nv_ref)
    q = q * lax.rsqrt(jnp.sum(q * q, axis=-1, keepdims=True) + L2_EPS) * (dk ** -0.5)
    k = k * lax.rsqrt(jnp.sum(k * k, axis=-1, keepdims=True) + L2_EPS)
    er = lax.broadcasted_iota(jnp.int32, (128, dk), 0)
    ab = ab_ref[...]
    a_bc = jnp.dot(ab, (er == h).astype(F32), precision=HI, preferred_element_type=F32)
    b_bc = jnp.dot(ab, (er == h + GDN_HEADS).astype(F32), precision=HI, preferred_element_type=F32)
    g = -jnp.exp(alog_ref[0]) * jax.nn.softplus(a_bc + dtb_ref[0])
    eg = jnp.exp(g)
    beta = jax.nn.sigmoid(b_bc)
    qk = jnp.sum(q * k, axis=1, keepdims=True)
    qT = q.T
    kT = k.T
    for b in range(nb):
        s0 = s0_ref[b, 0]
        qcol = qT[:, b:b + 1]
        kcol = kT[:, b:b + 1]
        ks = jnp.sum(s0 * kcol, axis=0, keepdims=True)
        qs = jnp.sum(s0 * qcol, axis=0, keepdims=True)
        bb = beta[b:b + 1, :]
        egb = eg[b:b + 1, :]
        v_new = bb * v[b:b + 1, :] - bb * egb * ks
        s_ref[b, 0] = s0 * egb + kcol * v_new
        obuf[b:b + 1, :] = egb * qs + qk[b:b + 1, :] * v_new
    o = obuf[...]
    o = o * lax.rsqrt(jnp.mean(o * o, axis=-1, keepdims=True) + RMS_EPS) * nw_ref[...]
    o_ref[...] = (o * _silu(dz_ref[...])).astype(BF16)


def _gdn_sample(proj2, cs_t, cw, alog_t, dtb_t, nw, s0):
    nb = proj2.shape[0]
    H = GDN_HEADS
    dk = GDN_DK
    blk = lambda k: pl.BlockSpec((nb, dk), lambda h: (0, k * H + h))
    csblk = lambda k: pl.BlockSpec((CONV_W - 1, nb, dk), lambda h: (0, 0, k * H + h))
    cwblk = lambda k: pl.BlockSpec((CONV_W, dk), lambda h: (0, k * H + h))
    tab = pl.BlockSpec((1, 1, dk), lambda h: (h, 0, 0))
    sblk = pl.BlockSpec((nb, 1, dk, dk), lambda h: (0, h, 0, 0))
    nblk = pl.BlockSpec((CONV_W - 1, nb, dk), lambda h: (0, 0, h))
    nshape = jax.ShapeDtypeStruct((CONV_W - 1, nb, GDN_W), F32)
    return pl.pallas_call(
        _gdn_sample_body,
        grid=(H,),
        in_specs=[blk(2), blk(3), blk(4), blk(5),
                  pl.BlockSpec((nb, 128), lambda h: (0, COL_AB // 128)),
                  csblk(0), csblk(1), csblk(2), cwblk(0), cwblk(1), cwblk(2),
                  tab, tab, pl.BlockSpec((1, dk), lambda h: (0, 0)), sblk],
        out_specs=[pl.BlockSpec((nb, dk), lambda h: (0, h)), sblk, nblk, nblk, nblk],
        out_shape=[jax.ShapeDtypeStruct((nb, GDN_W), BF16), jax.ShapeDtypeStruct(s0.shape, F32),
                   nshape, nshape, nshape],
        scratch_shapes=[pltpu.VMEM((nb, dk), F32)],
        compiler_params=_params(("arbitrary",)),
        name="gdn_sample",
    )(proj2, proj2, proj2, proj2, proj2, cs_t, cs_t, cs_t, cw, cw, cw, alog_t, dtb_t, nw, s0)


def _ssm_prep_body(are_ref, aim_ref, ldt_ref, bre_ref, bim_ref, lre_ref, lim_ref, obre_ref, obim_ref):
    ar = are_ref[0]
    ai = aim_ref[0]
    dt = jnp.exp(ldt_ref[0])
    er = jnp.exp(ar * dt)
    lr = er * jnp.cos(ai * dt)
    li = er * jnp.sin(ai * dt)
    lre_ref[0] = lr
    lim_ref[0] = li
    xr = lr - 1.0
    den = ar * ar + ai * ai
    cr = ((xr * ar + li * ai) / den)[:, None, :]
    ci = ((li * ar - xr * ai) / den)[:, None, :]
    br = bre_ref[0]
    bi = bim_ref[0]
    obre_ref[0] = cr * br - ci * bi
    obim_ref[0] = cr * bi + ci * br


def _ssm_prep(a_re, a_im, log_dt, b_re, b_im):
    G, N, P = SSM_GROUPS, SSM_N, SSM_GROUP
    ldt = jnp.broadcast_to(log_dt[:, :, None], (DEPTH, G, N))
    bt_re = jnp.transpose(b_re, (0, 1, 3, 2))
    bt_im = jnp.transpose(b_im, (0, 1, 3, 2))
    s2 = pl.BlockSpec((1, G, N), lambda l: (l, 0, 0))
    s3 = pl.BlockSpec((1, G, P, N), lambda l: (l, 0, 0, 0))
    return pl.pallas_call(
        _ssm_prep_body,
        grid=(DEPTH,),
        in_specs=[s2, s2, s2, s3, s3],
        out_specs=[s2, s2, s3, s3],
        out_shape=[jax.ShapeDtypeStruct((DEPTH, G, N), F32), jax.ShapeDtypeStruct((DEPTH, G, N), F32),
                   jax.ShapeDtypeStruct((DEPTH, G, P, N), F32), jax.ShapeDtypeStruct((DEPTH, G, P, N), F32)],
        name="ssm_prep",
    )(a_re, a_im, ldt, bt_re, bt_im)


def _ssm_tile_blockdiag(re, im):
    x = jnp.stack([re, im]).reshape(2, SSM_TILES, SSM_TILE_G, SSM_GROUP, SSM_N)
    eye = jnp.eye(SSM_TILE_G, dtype=x.dtype)
    y = x[:, :, :, :, None, :] * eye[None, None, :, None, :, None]
    y = jnp.transpose(y, (1, 2, 3, 0, 4, 5))
    return y.reshape(SSM_TILES, SSM_TILE_G * SSM_GROUP, SSM_TILE_W)


def _ssm_state_to_tiles(re, im):
    B = re.shape[0]
    x = jnp.stack([re, im], axis=1).reshape(B, 2, SSM_TILES, SSM_HALF)
    return jnp.transpose(x, (0, 2, 1, 3)).reshape(B, SSM_TILES * SSM_TILE_W)


def _ssm_tiles_to_state(x):
    B = x.shape[0]
    x = jnp.transpose(x.reshape(B, SSM_TILES, 2, SSM_HALF), (0, 2, 1, 3)).reshape(B, 2, SSM_GROUPS, SSM_N)
    return x[:, 0], x[:, 1]


def _cmul(ar, ai, br, bi):
    return ar * br - ai * bi, ar * bi + ai * br


def _s5_prompt_body(u_ref, bt_ref, ct_ref, lam_ref, d_ref, y_ref, hl_ref, buf):
    L = u_ref.shape[1]
    nseg = L // SSM_SEG
    ns = SSM_HALF // 128
    bmat = bt_ref[0].astype(BF16)
    for i in range(nseg):
        bu = jnp.dot(u_ref[0, i * SSM_SEG:(i + 1) * SSM_SEG, :].astype(BF16), bmat, preferred_element_type=F32)
        for s in range(2 * ns):
            buf[s, i * SSM_PITCH:i * SSM_PITCH + SSM_SEG, :] = bu[:, s * 128:(s + 1) * 128]

    lam = lam_ref[0]
    a_re = [jnp.broadcast_to(lam[:, s * 128:(s + 1) * 128], (nseg, 128)) for s in range(ns)]
    a_im = [jnp.broadcast_to(lam[:, SSM_HALF + s * 128:SSM_HALF + (s + 1) * 128], (nseg, 128)) for s in range(ns)]

    def scan_step(t, carry):
        out = []
        for s in range(ns):
            hr, hi = carry[2 * s], carry[2 * s + 1]
            pr, pi = _cmul(a_re[s], a_im[s], hr, hi)
            hr = pr + buf[s, pl.ds(t, nseg, stride=SSM_PITCH), :]
            hi = pi + buf[ns + s, pl.ds(t, nseg, stride=SSM_PITCH), :]
            buf[s, pl.ds(t, nseg, stride=SSM_PITCH), :] = hr
            buf[ns + s, pl.ds(t, nseg, stride=SSM_PITCH), :] = hi
            out += [hr, hi]
        return tuple(out)

    zero = jnp.zeros((nseg, 128), F32)
    ends = lax.fori_loop(0, SSM_SEG, scan_step, tuple(zero for _ in range(2 * ns)))

    sub = lax.broadcasted_iota(jnp.int32, (nseg, 128), 0)
    carries = []
    for s in range(ns):
        pr, pi = a_re[s], a_im[s]
        for _ in range(int(math.log2(SSM_SEG))):
            pr, pi = _cmul(pr, pi, pr, pi)
        xr, xi = ends[2 * s], ends[2 * s + 1]
        shift = 1
        while shift < nseg:
            sr = jnp.where(sub >= shift, pltpu.roll(xr, shift, 0), 0.0)
            si = jnp.where(sub >= shift, pltpu.roll(xi, shift, 0), 0.0)
            mr, mi = _cmul(pr, pi, sr, si)
            xr, xi = xr + mr, xi + mi
            pr, pi = _cmul(pr, pi, pr, pi)
            shift *= 2
        hl_ref[0, 0, :, s * 128:(s + 1) * 128] = xr[nseg - 1:nseg, :]
        hl_ref[0, 0, :, SSM_HALF + s * 128:SSM_HALF + (s + 1) * 128] = xi[nseg - 1:nseg, :]
        carries += [jnp.where(sub >= 1, pltpu.roll(xr, 1, 0), 0.0), jnp.where(sub >= 1, pltpu.roll(xi, 1, 0), 0.0)]

    def fix_step(t, pw):
        out = []
        for s in range(ns):
            pr, pi = pw[2 * s], pw[2 * s + 1]
            fr, fi = _cmul(pr, pi, carries[2 * s], carries[2 * s + 1])
            buf[s, pl.ds(t, nseg, stride=SSM_PITCH), :] += fr
            buf[ns + s, pl.ds(t, nseg, stride=SSM_PITCH), :] += fi
            nr, ni = _cmul(pr, pi, a_re[s], a_im[s])
            out += [nr, ni]
        return tuple(out)

    pw0 = []
    for s in range(ns):
        pw0 += [a_re[s], a_im[s]]
    lax.fori_loop(0, SSM_SEG, fix_step, tuple(pw0))

    cmat = ct_ref[0].astype(BF16)
    for i in range(nseg):
        rows = slice(i * SSM_PITCH, i * SSM_PITCH + SSM_SEG)
        y = jnp.zeros((SSM_SEG, 128), F32)
        for s in range(ns):
            y += lax.dot_general(buf[s, rows, :].astype(BF16), cmat[:, s * 128:(s + 1) * 128], NT,
                                 preferred_element_type=F32)
            y -= lax.dot_general(buf[ns + s, rows, :].astype(BF16),
                                 cmat[:, SSM_HALF + s * 128:SSM_HALF + (s + 1) * 128], NT,
                                 preferred_element_type=F32)
        u = u_ref[0, i * SSM_SEG:(i + 1) * SSM_SEG, :]
        y_ref[0, i * SSM_SEG:(i + 1) * SSM_SEG, :] = jax.nn.gelu(y + d_ref[...] * u)


def _s5_prompt(proj, bt, ct, lam, d):
    B, L, _ = proj.shape
    nseg = L // SSM_SEG
    tile = lambda shape: pl.BlockSpec(shape, lambda b, j: (j,) + tuple(0 for _ in shape[1:]))
    return pl.pallas_call(
        _s5_prompt_body,
        grid=(B, SSM_TILES),
        in_specs=[pl.BlockSpec((1, L, 128), lambda b, j: (b, 0, COL_SU // 128 + j)),
                  tile((1, 128, SSM_TILE_W)), tile((1, 128, SSM_TILE_W)), tile((1, 1, SSM_TILE_W)),
                  pl.BlockSpec((1, 128), lambda b, j: (0, j))],
        out_specs=[pl.BlockSpec((1, L, 128), lambda b, j: (b, 0, j)),
                   pl.BlockSpec((1, 1, 1, SSM_TILE_W), lambda b, j: (b, j, 0, 0))],
        out_shape=[jax.ShapeDtypeStruct((B, L, SSM_W), F32),
                   jax.ShapeDtypeStruct((B, SSM_TILES, 1, SSM_TILE_W), F32)],
        scratch_shapes=[pltpu.VMEM((SSM_TILE_W // 128, nseg * SSM_PITCH, 128), F32)],
        compiler_params=_params(("parallel", "parallel")),
        name="s5_prompt",
    )(proj, bt, ct, lam, d)


def _s5_sample_body(u_ref, bt_ref, ct_ref, lam_ref, d_ref, h0_ref, y_ref, h_ref):
    u = u_ref[...]
    bu = jnp.dot(u, bt_ref[0], precision=HI, preferred_element_type=F32)
    lam = lam_ref[0]
    lr, li = lam[:, 0:SSM_HALF], lam[:, SSM_HALF:]
    h0r, h0i = h0_ref[:, 0:SSM_HALF], h0_ref[:, SSM_HALF:]
    pr, pi = _cmul(lr, li, h0r, h0i)
    hr = pr + bu[:, 0:SSM_HALF]
    hi = pi + bu[:, SSM_HALF:]
    h_ref[:, 0:SSM_HALF] = hr
    h_ref[:, SSM_HALF:] = hi
    ct = ct_ref[0]
    y = lax.dot_general(hr, ct[:, 0:SSM_HALF], NT, precision=HI, preferred_element_type=F32)
    y -= lax.dot_general(hi, ct[:, SSM_HALF:], NT, precision=HI, preferred_element_type=F32)
    y_ref[...] = jax.nn.gelu(y + d_ref[...] * u)


def _s5_sample(proj2, bt, ct, lam, d, h0):
    nb = proj2.shape[0]
    tile = lambda shape: pl.BlockSpec(shape, lambda j: (j,) + tuple(0 for _ in shape[1:]))
    hblk = pl.BlockSpec((nb, SSM_TILE_W), lambda j: (0, j))
    return pl.pallas_call(
        _s5_sample_body,
        grid=(SSM_TILES,),
        in_specs=[pl.BlockSpec((nb, 128), lambda j: (0, COL_SU // 128 + j)),
                  tile((1, 128, SSM_TILE_W)), tile((1, 128, SSM_TILE_W)), tile((1, 1, SSM_TILE_W)),
                  pl.BlockSpec((1, 128), lambda j: (0, j)), hblk],
        out_specs=[pl.BlockSpec((nb, 128), lambda j: (0, j)), hblk],
        out_shape=[jax.ShapeDtypeStruct((nb, SSM_W), F32), jax.ShapeDtypeStruct(h0.shape, F32)],
        compiler_params=_params(("parallel",)),
        name="s5_sample",
    )(proj2, bt, ct, lam, d, h0)


def kernel(x_prompt, x_sample, c_prompt, c_sample, state_ret, state_gdn, state_conv, state_ssm_re, state_ssm_im, w_mod, b_mod, w_in, conv_w, ret_gn_w, gdn_a_log, gdn_dt_bias, gdn_norm_w, ssm_a_re, ssm_a_im, ssm_log_dt, ssm_b_re, ssm_b_im, ssm_c_re, ssm_c_im, ssm_d, ssm_w_glu, w_out, ln1_w, ln1_b, w_ffn_in, w_ffn_out, ln2_w, ln2_b):
    Bp, Lp, _ = x_prompt.shape
    Bs = x_sample.shape[0]
    assert x_sample.shape[1] == 1 and Bs % 8 == 0

    pad_rows = (-(Bp + Bs)) % 8
    c_all = jnp.concatenate([c_prompt, c_sample, jnp.zeros((pad_rows, D_MODEL), F32)], axis=0)
    mod_all = _modulation(c_all, w_mod, b_mod)
    lam_re, lam_im, bbar_re, bbar_im = _ssm_prep(ssm_a_re, ssm_a_im, ssm_log_dt, ssm_b_re, ssm_b_im)
    cos_p, sin_p, cos_s, sin_s = _rope_tables(Lp)

    yp = x_prompt
    ys = x_sample.reshape(1, Bs, D_MODEL)
    outs_p, outs_s = [], []
    for l in range(DEPTH):
        w_in_l = jnp.concatenate(
            [w_in[l][:, :COL_SU], w_in[l][:, COL_SU + 2 * GDN_HEADS:], w_in[l][:, COL_SU:COL_SU + 2 * GDN_HEADS],
             jnp.zeros((D_MODEL, N_PROJ - N_IN), F32)], axis=1).astype(BF16)
        w_out_l = w_out[l].astype(BF16)
        w_glu_l = ssm_w_glu[l].astype(BF16)
        w_ffi_l = w_ffn_in[l].astype(BF16)
        w_ffo_l = w_ffn_out[l].astype(BF16)
        mod_p = mod_all[l, :Bp][:, None, :]
        mod_s = mod_all[l, Bp:Bp + Bs][None]
        gnw = ret_gn_w[l][None, :]
        nw = gdn_norm_w[l][None, :]
        alog_bc = jnp.repeat(gdn_a_log[l], GDN_DK)[None, :]
        dtb_bc = jnp.repeat(gdn_dt_bias[l], GDN_DK)[None, :]
        alog_t = jnp.broadcast_to(gdn_a_log[l][:, None, None], (GDN_HEADS, 1, GDN_DK))
        dtb_t = jnp.broadcast_to(gdn_dt_bias[l][:, None, None], (GDN_HEADS, 1, GDN_DK))
        ssm_bt = _ssm_tile_blockdiag(bbar_re[l], bbar_im[l])
        ssm_ct = _ssm_tile_blockdiag(ssm_c_re[l], ssm_c_im[l])
        ssm_lam = jnp.concatenate([lam_re[l].reshape(SSM_TILES, 1, SSM_HALF),
                                   lam_im[l].reshape(SSM_TILES, 1, SSM_HALF)], axis=-1)
        ssm_dl = ssm_d[l][None, :]
        lnw1, lnb1 = ln1_w[l][None, :], ln1_b[l][None, :]
        lnw2, lnb2 = ln2_w[l][None, :], ln2_b[l][None, :]

        proj = _inproj(yp, mod_p, w_in_l, tm=1024, tn=768)
        ro, ret_p = _ret_prompt(proj, cos_p, sin_p, gnw)
        go, gdn_p, conv_p = _gdn_prompt(proj, conv_w[l], alog_bc, dtb_bc, nw, nbt=2)
        so, hl = _s5_prompt(proj, ssm_bt, ssm_ct, ssm_lam, ssm_dl)
        re_p, im_p = _ssm_tiles_to_state(hl.reshape(Bp, SSM_TILES * SSM_TILE_W))
        x1 = _outproj(ro, go, so, yp, mod_p, w_glu_l, w_out_l, lnw1, lnb1, tm=512)
        yp = _ffn(x1, mod_p, w_ffi_l, w_ffo_l, lnw2, lnb2, tm=512, tf=512)
        outs_p.append((ret_p, gdn_p, conv_p, re_p, im_p))

        proj = _inproj(ys, mod_s, w_in_l, tm=Bs, tn=768)
        proj2 = proj.reshape(Bs, N_PROJ)
        ro, ret_s = _ret_sample(proj2, cos_s, sin_s, gnw, state_ret[l])
        cs_t = jnp.transpose(state_conv[l], (1, 0, 2))
        go, gdn_s, nq, nk, nv = _gdn_sample(proj2, cs_t, conv_w[l], alog_t, dtb_t, nw, state_gdn[l])
        conv_s = jnp.transpose(jnp.concatenate([nq, nk, nv], axis=-1), (1, 0, 2))
        h0 = _ssm_state_to_tiles(state_ssm_re[l], state_ssm_im[l])
        so, hn = _s5_sample(proj2, ssm_bt, ssm_ct, ssm_lam, ssm_dl, h0)
        re_s, im_s = _ssm_tiles_to_state(hn)
        x1 = _outproj(ro[None], go[None], so[None], ys, mod_s, w_glu_l, w_out_l, lnw1, lnb1, tm=Bs)
        ys = _ffn(x1, mod_s, w_ffi_l, w_ffo_l, lnw2, lnb2, tm=Bs, tf=512)
        outs_s.append((ret_s, gdn_s, conv_s, re_s, im_s))

    ret_p, gdn_p, conv_p, re_p, im_p = [jnp.stack(t) for t in zip(*outs_p)]
    ret_s, gdn_s, conv_s, re_s, im_s = [jnp.stack(t) for t in zip(*outs_s)]
    return (yp, ys.reshape(Bs, 1, D_MODEL), ret_p, gdn_p, conv_p, re_p, im_p,
            ret_s, gdn_s, conv_s, re_s, im_s)
```

```python
import functools
import math

import numpy as np
import jax
import jax.numpy as jnp
from jax import lax
from jax.experimental import pallas as pl
from jax.experimental.pallas import tpu as pltpu

F32 = jnp.float32
BF16 = jnp.bfloat16

D_MODEL = 2048
DEPTH = 2
PAST_LEN = 16384
RET_HEADS = 4
RET_DK = 128
RET_W = 512
RET_CHUNK = 128
ROPE_BASE = 10000.0
GDN_HEADS = 8
GDN_DK = 128
GDN_W = 1024
GDN_CHUNK = 64
CONV_W = 4
GDN_CONV_C = 3072
SSM_W = 512
SSM_GROUP = 16
SSM_GROUPS = 32
SSM_N = 64
D_FF = 5632
ALPHA = (2 * DEPTH) ** 0.25
LN_EPS = 1e-5
RMS_EPS = 1e-6
L2_EPS = 1e-6

COL_SU = 6144
COL_AB = 6656
N_IN = 6672
PROJ_TN = 768
N_PROJ = COL_SU + PROJ_TN

SSM_TILES = 4
SSM_TILE_G = 8
SSM_HALF = SSM_TILE_G * SSM_N
SSM_TILE_W = 2 * SSM_HALF
SSM_SEG = 256
SSM_PITCH = 260

VMEM_LIMIT = 56 * 1024 * 1024

RET_LOG_G = [math.log1p(-(2.0 ** (-5.0 - h))) for h in range(RET_HEADS)]

NT = (((1,), (1,)), ((), ()))
TN = (((0,), (0,)), ((), ()))
HI = lax.Precision.HIGHEST


def _params(sem):
    return pltpu.CompilerParams(dimension_semantics=sem, vmem_limit_bytes=VMEM_LIMIT)


def _silu(x):
    return x * jax.nn.sigmoid(x)


def _bdot(a, b):
    return jnp.dot(a.astype(BF16), b.astype(BF16), preferred_element_type=F32)


def _hdot(a, b):
    return jnp.dot(a, b, precision=HI, preferred_element_type=F32)


def _bdot_nt(a, b):
    return lax.dot_general(a.astype(BF16), b.astype(BF16), NT, preferred_element_type=F32)


def _bdot_tn(a, b):
    return lax.dot_general(a.astype(BF16), b.astype(BF16), TN, preferred_element_type=F32)


def _layernorm(z, w, b):
    mu = jnp.mean(z, axis=-1, keepdims=True)
    zc = z - mu
    var = jnp.mean(zc * zc, axis=-1, keepdims=True)
    return zc * lax.rsqrt(var + LN_EPS) * w + b


def _mod_body(c_ref, w_ref, b_ref, o_ref):
    c = c_ref[...]
    o_ref[0] = _bdot(_silu(c), w_ref[0]) + b_ref[0]


def _modulation(c_all, w_mod, b_mod):
    rows = c_all.shape[0]
    tn = 1024
    n = 6 * D_MODEL
    return pl.pallas_call(
        _mod_body,
        grid=(DEPTH, n // tn),
        in_specs=[pl.BlockSpec((rows, D_MODEL), lambda l, j: (0, 0)),
                  pl.BlockSpec((1, D_MODEL, tn), lambda l, j: (l, 0, j)),
                  pl.BlockSpec((1, 1, tn), lambda l, j: (l, 0, j))],
        out_specs=pl.BlockSpec((1, rows, tn), lambda l, j: (l, 0, j)),
        out_shape=jax.ShapeDtypeStruct((DEPTH, rows, n), F32),
        compiler_params=_params(("parallel", "parallel")),
        name="modulation",
    )(c_all, w_mod, b_mod.reshape(DEPTH, 1, n))


def _mod_spec(mod3, tm, k, grid_rank):
    per_token = mod3.shape[1] != 1
    lmb = tm if per_token else 1
    if grid_rank == 3:
        return pl.BlockSpec((1, lmb, D_MODEL), lambda b, i, j: (b, i if per_token else 0, k))
    return pl.BlockSpec((1, lmb, D_MODEL), lambda b, i: (b, i if per_token else 0, k))


def _inproj_body(x_ref, sc_ref, sh_ref, w_ref, wt_ref, o_ref, h_ref, *, n_main):
    j = pl.program_id(2)

    @pl.when(j == 0)
    def _():
        h_ref[...] = (x_ref[0] * (1.0 + sc_ref[0]) + sh_ref[0]).astype(BF16)

    @pl.when(j < n_main)
    def _():
        o_ref[0] = jnp.dot(h_ref[...], w_ref[0].astype(BF16), preferred_element_type=F32)

    @pl.when(j == n_main)
    def _():
        o_ref[0] = jnp.dot(h_ref[...], wt_ref[...], preferred_element_type=F32)


def _inproj(x, mod3, w_in, l, w_tail, tm, tn):
    B, L, _ = x.shape
    n_main = COL_SU // tn
    assert n_main * tn == COL_SU and w_tail.shape == (D_MODEL, tn) and (n_main + 1) * tn == N_PROJ
    return pl.pallas_call(
        functools.partial(_inproj_body, n_main=n_main),
        grid=(B, L // tm, n_main + 1),
        in_specs=[pl.BlockSpec((1, tm, D_MODEL), lambda b, i, j: (b, i, 0)),
                  _mod_spec(mod3, tm, 1, 3),
                  _mod_spec(mod3, tm, 0, 3),
                  pl.BlockSpec((1, D_MODEL, tn), lambda b, i, j: (l, 0, jnp.minimum(j, n_main - 1))),
                  pl.BlockSpec((D_MODEL, tn), lambda b, i, j: (0, 0))],
        out_specs=pl.BlockSpec((1, tm, tn), lambda b, i, j: (b, i, j)),
        out_shape=jax.ShapeDtypeStruct((B, L, N_PROJ), F32),
        scratch_shapes=[pltpu.VMEM((tm, D_MODEL), BF16)],
        compiler_params=_params(("parallel", "parallel", "arbitrary")),
        name="inproj",
    )(x, mod3, mod3, w_in, w_tail)


def _outproj_body(ro_ref, go_ref, y_ref, x_ref, g1_ref, wglu_ref, wout_ref, lnw_ref, lnb_ref, o_ref):
    y = y_ref[0]
    so = y * jax.nn.sigmoid(_bdot(y, wglu_ref[0]))
    acc = jnp.dot(ro_ref[0], wout_ref[0, 0:RET_W, :], preferred_element_type=F32)
    acc += jnp.dot(go_ref[0], wout_ref[0, RET_W:RET_W + GDN_W, :], preferred_element_type=F32)
    acc += jnp.dot(so.astype(BF16), wout_ref[0, RET_W + GDN_W:, :], preferred_element_type=F32)
    z = ALPHA * x_ref[0] + g1_ref[0] * acc
    o_ref[0] = _layernorm(z, lnw_ref[...], lnb_ref[...])


def _outproj(ro, go, y, x, mod3, wglu, wout, l, lnw, lnb, tm):
    B, L, _ = x.shape
    return pl.pallas_call(
        _outproj_body,
        grid=(B, L // tm),
        in_specs=[pl.BlockSpec((1, tm, RET_W), lambda b, i: (b, i, 0)),
                  pl.BlockSpec((1, tm, GDN_W), lambda b, i: (b, i, 0)),
                  pl.BlockSpec((1, tm, SSM_W), lambda b, i: (b, i, 0)),
                  pl.BlockSpec((1, tm, D_MODEL), lambda b, i: (b, i, 0)),
                  _mod_spec(mod3, tm, 2, 2),
                  pl.BlockSpec((1, SSM_W, SSM_W), lambda b, i: (l, 0, 0)),
                  pl.BlockSpec((1, D_MODEL, D_MODEL), lambda b, i: (l, 0, 0)),
                  pl.BlockSpec((1, D_MODEL), lambda b, i: (0, 0)),
                  pl.BlockSpec((1, D_MODEL), lambda b, i: (0, 0))],
        out_specs=pl.BlockSpec((1, tm, D_MODEL), lambda b, i: (b, i, 0)),
        out_shape=jax.ShapeDtypeStruct((B, L, D_MODEL), F32),
        compiler_params=_params(("parallel", "parallel")),
        name="outproj_ln1",
    )(ro, go, y, x, mod3, wglu, wout, lnw, lnb)


def _ffn_body(x_ref, sc_ref, sh_ref, g2_ref, wg_ref, wu_ref, wo_ref, lnw_ref, lnb_ref, o_ref, h_ref, acc_ref):
    f = pl.program_id(2)

    @pl.when(f == 0)
    def _():
        h_ref[...] = (x_ref[0] * (1.0 + sc_ref[0]) + sh_ref[0]).astype(BF16)
        acc_ref[...] = jnp.zeros_like(acc_ref)

    h = h_ref[...]
    gate = jnp.dot(h, wg_ref[0], preferred_element_type=F32)
    up = jnp.dot(h, wu_ref[0], preferred_element_type=F32)
    act = (_silu(gate) * up).astype(BF16)
    acc_ref[...] += jnp.dot(act, wo_ref[0], preferred_element_type=F32)

    @pl.when(f == pl.num_programs(2) - 1)
    def _():
        z = ALPHA * x_ref[0] + g2_ref[0] * acc_ref[...]
        o_ref[0] = _layernorm(z, lnw_ref[...], lnb_ref[...])


def _ffn(x, mod3, w_in, w_out, l, lnw, lnb, tm, tf):
    B, L, _ = x.shape
    nf = D_FF // tf
    return pl.pallas_call(
        _ffn_body,
        grid=(B, L // tm, nf),
        in_specs=[pl.BlockSpec((1, tm, D_MODEL), lambda b, i, f: (b, i, 0)),
                  _mod_spec(mod3, tm, 4, 3),
                  _mod_spec(mod3, tm, 3, 3),
                  _mod_spec(mod3, tm, 5, 3),
                  pl.BlockSpec((1, D_MODEL, tf), lambda b, i, f: (l, 0, f)),
                  pl.BlockSpec((1, D_MODEL, tf), lambda b, i, f: (l, 0, f + nf)),
                  pl.BlockSpec((1, tf, D_MODEL), lambda b, i, f: (l, f, 0)),
                  pl.BlockSpec((1, D_MODEL), lambda b, i, f: (0, 0)),
                  pl.BlockSpec((1, D_MODEL), lambda b, i, f: (0, 0))],
        out_specs=pl.BlockSpec((1, tm, D_MODEL), lambda b, i, f: (b, i, 0)),
        out_shape=jax.ShapeDtypeStruct((B, L, D_MODEL), F32),
        scratch_shapes=[pltpu.VMEM((tm, D_MODEL), BF16), pltpu.VMEM((tm, D_MODEL), F32)],
        compiler_params=_params(("parallel", "parallel", "arbitrary")),
        name="ffn_ln2",
    )(x, mod3, mod3, mod3, w_in, w_in, w_out, lnw, lnb)


def _rope_body(cp_ref, sp_ref, cs_ref, ss_ref):
    half = RET_DK // 2
    for c_ref, s_ref, base, step in ((cp_ref, sp_ref, 0, 1), (cs_ref, ss_ref, PAST_LEN, 0)):
        shape = c_ref.shape
        lane = lax.broadcasted_iota(jnp.int32, shape, 1)
        row = lax.broadcasted_iota(jnp.int32, shape, 0)
        freq = jnp.exp(jnp.where(lane < half, lane, lane - half).astype(F32) * (-math.log(ROPE_BASE) / half))
        ang = (row * step + base).astype(F32) * freq
        c_ref[...] = jnp.cos(ang)
        s = jnp.sin(ang)
        s_ref[...] = jnp.where(lane < RET_DK // 2, -s, s)


def _rope_tables(seq):
    return pl.pallas_call(
        _rope_body,
        out_shape=[jax.ShapeDtypeStruct((seq, RET_DK), F32), jax.ShapeDtypeStruct((seq, RET_DK), F32),
                   jax.ShapeDtypeStruct((8, RET_DK), F32), jax.ShapeDtypeStruct((8, RET_DK), F32)],
        name="rope_tables",
    )()


def _rope(x, cos, sin_signed):
    return x * cos + pltpu.roll(x, RET_DK // 2, 1) * sin_signed


def _ret_prompt_body(q_ref, k_ref, v_ref, g_ref, cos_ref, sin_ref, gnw_ref, o_ref, s_ref, S):
    c = pl.program_id(1)
    C = RET_CHUNK

    @pl.when(c == 0)
    def _():
        S[...] = jnp.zeros_like(S)

    cos = cos_ref[...]
    sin = sin_ref[...]
    row = lax.broadcasted_iota(jnp.int32, (C, C), 0).astype(F32)
    col = lax.broadcasted_iota(jnp.int32, (C, C), 1).astype(F32)
    diff = row - col
    for h in range(RET_HEADS):
        lg = RET_LOG_G[h]
        sl = slice(h * RET_DK, (h + 1) * RET_DK)
        q = _rope(q_ref[0, :, sl], cos, sin)
        k = _rope(k_ref[0, :, sl], cos, sin) * (RET_DK ** -0.5)
        v = v_ref[0, :, sl]
        dmask = jnp.exp(jnp.where(diff >= 0, lg * diff, -jnp.inf))
        scores = _bdot_nt(q, k) * dmask
        intra = _bdot(scores, v)
        kv = _bdot_tn(k * jnp.exp(lg * (C - 1 - row)), v)
        s_prev = S[h]
        cross = _bdot(q * jnp.exp(lg * (row + 1.0)), s_prev)
        S[h] = s_prev * math.exp(lg * C) + kv
        ro = intra + cross
        mu = jnp.mean(ro, axis=-1, keepdims=True)
        rc = ro - mu
        var = jnp.mean(rc * rc, axis=-1, keepdims=True)
        rn = rc * lax.rsqrt(var + LN_EPS) * gnw_ref[:, sl] * _silu(g_ref[0, :, sl])
        o_ref[0, :, sl] = rn.astype(BF16)

    @pl.when(c == pl.num_programs(1) - 1)
    def _():
        s_ref[0] = S[...]


def _ret_prompt(proj, cos, sin, gnw):
    B, L, _ = proj.shape
    C = RET_CHUNK
    blk = lambda k: pl.BlockSpec((1, C, RET_W), lambda b, c: (b, c, k))
    return pl.pallas_call(
        _ret_prompt_body,
        grid=(B, L // C),
        in_specs=[blk(0), blk(1), blk(2), blk(3),
                  pl.BlockSpec((C, RET_DK), lambda b, c: (c, 0)),
                  pl.BlockSpec((C, RET_DK), lambda b, c: (c, 0)),
                  pl.BlockSpec((1, RET_W), lambda b, c: (0, 0))],
        out_specs=[pl.BlockSpec((1, C, RET_W), lambda b, c: (b, c, 0)),
                   pl.BlockSpec((1, RET_HEADS, RET_DK, RET_DK), lambda b, c: (b, 0, 0, 0))],
        out_shape=[jax.ShapeDtypeStruct((B, L, RET_W), BF16),
                   jax.ShapeDtypeStruct((B, RET_HEADS, RET_DK, RET_DK), F32)],
        scratch_shapes=[pltpu.VMEM((RET_HEADS, RET_DK, RET_DK), F32)],
        compiler_params=_params(("parallel", "arbitrary")),
        name="ret_prompt",
    )(proj, proj, proj, proj, cos, sin, gnw)


def _ret_sample_body(q_ref, k_ref, v_ref, g_ref, cos_ref, sin_ref, gnw_ref, dec_ref, s0_ref, o_ref, s_ref, obuf):
    nb = q_ref.shape[0]
    cos = cos_ref[0:1, :]
    sin = sin_ref[0:1, :]
    gdec = dec_ref[0]
    q = _rope(q_ref[...], cos, sin)
    k = _rope(k_ref[...], cos, sin) * (RET_DK ** -0.5)
    v = v_ref[...]
    qk = jnp.sum(q * k, axis=1, keepdims=True)
    qT = q.T
    kT = k.T
    for b in range(nb):
        s0 = s0_ref[0, b, 0]
        qcol = qT[:, b:b + 1]
        kcol = kT[:, b:b + 1]
        vrow = v[b:b + 1, :]
        cross = gdec * jnp.sum(s0 * qcol, axis=0, keepdims=True)
        s_ref[0, b, 0] = s0 * gdec + kcol * vrow
        obuf[b:b + 1, :] = qk[b:b + 1, :] * vrow + cross
    ro = obuf[...]
    mu = jnp.mean(ro, axis=-1, keepdims=True)
    rc = ro - mu
    var = jnp.mean(rc * rc, axis=-1, keepdims=True)
    o_ref[...] = (rc * lax.rsqrt(var + LN_EPS) * gnw_ref[...] * _silu(g_ref[...])).astype(BF16)


def _skip_ref(body, index):
    def wrapped(*refs):
        return body(*refs[:index], *refs[index + 1:])
    return wrapped


def _carry_alias(body, n_in, state_out_index, carried):
    if carried is None:
        return body, [], [], {}
    return (_skip_ref(body, n_in), [pl.BlockSpec(memory_space=pl.ANY)], [carried], {n_in: state_out_index})


def _ret_sample(proj2, cos, sin, gnw, state, l, carried):
    nb = proj2.shape[0]
    H = RET_HEADS
    dec = jnp.asarray(np.broadcast_to(np.exp(np.array(RET_LOG_G))[:, None, None], (H, 1, RET_DK)), F32)
    blk = lambda k: pl.BlockSpec((nb, RET_DK), lambda h: (0, k * H + h))
    sblk = pl.BlockSpec((1, nb, 1, RET_DK, RET_DK), lambda h: (l, 0, h, 0, 0))
    in_specs = [blk(0), blk(1), blk(2), blk(3),
                pl.BlockSpec((8, RET_DK), lambda h: (0, 0)),
                pl.BlockSpec((8, RET_DK), lambda h: (0, 0)),
                pl.BlockSpec((1, RET_DK), lambda h: (0, h)),
                pl.BlockSpec((1, 1, RET_DK), lambda h: (h, 0, 0)),
                sblk]
    body, extra_specs, extra_args, aliases = _carry_alias(_ret_sample_body, len(in_specs), 1, carried)
    return pl.pallas_call(
        body,
        grid=(H,),
        in_specs=in_specs + extra_specs,
        out_specs=[pl.BlockSpec((nb, RET_DK), lambda h: (0, h)), sblk],
        out_shape=[jax.ShapeDtypeStruct((nb, RET_W), BF16),
                   jax.ShapeDtypeStruct(state.shape, F32)],
        scratch_shapes=[pltpu.VMEM((nb, RET_DK), F32)],
        input_output_aliases=aliases,
        compiler_params=_params(("arbitrary",)),
        name="ret_sample",
    )(proj2, proj2, proj2, proj2, cos, sin, gnw, dec, state, *extra_args)


def _gdn_prompt_body(dq_ref, dk_ref, dv_ref, dz_ref, ab_ref, cw_ref, alog_ref, dtb_ref, nw_ref,
                     o_ref, s_ref, conv_ref, xbuf, S):
    c = pl.program_id(1)
    C = GDN_CHUNK
    H = GDN_HEADS
    dk = GDN_DK
    nbt = dq_ref.shape[0]

    @pl.when(c == 0)
    def _():
        S[...] = jnp.zeros_like(S)
        xbuf[:, 0:8, :] = jnp.zeros((nbt, 8, GDN_CONV_C), F32)

    row = lax.broadcasted_iota(jnp.int32, (C, dk), 0)
    lane = lax.broadcasted_iota(jnp.int32, (C, dk), 1)
    tri = lane <= row
    strict = lane < row
    right = lane >= C
    eye_right = (lane == row + C).astype(F32)
    rt = lax.broadcasted_iota(jnp.int32, (C, H * dk), 0)
    jt = lax.broadcasted_iota(jnp.int32, (C, H * dk), 1) % dk
    upper = (rt <= jnp.where(jt < C, jt, -1)).astype(F32)
    r3 = lax.broadcasted_iota(jnp.int32, (C, 3 * C), 0)
    c3 = lax.broadcasted_iota(jnp.int32, (C, 3 * C), 1) % C
    tril_b = jnp.where(r3 >= c3, 1.0, 0.0).astype(BF16)
    ones_b = jnp.ones((C, 3 * C), BF16)

    def cumsum_dot(m3, x):
        hi = x.astype(BF16)
        r1 = x - hi.astype(F32)
        mid = r1.astype(BF16)
        lo = (r1 - mid.astype(F32)).astype(BF16)
        return jnp.dot(m3, jnp.concatenate([hi, mid, lo], axis=0), preferred_element_type=F32)

    cw = cw_ref[...]
    zrow_b = jnp.zeros((C, dk), BF16)

    def split(x):
        hi = x.astype(BF16)
        return hi, (x - hi.astype(F32)).astype(BF16)

    items = []
    for bt in range(nbt):
        xbuf[bt, 8:8 + C, 0:GDN_W] = dq_ref[bt]
        xbuf[bt, 8:8 + C, GDN_W:2 * GDN_W] = dk_ref[bt]
        xbuf[bt, 8:8 + C, 2 * GDN_W:3 * GDN_W] = dv_ref[bt]

        def conv(off, bt=bt):
            y = xbuf[bt, 5:5 + C, off:off + dk] * cw[0:1, off:off + dk]
            for i in range(1, CONV_W):
                y = y + xbuf[bt, 5 + i:5 + i + C, off:off + dk] * cw[i:i + 1, off:off + dk]
            return _silu(y)

        ab = ab_ref[bt]
        a_bc = jnp.concatenate([jnp.broadcast_to(ab[:, h:h + 1], (C, dk)) for h in range(H)], axis=1)
        b_bc = jnp.concatenate([jnp.broadcast_to(ab[:, H + h:H + h + 1], (C, dk)) for h in range(H)], axis=1)
        g_bc = -jnp.exp(alog_ref[...]) * jax.nn.softplus(a_bc + dtb_ref[...])
        gc_col = cumsum_dot(tril_b, g_bc)
        gc_row = cumsum_dot(ones_b, g_bc * upper)
        dfull = gc_col - gc_row
        egc_all = jnp.exp(gc_col)
        beta_bc = jax.nn.sigmoid(b_bc)

        for h in range(H):
            sl = slice(h * dk, (h + 1) * dk)
            q = conv(h * dk)
            k = conv(GDN_W + h * dk)
            v = conv(2 * GDN_W + h * dk)
            q = q * lax.rsqrt(jnp.sum(q * q, axis=-1, keepdims=True) + L2_EPS) * (dk ** -0.5)
            k = k * lax.rsqrt(jnp.sum(k * k, axis=-1, keepdims=True) + L2_EPS)
            gcb = gc_col[:, sl]
            beta = beta_bc[:, sl]
            egc = egc_all[:, sl]
            kb = k * beta
            kq = jnp.concatenate([kb, q], axis=0).astype(BF16)
            k2 = jnp.concatenate([k.astype(BF16), zrow_b], axis=0)
            kkqk = lax.dot_general(kq, k2, NT, preferred_element_type=F32)
            decay = jnp.exp(jnp.where(tri, dfull[:, sl], -jnp.inf))
            w0 = jnp.where(strict, -(kkqk[0:C] * decay), eye_right)
            attn = (kkqk[C:2 * C] * decay)[:, 0:C].astype(BF16)
            rhs = jnp.concatenate([v * beta, kb * egc], axis=1)
            gl = gcb[C - 1:C, :]
            items.append(dict(bt=bt, h=h, w=w0, rhs=rhs, attn=attn, qd=(q * egc).astype(BF16),
                              kd=(k * jnp.exp(gl - gcb)).astype(BF16), egl=jnp.exp(gl)))

    def stacked_lhs(w):
        hi, lo = split(w)
        return jnp.concatenate([hi, lo, hi], axis=1)

    def stacked_rhs(top, bottom_zero):
        hi, lo = split(top)
        z = jnp.zeros_like(hi)
        parts = []
        for piece in (hi, hi, lo):
            parts += [piece, z] if bottom_zero else [z, piece]
        return jnp.concatenate(parts, axis=0)

    for _ in range(6):
        for it in items:
            w = it["w"]
            it["w"] = (jnp.dot(stacked_lhs(w), stacked_rhs(w, True), preferred_element_type=F32)
                       + jnp.where(right, w, 0.0))
    for it in items:
        it["sol"] = jnp.dot(stacked_lhs(it["w"]), stacked_rhs(it["rhs"], False), preferred_element_type=F32)

    for it in items:
        bt, h = it["bt"], it["h"]
        sl = slice(h * dk, (h + 1) * dk)
        u = it["sol"][:, 0:dk]
        w = it["sol"][:, dk:2 * dk]
        s_prev = S[bt, h]
        wq = jnp.concatenate([w.astype(BF16), it["qd"]], axis=0)
        r = jnp.dot(wq, s_prev.astype(BF16), preferred_element_type=F32)
        v_new = u - r[0:C]
        o = r[C:2 * C] + jnp.dot(it["attn"], v_new.astype(BF16), preferred_element_type=F32)
        S[bt, h] = s_prev * it["egl"] + lax.dot_general(it["kd"], v_new.astype(BF16), TN, preferred_element_type=F32)
        o = o * lax.rsqrt(jnp.mean(o * o, axis=-1, keepdims=True) + RMS_EPS) * nw_ref[...]
        o_ref[bt, :, sl] = (o * _silu(dz_ref[bt, :, sl])).astype(BF16)

    for bt in range(nbt):
        xbuf[bt, 0:8, :] = xbuf[bt, C:C + 8, :]

    @pl.when(c == pl.num_programs(1) - 1)
    def _():
        s_ref[...] = S[...]
        conv_ref[...] = xbuf[:, C + 8 - (CONV_W - 1):C + 8, :]


def _gdn_prompt(proj, cw, alog_bc, dtb_bc, nw, nbt):
    B, L, _ = proj.shape
    C = GDN_CHUNK
    H = GDN_HEADS
    blk = lambda k: pl.BlockSpec((nbt, C, GDN_W), lambda b, c: (b, c, k))
    full = lambda shape: pl.BlockSpec(shape, lambda b, c: tuple(0 for _ in shape))
    return pl.pallas_call(
        _gdn_prompt_body,
        grid=(B // nbt, L // C),
        in_specs=[blk(2), blk(3), blk(4), blk(5),
                  pl.BlockSpec((nbt, C, 128), lambda b, c: (b, c, COL_AB // 128)),
                  full((CONV_W, GDN_CONV_C)), full((1, GDN_W)), full((1, GDN_W)), full((1, GDN_DK))],
        out_specs=[pl.BlockSpec((nbt, C, GDN_W), lambda b, c: (b, c, 0)),
                   pl.BlockSpec((nbt, H, GDN_DK, GDN_DK), lambda b, c: (b, 0, 0, 0)),
                   pl.BlockSpec((nbt, CONV_W - 1, GDN_CONV_C), lambda b, c: (b, 0, 0))],
        out_shape=[jax.ShapeDtypeStruct((B, L, GDN_W), BF16),
                   jax.ShapeDtypeStruct((B, H, GDN_DK, GDN_DK), F32),
                   jax.ShapeDtypeStruct((B, CONV_W - 1, GDN_CONV_C), F32)],
        scratch_shapes=[pltpu.VMEM((nbt, C + 8, GDN_CONV_C), F32),
                        pltpu.VMEM((nbt, H, GDN_DK, GDN_DK), F32)],
        compiler_params=_params(("parallel", "arbitrary")),
        name="gdn_prompt",
    )(proj, proj, proj, proj, proj, cw, alog_bc, dtb_bc, nw)


def _gdn_sample_body(dq_ref, dk_ref, dv_ref, dz_ref, ab_ref, csq_ref, csk_ref, csv_ref, cwq_ref, cwk_ref, cwv_ref,
                     alog_ref, dtb_ref, nw_ref, s0_ref, o_ref, s_ref, nq_ref, nk_ref, nv_ref, obuf):
    h = pl.program_id(0)
    nb = dq_ref.shape[0]
    dk = GDN_DK

    def conv(x_ref, cs_ref, w_ref, n_ref):
        x = x_ref[...]
        y = cs_ref[0] * w_ref[0:1, :] + cs_ref[1] * w_ref[1:2, :] + cs_ref[2] * w_ref[2:3, :] + x * w_ref[3:4, :]
        n_ref[0] = cs_ref[1]
        n_ref[1] = cs_ref[2]
        n_ref[2] = x
        return _silu(y)

    q = conv(dq_ref, csq_ref, cwq_ref, nq_ref)
    k = conv(dk_ref, csk_ref, cwk_ref, nk_ref)
    v = conv(dv_ref, csv_ref, cwv_ref, nv_ref)
    q = q * lax.rsqrt(jnp.sum(q * q, axis=-1, keepdims=True) + L2_EPS) * (dk ** -0.5)
    k = k * lax.rsqrt(jnp.sum(k * k, axis=-1, keepdims=True) + L2_EPS)
    er = lax.broadcasted_iota(jnp.int32, (128, dk), 0)
    ab = ab_ref[...]
    a_bc = jnp.dot(ab, (er == h).astype(F32), precision=HI, preferred_element_type=F32)
    b_bc = jnp.dot(ab, (er == h + GDN_HEADS).astype(F32), precision=HI, preferred_element_type=F32)
    g = -jnp.exp(alog_ref[0]) * jax.nn.softplus(a_bc + dtb_ref[0])
    eg = jnp.exp(g)
    beta = jax.nn.sigmoid(b_bc)
    qk = jnp.sum(q * k, axis=1, keepdims=True)
    qT = q.T
    kT = k.T
    for b in range(nb):
        s0 = s0_ref[0, b, 0]
        qcol = qT[:, b:b + 1]
        kcol = kT[:, b:b + 1]
        ks = jnp.sum(s0 * kcol, axis=0, keepdims=True)
        qs = jnp.sum(s0 * qcol, axis=0, keepdims=True)
        bb = beta[b:b + 1, :]
        egb = eg[b:b + 1, :]
        v_new = bb * v[b:b + 1, :] - bb * egb * ks
        s_ref[0, b, 0] = s0 * egb + kcol * v_new
        obuf[b:b + 1, :] = egb * qs + qk[b:b + 1, :] * v_new
    o = obuf[...]
    o = o * lax.rsqrt(jnp.mean(o * o, axis=-1, keepdims=True) + RMS_EPS) * nw_ref[...]
    o_ref[...] = (o * _silu(dz_ref[...])).astype(BF16)


def _gdn_sample(proj2, cs_t, cw, alog_t, dtb_t, nw, state, l, carried):
    nb = proj2.shape[0]
    H = GDN_HEADS
    dk = GDN_DK
    blk = lambda k: pl.BlockSpec((nb, dk), lambda h: (0, k * H + h))
    csblk = lambda k: pl.BlockSpec((CONV_W - 1, nb, dk), lambda h: (0, 0, k * H + h))
    cwblk = lambda k: pl.BlockSpec((CONV_W, dk), lambda h: (0, k * H + h))
    tab = pl.BlockSpec((1, 1, dk), lambda h: (h, 0, 0))
    sblk = pl.BlockSpec((1, nb, 1, dk, dk), lambda h: (l, 0, h, 0, 0))
    nblk = pl.BlockSpec((CONV_W - 1, nb, dk), lambda h: (0, 0, h))
    nshape = jax.ShapeDtypeStruct((CONV_W - 1, nb, GDN_W), F32)
    in_specs = [blk(2), blk(3), blk(4), blk(5),
                pl.BlockSpec((nb, 128), lambda h: (0, COL_AB // 128)),
                csblk(0), csblk(1), csblk(2), cwblk(0), cwblk(1), cwblk(2),
                tab, tab, pl.BlockSpec((1, dk), lambda h: (0, 0)), sblk]
    body, extra_specs, extra_args, aliases = _carry_alias(_gdn_sample_body, len(in_specs), 1, carried)
    return pl.pallas_call(
        body,
        grid=(H,),
        in_specs=in_specs + extra_specs,
        out_specs=[pl.BlockSpec((nb, dk), lambda h: (0, h)), sblk, nblk, nblk, nblk],
        out_shape=[jax.ShapeDtypeStruct((nb, GDN_W), BF16), jax.ShapeDtypeStruct(state.shape, F32),
                   nshape, nshape, nshape],
        scratch_shapes=[pltpu.VMEM((nb, dk), F32)],
        input_output_aliases=aliases,
        compiler_params=_params(("arbitrary",)),
        name="gdn_sample",
    )(proj2, proj2, proj2, proj2, proj2, cs_t, cs_t, cs_t, cw, cw, cw, alog_t, dtb_t, nw, state, *extra_args)


def _ssm_prep_body(are_ref, aim_ref, ldt_ref, bre_ref, bim_ref, lre_ref, lim_ref, obre_ref, obim_ref):
    ar = are_ref[0]
    ai = aim_ref[0]
    dt = jnp.exp(ldt_ref[0])
    er = jnp.exp(ar * dt)
    lr = er * jnp.cos(ai * dt)
    li = er * jnp.sin(ai * dt)
    lre_ref[0] = lr
    lim_ref[0] = li
    xr = lr - 1.0
    den = ar * ar + ai * ai
    cr = ((xr * ar + li * ai) / den)[:, None, :]
    ci = ((li * ar - xr * ai) / den)[:, None, :]
    br = bre_ref[0]
    bi = bim_ref[0]
    obre_ref[0] = cr * br - ci * bi
    obim_ref[0] = cr * bi + ci * br


def _ssm_prep(a_re, a_im, log_dt, b_re, b_im):
    G, N, P = SSM_GROUPS, SSM_N, SSM_GROUP
    ldt = jnp.broadcast_to(log_dt[:, :, None], (DEPTH, G, N))
    bt_re = jnp.transpose(b_re, (0, 1, 3, 2))
    bt_im = jnp.transpose(b_im, (0, 1, 3, 2))
    s2 = pl.BlockSpec((1, G, N), lambda l: (l, 0, 0))
    s3 = pl.BlockSpec((1, G, P, N), lambda l: (l, 0, 0, 0))
    return pl.pallas_call(
        _ssm_prep_body,
        grid=(DEPTH,),
        in_specs=[s2, s2, s2, s3, s3],
        out_specs=[s2, s2, s3, s3],
        out_shape=[jax.ShapeDtypeStruct((DEPTH, G, N), F32), jax.ShapeDtypeStruct((DEPTH, G, N), F32),
                   jax.ShapeDtypeStruct((DEPTH, G, P, N), F32), jax.ShapeDtypeStruct((DEPTH, G, P, N), F32)],
        name="ssm_prep",
    )(a_re, a_im, ldt, bt_re, bt_im)


def _ssm_tile_blockdiag(re, im):
    x = jnp.stack([re, im]).reshape(2, SSM_TILES, SSM_TILE_G, SSM_GROUP, SSM_N)
    eye = jnp.eye(SSM_TILE_G, dtype=x.dtype)
    y = x[:, :, :, :, None, :] * eye[None, None, :, None, :, None]
    y = jnp.transpose(y, (1, 2, 3, 0, 4, 5))
    return y.reshape(SSM_TILES, SSM_TILE_G * SSM_GROUP, SSM_TILE_W)


def _ssm_state_to_tiles(re, im):
    B = re.shape[0]
    x = jnp.stack([re, im], axis=1).reshape(B, 2, SSM_TILES, SSM_HALF)
    return jnp.transpose(x, (0, 2, 1, 3)).reshape(B, SSM_TILES * SSM_TILE_W)


def _ssm_tiles_to_state(x):
    B = x.shape[0]
    x = jnp.transpose(x.reshape(B, SSM_TILES, 2, SSM_HALF), (0, 2, 1, 3)).reshape(B, 2, SSM_GROUPS, SSM_N)
    return x[:, 0], x[:, 1]


def _cmul(ar, ai, br, bi):
    return ar * br - ai * bi, ar * bi + ai * br


def _s5_prompt_body(u_ref, bt_ref, ct_ref, lam_ref, d_ref, y_ref, hl_ref, buf):
    L = u_ref.shape[1]
    nseg = L // SSM_SEG
    ns = SSM_HALF // 128
    bmat = bt_ref[0].astype(BF16)
    for i in range(nseg):
        bu = jnp.dot(u_ref[0, i * SSM_SEG:(i + 1) * SSM_SEG, :].astype(BF16), bmat, preferred_element_type=F32)
        for s in range(2 * ns):
            buf[s, i * SSM_PITCH:i * SSM_PITCH + SSM_SEG, :] = bu[:, s * 128:(s + 1) * 128]

    lam = lam_ref[0]
    a_re = [jnp.broadcast_to(lam[:, s * 128:(s + 1) * 128], (nseg, 128)) for s in range(ns)]
    a_im = [jnp.broadcast_to(lam[:, SSM_HALF + s * 128:SSM_HALF + (s + 1) * 128], (nseg, 128)) for s in range(ns)]

    def scan_step(t, carry):
        out = []
        for s in range(ns):
            hr, hi = carry[2 * s], carry[2 * s + 1]
            pr, pi = _cmul(a_re[s], a_im[s], hr, hi)
            hr = pr + buf[s, pl.ds(t, nseg, stride=SSM_PITCH), :]
            hi = pi + buf[ns + s, pl.ds(t, nseg, stride=SSM_PITCH), :]
            buf[s, pl.ds(t, nseg, stride=SSM_PITCH), :] = hr
            buf[ns + s, pl.ds(t, nseg, stride=SSM_PITCH), :] = hi
            out += [hr, hi]
        return tuple(out)

    zero = jnp.zeros((nseg, 128), F32)
    ends = lax.fori_loop(0, SSM_SEG, scan_step, tuple(zero for _ in range(2 * ns)))

    sub = lax.broadcasted_iota(jnp.int32, (nseg, 128), 0)
    carries = []
    for s in range(ns):
        pr, pi = a_re[s], a_im[s]
        for _ in range(int(math.log2(SSM_SEG))):
            pr, pi = _cmul(pr, pi, pr, pi)
        xr, xi = ends[2 * s], ends[2 * s + 1]
        shift = 1
        while shift < nseg:
            sr = jnp.where(sub >= shift, pltpu.roll(xr, shift, 0), 0.0)
            si = jnp.where(sub >= shift, pltpu.roll(xi, shift, 0), 0.0)
            mr, mi = _cmul(pr, pi, sr, si)
            xr, xi = xr + mr, xi + mi
            pr, pi = _cmul(pr, pi, pr, pi)
            shift *= 2
        hl_ref[0, 0, :, s * 128:(s + 1) * 128] = xr[nseg - 1:nseg, :]
        hl_ref[0, 0, :, SSM_HALF + s * 128:SSM_HALF + (s + 1) * 128] = xi[nseg - 1:nseg, :]
        carries += [jnp.where(sub >= 1, pltpu.roll(xr, 1, 0), 0.0), jnp.where(sub >= 1, pltpu.roll(xi, 1, 0), 0.0)]

    def fix_step(t, pw):
        out = []
        for s in range(ns):
            pr, pi = pw[2 * s], pw[2 * s + 1]
            fr, fi = _cmul(pr, pi, carries[2 * s], carries[2 * s + 1])
            buf[s, pl.ds(t, nseg, stride=SSM_PITCH), :] += fr
            buf[ns + s, pl.ds(t, nseg, stride=SSM_PITCH), :] += fi
            nr, ni = _cmul(pr, pi, a_re[s], a_im[s])
            out += [nr, ni]
        return tuple(out)

    pw0 = []
    for s in range(ns):
        pw0 += [a_re[s], a_im[s]]
    lax.fori_loop(0, SSM_SEG, fix_step, tuple(pw0))

    cmat = ct_ref[0].astype(BF16)
    for i in range(nseg):
        rows = slice(i * SSM_PITCH, i * SSM_PITCH + SSM_SEG)
        y = jnp.zeros((SSM_SEG, 128), F32)
        for s in range(ns):
            y += lax.dot_general(buf[s, rows, :].astype(BF16), cmat[:, s * 128:(s + 1) * 128], NT,
                                 preferred_element_type=F32)
            y -= lax.dot_general(buf[ns + s, rows, :].astype(BF16),
                                 cmat[:, SSM_HALF + s * 128:SSM_HALF + (s + 1) * 128], NT,
                                 preferred_element_type=F32)
        u = u_ref[0, i * SSM_SEG:(i + 1) * SSM_SEG, :]
        y_ref[0, i * SSM_SEG:(i + 1) * SSM_SEG, :] = jax.nn.gelu(y + d_ref[...] * u)


def _s5_prompt(proj, bt, ct, lam, d):
    B, L, _ = proj.shape
    nseg = L // SSM_SEG
    tile = lambda shape: pl.BlockSpec(shape, lambda b, j: (j,) + tuple(0 for _ in shape[1:]))
    return pl.pallas_call(
        _s5_prompt_body,
        grid=(B, SSM_TILES),
        in_specs=[pl.BlockSpec((1, L, 128), lambda b, j: (b, 0, COL_SU // 128 + j)),
                  tile((1, 128, SSM_TILE_W)), tile((1, 128, SSM_TILE_W)), tile((1, 1, SSM_TILE_W)),
                  pl.BlockSpec((1, 128), lambda b, j: (0, j))],
        out_specs=[pl.BlockSpec((1, L, 128), lambda b, j: (b, 0, j)),
                   pl.BlockSpec((1, 1, 1, SSM_TILE_W), lambda b, j: (b, j, 0, 0))],
        out_shape=[jax.ShapeDtypeStruct((B, L, SSM_W), F32),
                   jax.ShapeDtypeStruct((B, SSM_TILES, 1, SSM_TILE_W), F32)],
        scratch_shapes=[pltpu.VMEM((SSM_TILE_W // 128, nseg * SSM_PITCH, 128), F32)],
        compiler_params=_params(("parallel", "parallel")),
        name="s5_prompt",
    )(proj, bt, ct, lam, d)


def _s5_sample_body(u_ref, bt_ref, ct_ref, lam_ref, d_ref, h0_ref, y_ref, h_ref):
    u = u_ref[...]
    bu = jnp.dot(u, bt_ref[0], precision=HI, preferred_element_type=F32)
    lam = lam_ref[0]
    lr, li = lam[:, 0:SSM_HALF], lam[:, SSM_HALF:]
    h0r, h0i = h0_ref[:, 0:SSM_HALF], h0_ref[:, SSM_HALF:]
    pr, pi = _cmul(lr, li, h0r, h0i)
    hr = pr + bu[:, 0:SSM_HALF]
    hi = pi + bu[:, SSM_HALF:]
    h_ref[:, 0:SSM_HALF] = hr
    h_ref[:, SSM_HALF:] = hi
    ct = ct_ref[0]
    y = lax.dot_general(hr, ct[:, 0:SSM_HALF], NT, precision=HI, preferred_element_type=F32)
    y -= lax.dot_general(hi, ct[:, SSM_HALF:], NT, precision=HI, preferred_element_type=F32)
    y_ref[...] = jax.nn.gelu(y + d_ref[...] * u)


def _s5_sample(proj2, bt, ct, lam, d, h0):
    nb = proj2.shape[0]
    tile = lambda shape: pl.BlockSpec(shape, lambda j: (j,) + tuple(0 for _ in shape[1:]))
    hblk = pl.BlockSpec((nb, SSM_TILE_W), lambda j: (0, j))
    return pl.pallas_call(
        _s5_sample_body,
        grid=(SSM_TILES,),
        in_specs=[pl.BlockSpec((nb, 128), lambda j: (0, COL_SU // 128 + j)),
                  tile((1, 128, SSM_TILE_W)), tile((1, 128, SSM_TILE_W)), tile((1, 1, SSM_TILE_W)),
                  pl.BlockSpec((1, 128), lambda j: (0, j)), hblk],
        out_specs=[pl.BlockSpec((nb, 128), lambda j: (0, j)), hblk],
        out_shape=[jax.ShapeDtypeStruct((nb, SSM_W), F32), jax.ShapeDtypeStruct(h0.shape, F32)],
        compiler_params=_params(("parallel",)),
        name="s5_sample",
    )(proj2, bt, ct, lam, d, h0)


def kernel(x_prompt, x_sample, c_prompt, c_sample, state_ret, state_gdn, state_conv, state_ssm_re, state_ssm_im, w_mod, b_mod, w_in, conv_w, ret_gn_w, gdn_a_log, gdn_dt_bias, gdn_norm_w, ssm_a_re, ssm_a_im, ssm_log_dt, ssm_b_re, ssm_b_im, ssm_c_re, ssm_c_im, ssm_d, ssm_w_glu, w_out, ln1_w, ln1_b, w_ffn_in, w_ffn_out, ln2_w, ln2_b):
    Bp, Lp, _ = x_prompt.shape
    Bs = x_sample.shape[0]
    assert x_sample.shape[1] == 1 and Bs % 8 == 0

    pad_rows = (-(Bp + Bs)) % 8
    c_all = jnp.concatenate([c_prompt, c_sample, jnp.zeros((pad_rows, D_MODEL), F32)], axis=0)
    mod_all = _modulation(c_all, w_mod, b_mod)
    lam_re, lam_im, bbar_re, bbar_im = _ssm_prep(ssm_a_re, ssm_a_im, ssm_log_dt, ssm_b_re, ssm_b_im)
    cos_p, sin_p, cos_s, sin_s = _rope_tables(Lp)

    w_out_b = w_out.astype(BF16)
    w_glu_b = ssm_w_glu.astype(BF16)
    w_ffi_b = w_ffn_in.astype(BF16)
    w_ffo_b = w_ffn_out.astype(BF16)

    yp = x_prompt
    ys = x_sample.reshape(1, Bs, D_MODEL)
    outs_p, outs_s = [], []
    ret_s = gdn_s = None
    for l in range(DEPTH):
        w_tail = jnp.concatenate(
            [w_in[l][:, COL_SU + 2 * GDN_HEADS:], w_in[l][:, COL_SU:COL_SU + 2 * GDN_HEADS],
             jnp.zeros((D_MODEL, N_PROJ - N_IN), F32)], axis=1).astype(BF16)
        mod_p = mod_all[l, :Bp][:, None, :]
        mod_s = mod_all[l, Bp:Bp + Bs][None]
        gnw = ret_gn_w[l][None, :]
        nw = gdn_norm_w[l][None, :]
        alog_bc = jnp.repeat(gdn_a_log[l], GDN_DK)[None, :]
        dtb_bc = jnp.repeat(gdn_dt_bias[l], GDN_DK)[None, :]
        alog_t = jnp.broadcast_to(gdn_a_log[l][:, None, None], (GDN_HEADS, 1, GDN_DK))
        dtb_t = jnp.broadcast_to(gdn_dt_bias[l][:, None, None], (GDN_HEADS, 1, GDN_DK))
        ssm_bt = _ssm_tile_blockdiag(bbar_re[l], bbar_im[l])
        ssm_ct = _ssm_tile_blockdiag(ssm_c_re[l], ssm_c_im[l])
        ssm_lam = jnp.concatenate([lam_re[l].reshape(SSM_TILES, 1, SSM_HALF),
                                   lam_im[l].reshape(SSM_TILES, 1, SSM_HALF)], axis=-1)
        ssm_dl = ssm_d[l][None, :]
        lnw1, lnb1 = ln1_w[l][None, :], ln1_b[l][None, :]
        lnw2, lnb2 = ln2_w[l][None, :], ln2_b[l][None, :]

        proj = _inproj(yp, mod_p, w_in, l, w_tail, tm=1024, tn=PROJ_TN)
        ro, ret_p = _ret_prompt(proj, cos_p, sin_p, gnw)
        go, gdn_p, conv_p = _gdn_prompt(proj, conv_w[l], alog_bc, dtb_bc, nw, nbt=2)
        so, hl = _s5_prompt(proj, ssm_bt, ssm_ct, ssm_lam, ssm_dl)
        re_p, im_p = _ssm_tiles_to_state(hl.reshape(Bp, SSM_TILES * SSM_TILE_W))
        x1 = _outproj(ro, go, so, yp, mod_p, w_glu_b, w_out_b, l, lnw1, lnb1, tm=512)
        yp = _ffn(x1, mod_p, w_ffi_b, w_ffo_b, l, lnw2, lnb2, tm=512, tf=512)
        outs_p.append((ret_p, gdn_p, conv_p, re_p, im_p))

        proj = _inproj(ys, mod_s, w_in, l, w_tail, tm=Bs, tn=PROJ_TN)
        proj2 = proj.reshape(Bs, N_PROJ)
        ro, ret_s = _ret_sample(proj2, cos_s, sin_s, gnw, state_ret, l, ret_s)
        cs_t = jnp.transpose(state_conv[l], (1, 0, 2))
        go, gdn_s, nq, nk, nv = _gdn_sample(proj2, cs_t, conv_w[l], alog_t, dtb_t, nw, state_gdn, l, gdn_s)
        conv_s = jnp.transpose(jnp.concatenate([nq, nk, nv], axis=-1), (1, 0, 2))
        h0 = _ssm_state_to_tiles(state_ssm_re[l], state_ssm_im[l])
        so, hn = _s5_sample(proj2, ssm_bt, ssm_ct, ssm_lam, ssm_dl, h0)
        re_s, im_s = _ssm_tiles_to_state(hn)
        x1 = _outproj(ro[None], go[None], so[None], ys, mod_s, w_glu_b, w_out_b, l, lnw1, lnb1, tm=Bs)
        ys = _ffn(x1, mod_s, w_ffi_b, w_ffo_b, l, lnw2, lnb2, tm=Bs, tf=512)
        outs_s.append((conv_s, re_s, im_s))

    ret_p, gdn_p, conv_p, re_p, im_p = [jnp.stack(t) for t in zip(*outs_p)]
    conv_s, re_s, im_s = [jnp.stack(t) for t in zip(*outs_s)]
    return (yp, ys.reshape(Bs, 1, D_MODEL), ret_p, gdn_p, conv_p, re_p, im_p,
            ret_s, gdn_s, conv_s, re_s, im_s)
```

```python
import functools
import math

import numpy as np
import jax
import jax.numpy as jnp
from jax import lax
from jax.experimental import pallas as pl
from jax.experimental.pallas import tpu as pltpu

F32 = jnp.float32
BF16 = jnp.bfloat16

D_MODEL = 2048
DEPTH = 2
PAST_LEN = 16384
RET_HEADS = 4
RET_DK = 128
RET_W = 512
RET_CHUNK = 128
ROPE_BASE = 10000.0
GDN_HEADS = 8
GDN_DK = 128
GDN_W = 1024
GDN_CHUNK = 64
CONV_W = 4
GDN_CONV_C = 3072
SSM_W = 512
SSM_GROUP = 16
SSM_GROUPS = 32
SSM_N = 64
D_FF = 5632
ALPHA = (2 * DEPTH) ** 0.25
LN_EPS = 1e-5
RMS_EPS = 1e-6
L2_EPS = 1e-6

COL_SU = 6144
COL_AB = 6656
N_IN = 6672
PROJ_TN = 768
N_PROJ = COL_SU + PROJ_TN

SSM_TILES = 4
SSM_TILE_G = 8
SSM_HALF = SSM_TILE_G * SSM_N
SSM_TILE_W = 2 * SSM_HALF
SSM_SEG = 256
SSM_PITCH = 260

VMEM_LIMIT = 56 * 1024 * 1024

RET_LOG_G = [math.log1p(-(2.0 ** (-5.0 - h))) for h in range(RET_HEADS)]

NT = (((1,), (1,)), ((), ()))
TN = (((0,), (0,)), ((), ()))
HI = lax.Precision.HIGHEST


def _params(sem):
    return pltpu.CompilerParams(dimension_semantics=sem, vmem_limit_bytes=VMEM_LIMIT)


def _silu(x):
    return x * jax.nn.sigmoid(x)


def _bdot(a, b):
    return jnp.dot(a.astype(BF16), b.astype(BF16), preferred_element_type=F32)


def _hdot(a, b):
    return jnp.dot(a, b, precision=HI, preferred_element_type=F32)


def _bdot_nt(a, b):
    return lax.dot_general(a.astype(BF16), b.astype(BF16), NT, preferred_element_type=F32)


def _bdot_tn(a, b):
    return lax.dot_general(a.astype(BF16), b.astype(BF16), TN, preferred_element_type=F32)


def _layernorm(z, w, b):
    mu = jnp.mean(z, axis=-1, keepdims=True)
    zc = z - mu
    var = jnp.mean(zc * zc, axis=-1, keepdims=True)
    return zc * lax.rsqrt(var + LN_EPS) * w + b


def _mod_body(c_ref, w_ref, b_ref, o_ref):
    c = c_ref[...]
    o_ref[0] = _bdot(_silu(c), w_ref[0]) + b_ref[0]


def _modulation(c_all, w_mod, b_mod):
    rows = c_all.shape[0]
    tn = 1024
    n = 6 * D_MODEL
    return pl.pallas_call(
        _mod_body,
        grid=(DEPTH, n // tn),
        in_specs=[pl.BlockSpec((rows, D_MODEL), lambda l, j: (0, 0)),
                  pl.BlockSpec((1, D_MODEL, tn), lambda l, j: (l, 0, j)),
                  pl.BlockSpec((1, 1, tn), lambda l, j: (l, 0, j))],
        out_specs=pl.BlockSpec((1, rows, tn), lambda l, j: (l, 0, j)),
        out_shape=jax.ShapeDtypeStruct((DEPTH, rows, n), F32),
        compiler_params=_params(("parallel", "parallel")),
        name="modulation",
    )(c_all, w_mod, b_mod.reshape(DEPTH, 1, n))


def _mod_spec(mod3, tm, k, grid_rank):
    per_token = mod3.shape[1] != 1
    lmb = tm if per_token else 1
    if grid_rank == 3:
        return pl.BlockSpec((1, lmb, D_MODEL), lambda b, i, j: (b, i if per_token else 0, k))
    return pl.BlockSpec((1, lmb, D_MODEL), lambda b, i: (b, i if per_token else 0, k))


def _inproj_body(x_ref, sc_ref, sh_ref, w_ref, wt_ref, o_ref, h_ref, *, n_main):
    j = pl.program_id(2)

    @pl.when(j == 0)
    def _():
        h_ref[...] = (x_ref[0] * (1.0 + sc_ref[0]) + sh_ref[0]).astype(BF16)

    @pl.when(j < n_main)
    def _():
        o_ref[0] = jnp.dot(h_ref[...], w_ref[0], preferred_element_type=F32)

    @pl.when(j == n_main)
    def _():
        o_ref[0] = jnp.dot(h_ref[...], wt_ref[...], preferred_element_type=F32)


def _inproj(x, mod3, w_in, l, w_tail, tm, tn):
    B, L, _ = x.shape
    n_main = COL_SU // tn
    assert n_main * tn == COL_SU and w_tail.shape == (D_MODEL, tn) and (n_main + 1) * tn == N_PROJ
    return pl.pallas_call(
        functools.partial(_inproj_body, n_main=n_main),
        grid=(B, L // tm, n_main + 1),
        in_specs=[pl.BlockSpec((1, tm, D_MODEL), lambda b, i, j: (b, i, 0)),
                  _mod_spec(mod3, tm, 1, 3),
                  _mod_spec(mod3, tm, 0, 3),
                  pl.BlockSpec((1, D_MODEL, tn), lambda b, i, j: (l, 0, jnp.minimum(j, n_main - 1))),
                  pl.BlockSpec((D_MODEL, tn), lambda b, i, j: (0, 0))],
        out_specs=pl.BlockSpec((1, tm, tn), lambda b, i, j: (b, i, j)),
        out_shape=jax.ShapeDtypeStruct((B, L, N_PROJ), F32),
        scratch_shapes=[pltpu.VMEM((tm, D_MODEL), BF16)],
        compiler_params=_params(("parallel", "parallel", "arbitrary")),
        name="inproj",
    )(x, mod3, mod3, w_in, w_tail)


def _outproj_body(ro_ref, go_ref, y_ref, x_ref, g1_ref, wglu_ref, wout_ref, lnw_ref, lnb_ref, o_ref):
    y = y_ref[0]
    so = y * jax.nn.sigmoid(_bdot(y, wglu_ref[0]))
    acc = jnp.dot(ro_ref[0], wout_ref[0, 0:RET_W, :], preferred_element_type=F32)
    acc += jnp.dot(go_ref[0], wout_ref[0, RET_W:RET_W + GDN_W, :], preferred_element_type=F32)
    acc += jnp.dot(so.astype(BF16), wout_ref[0, RET_W + GDN_W:, :], preferred_element_type=F32)
    z = ALPHA * x_ref[0] + g1_ref[0] * acc
    o_ref[0] = _layernorm(z, lnw_ref[...], lnb_ref[...])


def _outproj(ro, go, y, x, mod3, wglu, wout, l, lnw, lnb, tm):
    B, L, _ = x.shape
    return pl.pallas_call(
        _outproj_body,
        grid=(B, L // tm),
        in_specs=[pl.BlockSpec((1, tm, RET_W), lambda b, i: (b, i, 0)),
                  pl.BlockSpec((1, tm, GDN_W), lambda b, i: (b, i, 0)),
                  pl.BlockSpec((1, tm, SSM_W), lambda b, i: (b, i, 0)),
                  pl.BlockSpec((1, tm, D_MODEL), lambda b, i: (b, i, 0)),
                  _mod_spec(mod3, tm, 2, 2),
                  pl.BlockSpec((1, SSM_W, SSM_W), lambda b, i: (l, 0, 0)),
                  pl.BlockSpec((1, D_MODEL, D_MODEL), lambda b, i: (l, 0, 0)),
                  pl.BlockSpec((1, D_MODEL), lambda b, i: (0, 0)),
                  pl.BlockSpec((1, D_MODEL), lambda b, i: (0, 0))],
        out_specs=pl.BlockSpec((1, tm, D_MODEL), lambda b, i: (b, i, 0)),
        out_shape=jax.ShapeDtypeStruct((B, L, D_MODEL), F32),
        compiler_params=_params(("parallel", "parallel")),
        name="outproj_ln1",
    )(ro, go, y, x, mod3, wglu, wout, lnw, lnb)


def _ffn_body(x_ref, sc_ref, sh_ref, g2_ref, wg_ref, wu_ref, wo_ref, lnw_ref, lnb_ref, o_ref, h_ref):
    f = pl.program_id(2)

    @pl.when(f == 0)
    def _():
        h_ref[...] = (x_ref[0] * (1.0 + sc_ref[0]) + sh_ref[0]).astype(BF16)
        o_ref[...] = jnp.zeros_like(o_ref)

    h = h_ref[...]
    gate = jnp.dot(h, wg_ref[0], preferred_element_type=F32)
    up = jnp.dot(h, wu_ref[0], preferred_element_type=F32)
    act = (_silu(gate) * up).astype(BF16)
    o_ref[0] += jnp.dot(act, wo_ref[0], preferred_element_type=F32)

    @pl.when(f == pl.num_programs(2) - 1)
    def _():
        z = ALPHA * x_ref[0] + g2_ref[0] * o_ref[0]
        o_ref[0] = _layernorm(z, lnw_ref[...], lnb_ref[...])


def _ffn(x, mod3, w_in, w_out, l, lnw, lnb, tm, tf):
    B, L, _ = x.shape
    nf = D_FF // tf
    return pl.pallas_call(
        _ffn_body,
        grid=(B, L // tm, nf),
        in_specs=[pl.BlockSpec((1, tm, D_MODEL), lambda b, i, f: (b, i, 0)),
                  _mod_spec(mod3, tm, 4, 3),
                  _mod_spec(mod3, tm, 3, 3),
                  _mod_spec(mod3, tm, 5, 3),
                  pl.BlockSpec((1, D_MODEL, tf), lambda b, i, f: (l, 0, f)),
                  pl.BlockSpec((1, D_MODEL, tf), lambda b, i, f: (l, 0, f + nf)),
                  pl.BlockSpec((1, tf, D_MODEL), lambda b, i, f: (l, f, 0)),
                  pl.BlockSpec((1, D_MODEL), lambda b, i, f: (0, 0)),
                  pl.BlockSpec((1, D_MODEL), lambda b, i, f: (0, 0))],
        out_specs=pl.BlockSpec((1, tm, D_MODEL), lambda b, i, f: (b, i, 0)),
        out_shape=jax.ShapeDtypeStruct((B, L, D_MODEL), F32),
        scratch_shapes=[pltpu.VMEM((tm, D_MODEL), BF16)],
        compiler_params=_params(("parallel", "parallel", "arbitrary")),
        name="ffn_ln2",
    )(x, mod3, mod3, mod3, w_in, w_in, w_out, lnw, lnb)


def _rope_body(cp_ref, sp_ref, cs_ref, ss_ref):
    half = RET_DK // 2
    for c_ref, s_ref, base, step in ((cp_ref, sp_ref, 0, 1), (cs_ref, ss_ref, PAST_LEN, 0)):
        shape = c_ref.shape
        lane = lax.broadcasted_iota(jnp.int32, shape, 1)
        row = lax.broadcasted_iota(jnp.int32, shape, 0)
        freq = jnp.exp(jnp.where(lane < half, lane, lane - half).astype(F32) * (-math.log(ROPE_BASE) / half))
        ang = (row * step + base).astype(F32) * freq
        c_ref[...] = jnp.cos(ang)
        s = jnp.sin(ang)
        s_ref[...] = jnp.where(lane < RET_DK // 2, -s, s)


def _rope_tables(seq):
    return pl.pallas_call(
        _rope_body,
        out_shape=[jax.ShapeDtypeStruct((seq, RET_DK), F32), jax.ShapeDtypeStruct((seq, RET_DK), F32),
                   jax.ShapeDtypeStruct((8, RET_DK), F32), jax.ShapeDtypeStruct((8, RET_DK), F32)],
        name="rope_tables",
    )()


def _rope(x, cos, sin_signed):
    return x * cos + pltpu.roll(x, RET_DK // 2, 1) * sin_signed


def _ret_prompt_body(q_ref, k_ref, v_ref, g_ref, cos_ref, sin_ref, gnw_ref, o_ref, s_ref, S):
    c = pl.program_id(1)
    C = RET_CHUNK

    @pl.when(c == 0)
    def _():
        S[...] = jnp.zeros_like(S)

    cos = cos_ref[...]
    sin = sin_ref[...]
    row = lax.broadcasted_iota(jnp.int32, (C, C), 0).astype(F32)
    col = lax.broadcasted_iota(jnp.int32, (C, C), 1).astype(F32)
    diff = row - col
    for h in range(RET_HEADS):
        lg = RET_LOG_G[h]
        sl = slice(h * RET_DK, (h + 1) * RET_DK)
        dmask = jnp.exp(jnp.where(diff >= 0, lg * diff, -jnp.inf))
        k_decay = jnp.exp(lg * (C - 1 - row))
        q_decay = jnp.exp(lg * (row + 1.0))
        for bt in range(q_ref.shape[0]):
            q = _rope(q_ref[bt, :, sl], cos, sin)
            k = _rope(k_ref[bt, :, sl], cos, sin) * (RET_DK ** -0.5)
            v = v_ref[bt, :, sl]
            scores = _bdot_nt(q, k) * dmask
            intra = _bdot(scores, v)
            kv = _bdot_tn(k * k_decay, v)
            s_prev = S[bt, h]
            cross = _bdot(q * q_decay, s_prev)
            S[bt, h] = s_prev * math.exp(lg * C) + kv
            ro = intra + cross
            mu = jnp.mean(ro, axis=-1, keepdims=True)
            rc = ro - mu
            var = jnp.mean(rc * rc, axis=-1, keepdims=True)
            rn = rc * lax.rsqrt(var + LN_EPS) * gnw_ref[:, sl] * _silu(g_ref[bt, :, sl])
            o_ref[bt, :, sl] = rn.astype(BF16)

    @pl.when(c == pl.num_programs(1) - 1)
    def _():
        s_ref[...] = S[...]


def _ret_prompt(proj, cos, sin, gnw, nbt):
    B, L, _ = proj.shape
    C = RET_CHUNK
    blk = lambda k: pl.BlockSpec((nbt, C, RET_W), lambda b, c: (b, c, k))
    return pl.pallas_call(
        _ret_prompt_body,
        grid=(B // nbt, L // C),
        in_specs=[blk(0), blk(1), blk(2), blk(3),
                  pl.BlockSpec((C, RET_DK), lambda b, c: (c, 0)),
                  pl.BlockSpec((C, RET_DK), lambda b, c: (c, 0)),
                  pl.BlockSpec((1, RET_W), lambda b, c: (0, 0))],
        out_specs=[pl.BlockSpec((nbt, C, RET_W), lambda b, c: (b, c, 0)),
                   pl.BlockSpec((nbt, RET_HEADS, RET_DK, RET_DK), lambda b, c: (b, 0, 0, 0))],
        out_shape=[jax.ShapeDtypeStruct((B, L, RET_W), BF16),
                   jax.ShapeDtypeStruct((B, RET_HEADS, RET_DK, RET_DK), F32)],
        scratch_shapes=[pltpu.VMEM((nbt, RET_HEADS, RET_DK, RET_DK), F32)],
        compiler_params=_params(("parallel", "arbitrary")),
        name="ret_prompt",
    )(proj, proj, proj, proj, cos, sin, gnw)


def _ret_sample_body(q_ref, k_ref, v_ref, g_ref, cos_ref, sin_ref, gnw_ref, dec_ref, s0_ref, o_ref, s_ref, obuf):
    nb = q_ref.shape[0]
    cos = cos_ref[0:1, :]
    sin = sin_ref[0:1, :]
    gdec = dec_ref[0]
    q = _rope(q_ref[...], cos, sin)
    k = _rope(k_ref[...], cos, sin) * (RET_DK ** -0.5)
    v = v_ref[...]
    qk = jnp.sum(q * k, axis=1, keepdims=True)
    qT = q.T
    kT = k.T
    for b in range(nb):
        s0 = s0_ref[0, b, 0]
        qcol = qT[:, b:b + 1]
        kcol = kT[:, b:b + 1]
        vrow = v[b:b + 1, :]
        cross = gdec * jnp.sum(s0 * qcol, axis=0, keepdims=True)
        s_ref[0, b, 0] = s0 * gdec + kcol * vrow
        obuf[b:b + 1, :] = qk[b:b + 1, :] * vrow + cross
    ro = obuf[...]
    mu = jnp.mean(ro, axis=-1, keepdims=True)
    rc = ro - mu
    var = jnp.mean(rc * rc, axis=-1, keepdims=True)
    o_ref[...] = (rc * lax.rsqrt(var + LN_EPS) * gnw_ref[...] * _silu(g_ref[...])).astype(BF16)


def _skip_ref(body, index):
    def wrapped(*refs):
        return body(*refs[:index], *refs[index + 1:])
    return wrapped


def _carry_alias(body, n_in, state_out_index, carried):
    if carried is None:
        return body, [], [], {}
    return (_skip_ref(body, n_in), [pl.BlockSpec(memory_space=pl.ANY)], [carried], {n_in: state_out_index})


def _ret_sample(proj2, cos, sin, gnw, state, l, carried):
    nb = proj2.shape[0]
    H = RET_HEADS
    dec = jnp.asarray(np.broadcast_to(np.exp(np.array(RET_LOG_G))[:, None, None], (H, 1, RET_DK)), F32)
    blk = lambda k: pl.BlockSpec((nb, RET_DK), lambda h: (0, k * H + h))
    sblk = pl.BlockSpec((1, nb, 1, RET_DK, RET_DK), lambda h: (l, 0, h, 0, 0))
    in_specs = [blk(0), blk(1), blk(2), blk(3),
                pl.BlockSpec((8, RET_DK), lambda h: (0, 0)),
                pl.BlockSpec((8, RET_DK), lambda h: (0, 0)),
                pl.BlockSpec((1, RET_DK), lambda h: (0, h)),
                pl.BlockSpec((1, 1, RET_DK), lambda h: (h, 0, 0)),
                sblk]
    body, extra_specs, extra_args, aliases = _carry_alias(_ret_sample_body, len(in_specs), 1, carried)
    return pl.pallas_call(
        body,
        grid=(H,),
        in_specs=in_specs + extra_specs,
        out_specs=[pl.BlockSpec((nb, RET_DK), lambda h: (0, h)), sblk],
        out_shape=[jax.ShapeDtypeStruct((nb, RET_W), BF16),
                   jax.ShapeDtypeStruct(state.shape, F32)],
        scratch_shapes=[pltpu.VMEM((nb, RET_DK), F32)],
        input_output_aliases=aliases,
        compiler_params=_params(("arbitrary",)),
        name="ret_sample",
    )(proj2, proj2, proj2, proj2, cos, sin, gnw, dec, state, *extra_args)


def _gdn_prompt_body(dq_ref, dk_ref, dv_ref, dz_ref, ab_ref, cw_ref, alog_ref, dtb_ref, nw_ref,
                     o_ref, s_ref, conv_ref, xbuf, S):
    c = pl.program_id(1)
    C = GDN_CHUNK
    H = GDN_HEADS
    dk = GDN_DK
    nbt = dq_ref.shape[0]

    @pl.when(c == 0)
    def _():
        S[...] = jnp.zeros_like(S)
        xbuf[:, 0:8, :] = jnp.zeros((nbt, 8, GDN_CONV_C), F32)

    row = lax.broadcasted_iota(jnp.int32, (C, dk), 0)
    lane = lax.broadcasted_iota(jnp.int32, (C, dk), 1)
    tri = lane <= row
    strict = lane < row
    r3 =lax.broadcasted_iota(jnp.int32, (C, 3 * C), 0)
    c3 = lax.broadcasted_iota(jnp.int32, (C, 3 * C), 1) % C
    tril_b = jnp.where(r3 >= c3, 1.0, 0.0).astype(BF16)

    def cumsum_rows(x):
        hi = x.astype(BF16)
        r1 = x - hi.astype(F32)
        mid = r1.astype(BF16)
        lo = (r1 - mid.astype(F32)).astype(BF16)
        return jnp.dot(tril_b, jnp.concatenate([hi, mid, lo], axis=0), preferred_element_type=F32)

    def lanes(x, j):
        return jnp.broadcast_to(x[:, j:j + 1], (x.shape[0], dk))

    cw = cw_ref[...]
    zrow_b = jnp.zeros((C, dk), BF16)

    items = []
    for bt in range(nbt):
        xbuf[bt, 8:8 + C, 0:GDN_W] = dq_ref[bt]
        xbuf[bt, 8:8 + C, GDN_W:2 * GDN_W] = dk_ref[bt]
        xbuf[bt, 8:8 + C, 2 * GDN_W:3 * GDN_W] = dv_ref[bt]

        def conv(off, bt=bt):
            y = xbuf[bt, 5:5 + C, off:off + dk] * cw[0:1, off:off + dk]
            for i in range(1, CONV_W):
                y = y + xbuf[bt, 5 + i:5 + i + C, off:off + dk] * cw[i:i + 1, off:off + dk]
            return _silu(y)

        ab = ab_ref[bt]
        g_c = -jnp.exp(alog_ref[...]) * jax.nn.softplus(ab + dtb_ref[...])
        gc_c = cumsum_rows(g_c)
        beta_c = jax.nn.sigmoid(ab)
        egc_c = jnp.exp(gc_c)
        gl_c = gc_c[C - 1:C, :]
        kdec_c = jnp.exp(gl_c - gc_c)
        egl_c = jnp.exp(gl_c)
        gc_t = jnp.concatenate([gc_c, jnp.zeros_like(gc_c)], axis=0).T

        for h in range(H):
            sl = slice(h * dk, (h + 1) * dk)
            q = conv(h * dk)
            k = conv(GDN_W + h * dk)
            v = conv(2 * GDN_W + h * dk)
            q = q * lax.rsqrt(jnp.sum(q * q, axis=-1, keepdims=True) + L2_EPS) * (dk ** -0.5)
            k = k * lax.rsqrt(jnp.sum(k * k, axis=-1, keepdims=True) + L2_EPS)
            beta = lanes(beta_c, H + h)
            egc = lanes(egc_c, h)
            kb = k * beta
            kq = jnp.concatenate([kb, q], axis=0).astype(BF16)
            k2 = jnp.concatenate([k.astype(BF16), zrow_b], axis=0)
            kkqk = lax.dot_general(kq, k2, NT, preferred_element_type=F32)
            dgc = lanes(gc_c, h) - jnp.broadcast_to(gc_t[h:h + 1, :], (C, dk))
            decay = jnp.exp(jnp.where(tri, dgc, -jnp.inf))
            a = jnp.where(strict, kkqk[0:C] * decay, 0.0)
            attn = (kkqk[C:2 * C] * decay)[:, 0:C].astype(BF16)
            rhs = jnp.concatenate([v * beta, kb * egc], axis=1).astype(BF16)
            items.append(dict(bt=bt, h=h, a=a, rhs=rhs, attn=attn, qd=(q * egc).astype(BF16),
                              kd=(k * lanes(kdec_c, h)).astype(BF16), egl=lanes(egl_c, h)))

    pair_col = lane % C
    eye_pair = (pair_col == row).astype(F32)
    left = lane < C

    def level_mask(s):
        br = row // s
        return (pair_col // s) == jnp.where(br % 2 == 1, br - 1, -1)

    def block_diag(x):
        zero = jnp.zeros_like(x)
        return jnp.concatenate([jnp.where(left, x, zero), jnp.where(left, zero, x)], axis=0)

    pairs = []
    for i in range(0, len(items), 2):
        a_pair = items[i]["a"] + pltpu.roll(items[i + 1]["a"], C, 1)
        pairs.append(dict(a=a_pair, d=eye_pair - jnp.where(level_mask(1), a_pair, 0.0)))
    s_blk = 2
    while s_blk < C:
        mask = level_mask(s_blk)
        for p in pairs:
            d = p["d"]
            sub = block_diag(jnp.where(mask, p["a"], 0.0)).astype(BF16)
            g = jnp.dot(d.astype(BF16), sub, preferred_element_type=F32)
            p["d"] = d - jnp.dot(g.astype(BF16), block_diag(d).astype(BF16), preferred_element_type=F32)
        s_blk *= 2
    for i, p in enumerate(pairs):
        r0, r1 = items[2 * i]["rhs"], items[2 * i + 1]["rhs"]
        zero = jnp.zeros_like(r0)
        rr = jnp.concatenate([jnp.concatenate([r0, zero], axis=1), jnp.concatenate([zero, r1], axis=1)], axis=0)
        sol = jnp.dot(p["d"].astype(BF16), rr, preferred_element_type=F32)
        items[2 * i]["sol"] = sol[:, 0:2 * dk]
        items[2 * i + 1]["sol"] = sol[:, 2 * dk:4 * dk]

    for it in items:
        bt, h = it["bt"], it["h"]
        sl = slice(h * dk, (h + 1) * dk)
        u = it["sol"][:, 0:dk]
        w = it["sol"][:, dk:2 * dk]
        s_prev = S[bt, h]
        wq = jnp.concatenate([w.astype(BF16), it["qd"]], axis=0)
        r = jnp.dot(wq, s_prev.astype(BF16), preferred_element_type=F32)
        v_new = u - r[0:C]
        o = r[C:2 * C] + jnp.dot(it["attn"], v_new.astype(BF16), preferred_element_type=F32)
        S[bt, h] = s_prev * it["egl"] + lax.dot_general(it["kd"], v_new.astype(BF16), TN, preferred_element_type=F32)
        o = o * lax.rsqrt(jnp.mean(o * o, axis=-1, keepdims=True) + RMS_EPS) * nw_ref[...]
        o_ref[bt, :, sl] = (o * _silu(dz_ref[bt, :, sl])).astype(BF16)

    for bt in range(nbt):
        xbuf[bt, 0:8, :] = xbuf[bt, C:C + 8, :]

    @pl.when(c == pl.num_programs(1) - 1)
    def _():
        s_ref[...] = S[...]
        conv_ref[...] = xbuf[:, C + 8 - (CONV_W - 1):C + 8, :]


def _gdn_prompt(proj, cw, alog_bc, dtb_bc, nw, nbt):
    B, L, _ = proj.shape
    C = GDN_CHUNK
    H = GDN_HEADS
    blk = lambda k: pl.BlockSpec((nbt, C, GDN_W), lambda b, c: (b, c, k))
    full = lambda shape: pl.BlockSpec(shape, lambda b, c: tuple(0 for _ in shape))
    return pl.pallas_call(
        _gdn_prompt_body,
        grid=(B // nbt, L // C),
        in_specs=[blk(2), blk(3), blk(4), blk(5),
                  pl.BlockSpec((nbt, C, 128), lambda b, c: (b, c, COL_AB // 128)),
                  full((CONV_W, GDN_CONV_C)), full((1, 128)), full((1, 128)), full((1, GDN_DK))],
        out_specs=[pl.BlockSpec((nbt, C, GDN_W), lambda b, c: (b, c, 0)),
                   pl.BlockSpec((nbt, H, GDN_DK, GDN_DK), lambda b, c: (b, 0, 0, 0)),
                   pl.BlockSpec((nbt, CONV_W - 1, GDN_CONV_C), lambda b, c: (b, 0, 0))],
        out_shape=[jax.ShapeDtypeStruct((B, L, GDN_W), BF16),
                   jax.ShapeDtypeStruct((B, H, GDN_DK, GDN_DK), F32),
                   jax.ShapeDtypeStruct((B, CONV_W - 1, GDN_CONV_C), F32)],
        scratch_shapes=[pltpu.VMEM((nbt, C + 8, GDN_CONV_C), F32),
                        pltpu.VMEM((nbt, H, GDN_DK, GDN_DK), F32)],
        compiler_params=_params(("parallel", "arbitrary")),
        name="gdn_prompt",
    )(proj, proj, proj, proj, proj, cw, alog_bc, dtb_bc, nw)


def _gdn_sample_body(dq_ref, dk_ref, dv_ref, dz_ref, ab_ref, csq_ref, csk_ref, csv_ref, cwq_ref, cwk_ref, cwv_ref,
                     alog_ref, dtb_ref, nw_ref, s0_ref, o_ref, s_ref, nq_ref, nk_ref, nv_ref, obuf):
    h = pl.program_id(0)
    nb = dq_ref.shape[0]
    dk = GDN_DK

    def conv(x_ref, cs_ref, w_ref, n_ref):
        x = x_ref[...]
        y = cs_ref[0] * w_ref[0:1, :] + cs_ref[1] * w_ref[1:2, :] + cs_ref[2] * w_ref[2:3, :] + x * w_ref[3:4, :]
        n_ref[0] = cs_ref[1]
        n_ref[1] = cs_ref[2]
        n_ref[2] = x
        return _silu(y)

    q = conv(dq_ref, csq_ref, cwq_ref, nq_ref)
    k = conv(dk_ref, csk_ref, cwk_ref, nk_ref)
    v = conv(dv_ref, csv_ref, cwv_ref, nv_ref)
    q = q * lax.rsqrt(jnp.sum(q * q, axis=-1, keepdims=True) + L2_EPS) * (dk ** -0.5)
    k = k * lax.rsqrt(jnp.sum(k * k, axis=-1, keepdims=True) + L2_EPS)
    er = lax.broadcasted_iota(jnp.int32, (128, dk), 0)
    ab = ab_ref[...]
    a_bc = jnp.dot(ab, (er == h).astype(F32), precision=HI, preferred_element_type=F32)
    b_bc = jnp.dot(ab, (er == h + GDN_HEADS).astype(F32), precision=HI, preferred_element_type=F32)
    g = -jnp.exp(alog_ref[0]) * jax.nn.softplus(a_bc + dtb_ref[0])
    eg = jnp.exp(g)
    beta = jax.nn.sigmoid(b_bc)
    qk = jnp.sum(q * k, axis=1, keepdims=True)
    qT = q.T
    kT = k.T
    for b in range(nb):
        s0 = s0_ref[0, b, 0]
        qcol = qT[:, b:b + 1]
        kcol = kT[:, b:b + 1]
        ks = jnp.sum(s0 * kcol, axis=0, keepdims=True)
        qs = jnp.sum(s0 * qcol, axis=0, keepdims=True)
        bb = beta[b:b + 1, :]
        egb = eg[b:b + 1, :]
        v_new = bb * v[b:b + 1, :] - bb * egb * ks
        s_ref[0, b, 0] = s0 * egb + kcol * v_new
        obuf[b:b + 1, :] = egb * qs + qk[b:b + 1, :] * v_new
    o = obuf[...]
    o = o * lax.rsqrt(jnp.mean(o * o, axis=-1, keepdims=True) + RMS_EPS) * nw_ref[...]
    o_ref[...] = (o * _silu(dz_ref[...])).astype(BF16)


def _gdn_sample(proj2, cs_t, cw, alog_t, dtb_t, nw, state, l, carried):
    nb = proj2.shape[0]
    H = GDN_HEADS
    dk = GDN_DK
    blk = lambda k: pl.BlockSpec((nb, dk), lambda h: (0, k * H + h))
    csblk = lambda k: pl.BlockSpec((CONV_W - 1, nb, dk), lambda h: (0, 0, k * H + h))
    cwblk = lambda k: pl.BlockSpec((CONV_W, dk), lambda h: (0, k * H + h))
    tab = pl.BlockSpec((1, 1, dk), lambda h: (h, 0, 0))
    sblk = pl.BlockSpec((1, nb, 1, dk, dk), lambda h: (l, 0, h, 0, 0))
    nblk = pl.BlockSpec((CONV_W - 1, nb, dk), lambda h: (0, 0, h))
    nshape = jax.ShapeDtypeStruct((CONV_W - 1, nb, GDN_W), F32)
    in_specs = [blk(2), blk(3), blk(4), blk(5),
                pl.BlockSpec((nb, 128), lambda h: (0, COL_AB // 128)),
                csblk(0), csblk(1), csblk(2), cwblk(0), cwblk(1), cwblk(2),
                tab, tab, pl.BlockSpec((1, dk), lambda h: (0, 0)), sblk]
    body, extra_specs, extra_args, aliases = _carry_alias(_gdn_sample_body, len(in_specs), 1, carried)
    return pl.pallas_call(
        body,
        grid=(H,),
        in_specs=in_specs + extra_specs,
        out_specs=[pl.BlockSpec((nb, dk), lambda h: (0, h)), sblk, nblk, nblk, nblk],
        out_shape=[jax.ShapeDtypeStruct((nb, GDN_W), BF16), jax.ShapeDtypeStruct(state.shape, F32),
                   nshape, nshape, nshape],
        scratch_shapes=[pltpu.VMEM((nb, dk), F32)],
        input_output_aliases=aliases,
        compiler_params=_params(("arbitrary",)),
        name="gdn_sample",
    )(proj2, proj2, proj2, proj2, proj2, cs_t, cs_t, cs_t, cw, cw, cw, alog_t, dtb_t, nw, state, *extra_args)


def _ssm_prep_body(are_ref, aim_ref, ldt_ref, bre_ref, bim_ref, lre_ref, lim_ref, obre_ref, obim_ref):
    ar = are_ref[0]
    ai = aim_ref[0]
    dt = jnp.exp(ldt_ref[0])
    er = jnp.exp(ar * dt)
    lr = er * jnp.cos(ai * dt)
    li = er * jnp.sin(ai * dt)
    lre_ref[0] = lr
    lim_ref[0] = li
    xr = lr - 1.0
    den = ar * ar + ai * ai
    cr = ((xr * ar + li * ai) / den)[:, None, :]
    ci = ((li * ar - xr * ai) / den)[:, None, :]
    br = bre_ref[0]
    bi = bim_ref[0]
    obre_ref[0] = cr * br - ci * bi
    obim_ref[0] = cr * bi + ci * br


def _ssm_prep(a_re, a_im, log_dt, b_re, b_im):
    G, N, P = SSM_GROUPS, SSM_N, SSM_GROUP
    ldt = jnp.broadcast_to(log_dt[:, :, None], (DEPTH, G, N))
    bt_re = jnp.transpose(b_re, (0, 1, 3, 2))
    bt_im = jnp.transpose(b_im, (0, 1, 3, 2))
    s2 = pl.BlockSpec((1, G, N), lambda l: (l, 0, 0))
    s3 = pl.BlockSpec((1, G, P, N), lambda l: (l, 0, 0, 0))
    return pl.pallas_call(
        _ssm_prep_body,
        grid=(DEPTH,),
        in_specs=[s2, s2, s2, s3, s3],
        out_specs=[s2, s2, s3, s3],
        out_shape=[jax.ShapeDtypeStruct((DEPTH, G, N), F32), jax.ShapeDtypeStruct((DEPTH, G, N), F32),
                   jax.ShapeDtypeStruct((DEPTH, G, P, N), F32), jax.ShapeDtypeStruct((DEPTH, G, P, N), F32)],
        name="ssm_prep",
    )(a_re, a_im, ldt, bt_re, bt_im)


def _ssm_tile_blockdiag(re, im):
    x = jnp.stack([re, im]).reshape(2, SSM_TILES, SSM_TILE_G, SSM_GROUP, SSM_N)
    eye = jnp.eye(SSM_TILE_G, dtype=x.dtype)
    y = x[:, :, :, :, None, :] * eye[None, None, :, None, :, None]
    y = jnp.transpose(y, (1, 2, 3, 0, 4, 5))
    return y.reshape(SSM_TILES, SSM_TILE_G * SSM_GROUP, SSM_TILE_W)


def _ssm_state_to_tiles(re, im):
    B = re.shape[0]
    x = jnp.stack([re, im], axis=1).reshape(B, 2, SSM_TILES, SSM_HALF)
    return jnp.transpose(x, (0, 2, 1, 3)).reshape(B, SSM_TILES * SSM_TILE_W)


def _ssm_tiles_to_state(x):
    B = x.shape[0]
    x = jnp.transpose(x.reshape(B, SSM_TILES, 2, SSM_HALF), (0, 2, 1, 3)).reshape(B, 2, SSM_GROUPS, SSM_N)
    return x[:, 0], x[:, 1]


def _cmul(ar, ai, br, bi):
    return ar * br - ai * bi, ar * bi + ai * br


def _s5_prompt_body(u_ref, bt_ref, ct_ref, lam_ref, d_ref, y_ref, hl_ref, buf):
    L = u_ref.shape[1]
    nseg = L // SSM_SEG
    ns = SSM_HALF // 128
    bmat = bt_ref[0].astype(BF16)
    for i in range(nseg):
        bu = jnp.dot(u_ref[0, i * SSM_SEG:(i + 1) * SSM_SEG, :].astype(BF16), bmat, preferred_element_type=F32)
        for s in range(2 * ns):
            buf[s, i * SSM_PITCH:i * SSM_PITCH + SSM_SEG, :] = bu[:, s * 128:(s + 1) * 128]

    lam = lam_ref[0]
    a_re = [jnp.broadcast_to(lam[:, s * 128:(s + 1) * 128], (nseg, 128)) for s in range(ns)]
    a_im = [jnp.broadcast_to(lam[:, SSM_HALF + s * 128:SSM_HALF + (s + 1) * 128], (nseg, 128)) for s in range(ns)]

    def scan_step(t, carry):
        out = []
        for s in range(ns):
            hr, hi = carry[2 * s], carry[2 * s + 1]
            pr, pi = _cmul(a_re[s], a_im[s], hr, hi)
            hr = pr + buf[s, pl.ds(t, nseg, stride=SSM_PITCH), :]
            hi = pi + buf[ns + s, pl.ds(t, nseg, stride=SSM_PITCH), :]
            buf[s, pl.ds(t, nseg, stride=SSM_PITCH), :] = hr
            buf[ns + s, pl.ds(t, nseg, stride=SSM_PITCH), :] = hi
            out += [hr, hi]
        return tuple(out)

    zero = jnp.zeros((nseg, 128), F32)
    ends = lax.fori_loop(0, SSM_SEG, scan_step, tuple(zero for _ in range(2 * ns)))

    sub = lax.broadcasted_iota(jnp.int32, (nseg, 128), 0)
    carries = []
    for s in range(ns):
        pr, pi = a_re[s], a_im[s]
        for _ in range(int(math.log2(SSM_SEG))):
            pr, pi = _cmul(pr, pi, pr, pi)
        xr, xi = ends[2 * s], ends[2 * s + 1]
        shift = 1
        while shift < nseg:
            sr = jnp.where(sub >= shift, pltpu.roll(xr, shift, 0), 0.0)
            si = jnp.where(sub >= shift, pltpu.roll(xi, shift, 0), 0.0)
            mr, mi = _cmul(pr, pi, sr, si)
            xr, xi = xr + mr, xi + mi
            pr, pi = _cmul(pr, pi, pr, pi)
            shift *= 2
        hl_ref[0, 0, :, s * 128:(s + 1) * 128] = xr[nseg - 1:nseg, :]
        hl_ref[0, 0, :, SSM_HALF + s * 128:SSM_HALF + (s + 1) * 128] = xi[nseg - 1:nseg, :]
        carries += [jnp.where(sub >= 1, pltpu.roll(xr, 1, 0), 0.0), jnp.where(sub >= 1, pltpu.roll(xi, 1, 0), 0.0)]

    def fix_step(t, pw):
        out = []
        for s in range(ns):
            pr, pi = pw[2 * s], pw[2 * s + 1]
            fr, fi = _cmul(pr, pi, carries[2 * s], carries[2 * s + 1])
            buf[s, pl.ds(t, nseg, stride=SSM_PITCH), :] += fr
            buf[ns + s, pl.ds(t, nseg, stride=SSM_PITCH), :] += fi
            nr, ni = _cmul(pr, pi, a_re[s], a_im[s])
            out += [nr, ni]
        return tuple(out)

    pw0 = []
    for s in range(ns):
        pw0 += [a_re[s], a_im[s]]
    lax.fori_loop(0, SSM_SEG, fix_step, tuple(pw0))

    cmat = ct_ref[0].astype(BF16)
    for i in range(nseg):
        rows = slice(i * SSM_PITCH, i * SSM_PITCH + SSM_SEG)
        y = jnp.zeros((SSM_SEG, 128), F32)
        for s in range(ns):
            y += lax.dot_general(buf[s, rows, :].astype(BF16), cmat[:, s * 128:(s + 1) * 128], NT,
                                 preferred_element_type=F32)
            y -= lax.dot_general(buf[ns + s, rows, :].astype(BF16),
                                 cmat[:, SSM_HALF + s * 128:SSM_HALF + (s + 1) * 128], NT,
                                 preferred_element_type=F32)
        u = u_ref[0, i * SSM_SEG:(i + 1) * SSM_SEG, :]
        y_ref[0, i * SSM_SEG:(i + 1) * SSM_SEG, :] = jax.nn.gelu(y + d_ref[...] * u)


def _s5_prompt(proj, bt, ct, lam, d):
    B, L, _ = proj.shape
    nseg = L // SSM_SEG
    tile = lambda shape: pl.BlockSpec(shape, lambda b, j: (j,) + tuple(0 for _ in shape[1:]))
    return pl.pallas_call(
        _s5_prompt_body,
        grid=(B, SSM_TILES),
        in_specs=[pl.BlockSpec((1, L, 128), lambda b, j: (b, 0, COL_SU // 128 + j)),
                  tile((1, 128, SSM_TILE_W)), tile((1, 128, SSM_TILE_W)), tile((1, 1, SSM_TILE_W)),
                  pl.BlockSpec((1, 128), lambda b, j: (0, j))],
        out_specs=[pl.BlockSpec((1, L, 128), lambda b, j: (b, 0, j)),
                   pl.BlockSpec((1, 1, 1, SSM_TILE_W), lambda b, j: (b, j, 0, 0))],
        out_shape=[jax.ShapeDtypeStruct((B, L, SSM_W), F32),
                   jax.ShapeDtypeStruct((B, SSM_TILES, 1, SSM_TILE_W), F32)],
        scratch_shapes=[pltpu.VMEM((SSM_TILE_W // 128, nseg * SSM_PITCH, 128), F32)],
        compiler_params=_params(("parallel", "parallel")),
        name="s5_prompt",
    )(proj, bt, ct, lam, d)


def _s5_sample_body(u_ref, bt_ref, ct_ref, lam_ref, d_ref, h0_ref, y_ref, h_ref):
    u = u_ref[...]
    bu = jnp.dot(u, bt_ref[0], precision=HI, preferred_element_type=F32)
    lam = lam_ref[0]
    lr, li = lam[:, 0:SSM_HALF], lam[:, SSM_HALF:]
    h0r, h0i = h0_ref[:, 0:SSM_HALF], h0_ref[:, SSM_HALF:]
    pr, pi = _cmul(lr, li, h0r, h0i)
    hr = pr + bu[:, 0:SSM_HALF]
    hi = pi + bu[:, SSM_HALF:]
    h_ref[:, 0:SSM_HALF] = hr
    h_ref[:, SSM_HALF:] = hi
    ct = ct_ref[0]
    y = lax.dot_general(hr, ct[:, 0:SSM_HALF], NT, precision=HI, preferred_element_type=F32)
    y -= lax.dot_general(hi, ct[:, SSM_HALF:], NT, precision=HI, preferred_element_type=F32)
    y_ref[...] = jax.nn.gelu(y + d_ref[...] * u)


def _s5_sample(proj2, bt, ct, lam, d, h0):
    nb = proj2.shape[0]
    tile = lambda shape: pl.BlockSpec(shape, lambda j: (j,) + tuple(0 for _ in shape[1:]))
    hblk = pl.BlockSpec((nb, SSM_TILE_W), lambda j: (0, j))
    return pl.pallas_call(
        _s5_sample_body,
        grid=(SSM_TILES,),
        in_specs=[pl.BlockSpec((nb, 128), lambda j: (0, COL_SU // 128 + j)),
                  tile((1, 128, SSM_TILE_W)), tile((1, 128, SSM_TILE_W)), tile((1, 1, SSM_TILE_W)),
                  pl.BlockSpec((1, 128), lambda j: (0, j)), hblk],
        out_specs=[pl.BlockSpec((nb, 128), lambda j: (0, j)), hblk],
        out_shape=[jax.ShapeDtypeStruct((nb, SSM_W), F32), jax.ShapeDtypeStruct(h0.shape, F32)],
        compiler_params=_params(("parallel",)),
        name="s5_sample",
    )(proj2, bt, ct, lam, d, h0)


def kernel(x_prompt, x_sample, c_prompt, c_sample, state_ret, state_gdn, state_conv, state_ssm_re, state_ssm_im, w_mod, b_mod, w_in, conv_w, ret_gn_w, gdn_a_log, gdn_dt_bias, gdn_norm_w, ssm_a_re, ssm_a_im, ssm_log_dt, ssm_b_re, ssm_b_im, ssm_c_re, ssm_c_im, ssm_d, ssm_w_glu, w_out, ln1_w, ln1_b, w_ffn_in, w_ffn_out, ln2_w, ln2_b):
    Bp, Lp, _ = x_prompt.shape
    Bs = x_sample.shape[0]
    assert x_sample.shape[1] == 1 and Bs % 8 == 0

    pad_rows = (-(Bp + Bs)) % 8
    c_all = jnp.concatenate([c_prompt, c_sample, jnp.zeros((pad_rows, D_MODEL), F32)], axis=0)
    mod_all = _modulation(c_all, w_mod, b_mod)
    lam_re, lam_im, bbar_re, bbar_im = _ssm_prep(ssm_a_re, ssm_a_im, ssm_log_dt, ssm_b_re, ssm_b_im)
    cos_p, sin_p, cos_s, sin_s = _rope_tables(Lp)

    w_in_b = w_in.astype(BF16)
    w_out_b = w_out.astype(BF16)
    w_glu_b = ssm_w_glu.astype(BF16)
    w_ffi_b = w_ffn_in.astype(BF16)
    w_ffo_b = w_ffn_out.astype(BF16)

    yp = x_prompt
    ys = x_sample.reshape(1, Bs, D_MODEL)
    outs_p, outs_s = [], []
    ret_s = gdn_s = None
    for l in range(DEPTH):
        w_tail = jnp.concatenate(
            [w_in_b[l, :, COL_SU + 2 * GDN_HEADS:], w_in_b[l, :, COL_SU:COL_SU + 2 * GDN_HEADS],
             jnp.zeros((D_MODEL, N_PROJ - N_IN), BF16)], axis=1)
        mod_p = mod_all[l, :Bp][:, None, :]
        mod_s = mod_all[l, Bp:Bp + Bs][None]
        gnw = ret_gn_w[l][None, :]
        nw = gdn_norm_w[l][None, :]
        alog_bc = jnp.pad(gdn_a_log[l], (0, 128 - GDN_HEADS))[None, :]
        dtb_bc = jnp.pad(gdn_dt_bias[l], (0, 128 - GDN_HEADS))[None, :]
        alog_t = jnp.broadcast_to(gdn_a_log[l][:, None, None], (GDN_HEADS, 1, GDN_DK))
        dtb_t = jnp.broadcast_to(gdn_dt_bias[l][:, None, None], (GDN_HEADS, 1, GDN_DK))
        ssm_bt = _ssm_tile_blockdiag(bbar_re[l], bbar_im[l])
        ssm_ct = _ssm_tile_blockdiag(ssm_c_re[l], ssm_c_im[l])
        ssm_lam = jnp.concatenate([lam_re[l].reshape(SSM_TILES, 1, SSM_HALF),
                                   lam_im[l].reshape(SSM_TILES, 1, SSM_HALF)], axis=-1)
        ssm_dl = ssm_d[l][None, :]
        lnw1, lnb1 = ln1_w[l][None, :], ln1_b[l][None, :]
        lnw2, lnb2 = ln2_w[l][None, :], ln2_b[l][None, :]

        proj = _inproj(yp, mod_p, w_in_b, l, w_tail, tm=1024, tn=PROJ_TN)
        ro, ret_p = _ret_prompt(proj, cos_p, sin_p, gnw, nbt=4)
        go, gdn_p, conv_p = _gdn_prompt(proj, conv_w[l], alog_bc, dtb_bc, nw, nbt=4)
        so, hl = _s5_prompt(proj, ssm_bt, ssm_ct, ssm_lam, ssm_dl)
        re_p, im_p = _ssm_tiles_to_state(hl.reshape(Bp, SSM_TILES * SSM_TILE_W))
        x1 = _outproj(ro, go, so, yp, mod_p, w_glu_b, w_out_b, l, lnw1, lnb1, tm=512)
        yp = _ffn(x1, mod_p, w_ffi_b, w_ffo_b, l, lnw2, lnb2, tm=512, tf=512)
        outs_p.append((ret_p, gdn_p, conv_p, re_p, im_p))

        proj = _inproj(ys, mod_s, w_in_b, l, w_tail, tm=Bs, tn=PROJ_TN)
        proj2 = proj.reshape(Bs, N_PROJ)
        ro, ret_s = _ret_sample(proj2, cos_s, sin_s, gnw, state_ret, l, ret_s)
        cs_t = jnp.transpose(state_conv[l], (1, 0, 2))
        go, gdn_s, nq, nk, nv = _gdn_sample(proj2, cs_t, conv_w[l], alog_t, dtb_t, nw, state_gdn, l, gdn_s)
        conv_s = jnp.transpose(jnp.concatenate([nq, nk, nv], axis=-1), (1, 0, 2))
        h0 = _ssm_state_to_tiles(state_ssm_re[l], state_ssm_im[l])
        so, hn = _s5_sample(proj2, ssm_bt, ssm_ct, ssm_lam, ssm_dl, h0)
        re_s, im_s = _ssm_tiles_to_state(hn)
        x1 = _outproj(ro[None], go[None], so[None], ys, mod_s, w_glu_b, w_out_b, l, lnw1, lnb1, tm=Bs)
        ys = _ffn(x1, mod_s, w_ffi_b, w_ffo_b, l, lnw2, lnb2, tm=Bs, tf=512)
        outs_s.append((conv_s, re_s, im_s))

    ret_p, gdn_p, conv_p, re_p, im_p = [jnp.stack(t) for t in zip(*outs_p)]
    conv_s, re_s, im_s = [jnp.stack(t) for t in zip(*outs_s)]
    return (yp, ys.reshape(Bs, 1, D_MODEL), ret_p, gdn_p, conv_p, re_p, im_p,
            ret_s, gdn_s, conv_s, re_s, im_s)
```

```python
import functools
import math

import numpy as np
import jax
import jax.numpy as jnp
from jax import lax
from jax.experimental import pallas as pl
from jax.experimental.pallas import tpu as pltpu

F32 = jnp.float32
BF16 = jnp.bfloat16

D_MODEL = 2048
DEPTH = 2
PAST_LEN = 16384
RET_HEADS = 4
RET_DK = 128
RET_W = 512
RET_CHUNK = 128
ROPE_BASE = 10000.0
GDN_HEADS = 8
GDN_DK = 128
GDN_W = 1024
GDN_CHUNK = 64
CONV_W = 4
GDN_CONV_C = 3072
SSM_W = 512
SSM_GROUP = 16
SSM_GROUPS = 32
SSM_N = 64
D_FF = 5632
ALPHA = (2 * DEPTH) ** 0.25
LN_EPS = 1e-5
RMS_EPS = 1e-6
L2_EPS = 1e-6

COL_SU = 6144
COL_AB = 6656
N_IN = 6672
PROJ_TN = 768
N_PROJ = COL_SU + PROJ_TN

SSM_TILES = 4
SSM_TILE_G = 8
SSM_HALF = SSM_TILE_G * SSM_N
SSM_TILE_W = 2 * SSM_HALF
SSM_SEG = 256
SSM_PITCH = 260
SSM_UNROLL = 4

VMEM_LIMIT = 56 * 1024 * 1024

RET_LOG_G = [math.log1p(-(2.0 ** (-5.0 - h))) for h in range(RET_HEADS)]

NT = (((1,), (1,)), ((), ()))
TN = (((0,), (0,)), ((), ()))
HI = lax.Precision.HIGHEST


def _params(sem):
    return pltpu.CompilerParams(dimension_semantics=sem, vmem_limit_bytes=VMEM_LIMIT)


def _silu(x):
    return x * jax.nn.sigmoid(x)


def _bdot(a, b):
    return jnp.dot(a.astype(BF16), b.astype(BF16), preferred_element_type=F32)


def _hdot(a, b):
    return jnp.dot(a, b, precision=HI, preferred_element_type=F32)


def _bdot_nt(a, b):
    return lax.dot_general(a.astype(BF16), b.astype(BF16), NT, preferred_element_type=F32)


def _bdot_tn(a, b):
    return lax.dot_general(a.astype(BF16), b.astype(BF16), TN, preferred_element_type=F32)


def _layernorm(z, w, b):
    mu = jnp.mean(z, axis=-1, keepdims=True)
    zc = z - mu
    var = jnp.mean(zc * zc, axis=-1, keepdims=True)
    return zc * lax.rsqrt(var + LN_EPS) * w + b


def _cast_body(x_ref, o_ref):
    o_ref[...] = x_ref[...].astype(o_ref.dtype)


def _to_bf16(w, rows):
    depth, r, n = w.shape
    spec = pl.BlockSpec((1, rows, n), lambda l, i: (l, i, 0))
    return pl.pallas_call(
        _cast_body,
        grid=(depth, r // rows),
        in_specs=[spec],
        out_specs=spec,
        out_shape=jax.ShapeDtypeStruct(w.shape, BF16),
        compiler_params=_params(("parallel", "parallel")),
        name="to_bf16",
    )(w)


def _mod_body(c_ref, w_ref, b_ref, o_ref):
    c = c_ref[...]
    o_ref[0] = _bdot(_silu(c), w_ref[0]) + b_ref[0]


def _modulation(c_all, w_mod, b_mod):
    rows = c_all.shape[0]
    tn = 1024
    n = 6 * D_MODEL
    return pl.pallas_call(
        _mod_body,
        grid=(DEPTH, n // tn),
        in_specs=[pl.BlockSpec((rows, D_MODEL), lambda l, j: (0, 0)),
                  pl.BlockSpec((1, D_MODEL, tn), lambda l, j: (l, 0, j)),
                  pl.BlockSpec((1, 1, tn), lambda l, j: (l, 0, j))],
        out_specs=pl.BlockSpec((1, rows, tn), lambda l, j: (l, 0, j)),
        out_shape=jax.ShapeDtypeStruct((DEPTH, rows, n), F32),
        compiler_params=_params(("parallel", "parallel")),
        name="modulation",
    )(c_all, w_mod, b_mod.reshape(DEPTH, 1, n))


def _mod_spec(mod3, tm, k, grid_rank):
    per_token = mod3.shape[1] != 1
    lmb = tm if per_token else 1
    if grid_rank == 3:
        return pl.BlockSpec((1, lmb, D_MODEL), lambda b, i, j: (b, i if per_token else 0, k))
    return pl.BlockSpec((1, lmb, D_MODEL), lambda b, i: (b, i if per_token else 0, k))


def _inproj_body(x_ref, sc_ref, sh_ref, w_ref, wt_ref, o_ref, h_ref, *, n_main):
    j = pl.program_id(2)

    @pl.when(j == 0)
    def _():
        h_ref[...] = (x_ref[0] * (1.0 + sc_ref[0]) + sh_ref[0]).astype(BF16)

    @pl.when(j < n_main)
    def _():
        o_ref[0] = jnp.dot(h_ref[...], w_ref[0], preferred_element_type=F32)

    @pl.when(j == n_main)
    def _():
        o_ref[0] = jnp.dot(h_ref[...], wt_ref[...], preferred_element_type=F32)


def _inproj(x, mod3, w_in, l, w_tail, tm, tn):
    B, L, _ = x.shape
    n_main = COL_SU // tn
    assert n_main * tn == COL_SU and w_tail.shape == (D_MODEL, tn) and (n_main + 1) * tn == N_PROJ
    return pl.pallas_call(
        functools.partial(_inproj_body, n_main=n_main),
        grid=(B, L // tm, n_main + 1),
        in_specs=[pl.BlockSpec((1, tm, D_MODEL), lambda b, i, j: (b, i, 0)),
                  _mod_spec(mod3, tm, 1, 3),
                  _mod_spec(mod3, tm, 0, 3),
                  pl.BlockSpec((1, D_MODEL, tn), lambda b, i, j: (l, 0, jnp.minimum(j, n_main - 1))),
                  pl.BlockSpec((D_MODEL, tn), lambda b, i, j: (0, 0))],
        out_specs=pl.BlockSpec((1, tm, tn), lambda b, i, j: (b, i, j)),
        out_shape=jax.ShapeDtypeStruct((B, L, N_PROJ), F32),
        scratch_shapes=[pltpu.VMEM((tm, D_MODEL), BF16)],
        compiler_params=_params(("parallel", "parallel", "arbitrary")),
        name="inproj",
    )(x, mod3, mod3, w_in, w_tail)


def _outproj_body(ro_ref, go_ref, y_ref, x_ref, g1_ref, wglu_ref, wout_ref, lnw_ref, lnb_ref, o_ref):
    y = y_ref[0]
    so = y * jax.nn.sigmoid(_bdot(y, wglu_ref[0]))
    acc = jnp.dot(ro_ref[0], wout_ref[0, 0:RET_W, :], preferred_element_type=F32)
    acc += jnp.dot(go_ref[0], wout_ref[0, RET_W:RET_W + GDN_W, :], preferred_element_type=F32)
    acc += jnp.dot(so.astype(BF16), wout_ref[0, RET_W + GDN_W:, :], preferred_element_type=F32)
    z = ALPHA * x_ref[0] + g1_ref[0] * acc
    o_ref[0] = _layernorm(z, lnw_ref[...], lnb_ref[...])


def _outproj(ro, go, y, x, mod3, wglu, wout, l, lnw, lnb, tm):
    B, L, _ = x.shape
    return pl.pallas_call(
        _outproj_body,
        grid=(B, L // tm),
        in_specs=[pl.BlockSpec((1, tm, RET_W), lambda b, i: (b, i, 0)),
                  pl.BlockSpec((1, tm, GDN_W), lambda b, i: (b, i, 0)),
                  pl.BlockSpec((1, tm, SSM_W), lambda b, i: (b, i, 0)),
                  pl.BlockSpec((1, tm, D_MODEL), lambda b, i: (b, i, 0)),
                  _mod_spec(mod3, tm, 2, 2),
                  pl.BlockSpec((1, SSM_W, SSM_W), lambda b, i: (l, 0, 0)),
                  pl.BlockSpec((1, D_MODEL, D_MODEL), lambda b, i: (l, 0, 0)),
                  pl.BlockSpec((1, D_MODEL), lambda b, i: (0, 0)),
                  pl.BlockSpec((1, D_MODEL), lambda b, i: (0, 0))],
        out_specs=pl.BlockSpec((1, tm, D_MODEL), lambda b, i: (b, i, 0)),
        out_shape=jax.ShapeDtypeStruct((B, L, D_MODEL), F32),
        compiler_params=_params(("parallel", "parallel")),
        name="outproj_ln1",
    )(ro, go, y, x, mod3, wglu, wout, lnw, lnb)


def _ffn_body(x_ref, sc_ref, sh_ref, g2_ref, wg_ref, wu_ref, wo_ref, lnw_ref, lnb_ref, o_ref, h_ref):
    f = pl.program_id(2)

    @pl.when(f == 0)
    def _():
        h_ref[...] = (x_ref[0] * (1.0 + sc_ref[0]) + sh_ref[0]).astype(BF16)
        o_ref[...] = jnp.zeros_like(o_ref)

    h = h_ref[...]
    gate = jnp.dot(h, wg_ref[0], preferred_element_type=F32)
    up = jnp.dot(h, wu_ref[0], preferred_element_type=F32)
    act = (_silu(gate) * up).astype(BF16)
    o_ref[0] += jnp.dot(act, wo_ref[0], preferred_element_type=F32)

    @pl.when(f == pl.num_programs(2) - 1)
    def _():
        z = ALPHA * x_ref[0] + g2_ref[0] * o_ref[0]
        o_ref[0] = _layernorm(z, lnw_ref[...], lnb_ref[...])


def _ffn(x, mod3, w_in, w_out, l, lnw, lnb, tm, tf):
    B, L, _ = x.shape
    nf = D_FF // tf
    return pl.pallas_call(
        _ffn_body,
        grid=(B, L // tm, nf),
        in_specs=[pl.BlockSpec((1, tm, D_MODEL), lambda b, i, f: (b, i, 0)),
                  _mod_spec(mod3, tm, 4, 3),
                  _mod_spec(mod3, tm, 3, 3),
                  _mod_spec(mod3, tm, 5, 3),
                  pl.BlockSpec((1, D_MODEL, tf), lambda b, i, f: (l, 0, f)),
                  pl.BlockSpec((1, D_MODEL, tf), lambda b, i, f: (l, 0, f + nf)),
                  pl.BlockSpec((1, tf, D_MODEL), lambda b, i, f: (l, f, 0)),
                  pl.BlockSpec((1, D_MODEL), lambda b, i, f: (0, 0)),
                  pl.BlockSpec((1, D_MODEL), lambda b, i, f: (0, 0))],
        out_specs=pl.BlockSpec((1, tm, D_MODEL), lambda b, i, f: (b, i, 0)),
        out_shape=jax.ShapeDtypeStruct((B, L, D_MODEL), F32),
        scratch_shapes=[pltpu.VMEM((tm, D_MODEL), BF16)],
        compiler_params=_params(("parallel", "parallel", "arbitrary")),
        name="ffn_ln2",
    )(x, mod3, mod3, mod3, w_in, w_in, w_out, lnw, lnb)


def _rope_body(cp_ref, sp_ref, cs_ref, ss_ref):
    half = RET_DK // 2
    for c_ref, s_ref, base, step in ((cp_ref, sp_ref, 0, 1), (cs_ref, ss_ref, PAST_LEN, 0)):
        shape = c_ref.shape
        lane = lax.broadcasted_iota(jnp.int32, shape, 1)
        row = lax.broadcasted_iota(jnp.int32, shape, 0)
        freq = jnp.exp(jnp.where(lane < half, lane, lane - half).astype(F32) * (-math.log(ROPE_BASE) / half))
        ang = (row * step + base).astype(F32) * freq
        c_ref[...] = jnp.cos(ang)
        s = jnp.sin(ang)
        s_ref[...] = jnp.where(lane < RET_DK // 2, -s, s)


def _rope_tables(seq):
    return pl.pallas_call(
        _rope_body,
        out_shape=[jax.ShapeDtypeStruct((seq, RET_DK), F32), jax.ShapeDtypeStruct((seq, RET_DK), F32),
                   jax.ShapeDtypeStruct((8, RET_DK), F32), jax.ShapeDtypeStruct((8, RET_DK), F32)],
        name="rope_tables",
    )()


def _rope(x, cos, sin_signed):
    return x * cos + pltpu.roll(x, RET_DK // 2, 1) * sin_signed


def _ret_prompt_body(q_ref, k_ref, v_ref, g_ref, cos_ref, sin_ref, gnw_ref, o_ref, s_ref, S):
    c = pl.program_id(1)
    C = RET_CHUNK

    @pl.when(c == 0)
    def _():
        S[...] = jnp.zeros_like(S)

    cos = cos_ref[...]
    sin = sin_ref[...]
    row = lax.broadcasted_iota(jnp.int32, (C, C), 0).astype(F32)
    col = lax.broadcasted_iota(jnp.int32, (C, C), 1).astype(F32)
    diff = row - col
    for h in range(RET_HEADS):
        lg = RET_LOG_G[h]
        sl = slice(h * RET_DK, (h + 1) * RET_DK)
        dmask = jnp.exp(jnp.where(diff >= 0, lg * diff, -jnp.inf))
        k_decay = jnp.exp(lg * (C - 1 - row))
        q_decay = jnp.exp(lg * (row + 1.0))
        for bt in range(q_ref.shape[0]):
            q = _rope(q_ref[bt, :, sl], cos, sin)
            k = _rope(k_ref[bt, :, sl], cos, sin) * (RET_DK ** -0.5)
            v = v_ref[bt, :, sl]
            scores = _bdot_nt(q, k) * dmask
            intra = _bdot(scores, v)
            kv = _bdot_tn(k * k_decay, v)
            s_prev = S[bt, h]
            cross = _bdot(q * q_decay, s_prev)
            S[bt, h] = s_prev * math.exp(lg * C) + kv
            ro = intra + cross
            mu = jnp.mean(ro, axis=-1, keepdims=True)
            rc = ro - mu
            var = jnp.mean(rc * rc, axis=-1, keepdims=True)
            rn = rc * lax.rsqrt(var + LN_EPS) * gnw_ref[:, sl] * _silu(g_ref[bt, :, sl])
            o_ref[bt, :, sl] = rn.astype(BF16)

    @pl.when(c == pl.num_programs(1) - 1)
    def _():
        s_ref[...] = S[...]


def _ret_prompt(proj, cos, sin, gnw, nbt):
    B, L, _ = proj.shape
    C = RET_CHUNK
    blk = lambda k: pl.BlockSpec((nbt, C, RET_W), lambda b, c: (b, c, k))
    return pl.pallas_call(
        _ret_prompt_body,
        grid=(B // nbt, L // C),
        in_specs=[blk(0), blk(1), blk(2), blk(3),
                  pl.BlockSpec((C, RET_DK), lambda b, c: (c, 0)),
                  pl.BlockSpec((C, RET_DK), lambda b, c: (c, 0)),
                  pl.BlockSpec((1, RET_W), lambda b, c: (0, 0))],
        out_specs=[pl.BlockSpec((nbt, C, RET_W), lambda b, c: (b, c, 0)),
                   pl.BlockSpec((nbt, RET_HEADS, RET_DK, RET_DK), lambda b, c: (b, 0, 0, 0))],
        out_shape=[jax.ShapeDtypeStruct((B, L, RET_W), BF16),
                   jax.ShapeDtypeStruct((B, RET_HEADS, RET_DK, RET_DK), F32)],
        scratch_shapes=[pltpu.VMEM((nbt, RET_HEADS, RET_DK, RET_DK), F32)],
        compiler_params=_params(("parallel", "arbitrary")),
        name="ret_prompt",
    )(proj, proj, proj, proj, cos, sin, gnw)


def _ret_sample_body(q_ref, k_ref, v_ref, g_ref, cos_ref, sin_ref, gnw_ref, dec_ref, s0_ref, o_ref, s_ref, obuf):
    nb = q_ref.shape[0]
    cos = cos_ref[0:1, :]
    sin = sin_ref[0:1, :]
    gdec = dec_ref[0]
    q = _rope(q_ref[...], cos, sin)
    k = _rope(k_ref[...], cos, sin) * (RET_DK ** -0.5)
    v = v_ref[...]
    qk = jnp.sum(q * k, axis=1, keepdims=True)
    qT = q.T
    kT = k.T
    for b in range(nb):
        s0 = s0_ref[0, b, 0]
        qcol = qT[:, b:b + 1]
        kcol = kT[:, b:b + 1]
        vrow = v[b:b + 1, :]
        cross = gdec * jnp.sum(s0 * qcol, axis=0, keepdims=True)
        s_ref[0, b, 0] = s0 * gdec + kcol * vrow
        obuf[b:b + 1, :] = qk[b:b + 1, :] * vrow + cross
    ro = obuf[...]
    mu = jnp.mean(ro, axis=-1, keepdims=True)
    rc = ro - mu
    var = jnp.mean(rc * rc, axis=-1, keepdims=True)
    o_ref[...] = (rc * lax.rsqrt(var + LN_EPS) * gnw_ref[...] * _silu(g_ref[...])).astype(BF16)


def _state_phases(body, n_in, state_pos, carried):
    def phased(*refs):
        p = pl.program_id(0)

        @pl.when(p == 0)
        def _():
            body(*refs)

        @pl.when(p == 1)
        def _():
            refs[state_pos][...] = jnp.zeros(refs[state_pos].shape, F32)

    if carried is None:
        return phased, 2, [], [], {}

    def carried_body(*refs):
        phased(*refs[:n_in], *refs[n_in + 1:])

    return carried_body, 1, [pl.BlockSpec(memory_space=pl.ANY)], [carried], {n_in: 1}


def _ret_sample(proj2, cos, sin, gnw, state, l, carried):
    nb = proj2.shape[0]
    H = RET_HEADS
    dec = jnp.asarray(np.broadcast_to(np.exp(np.array(RET_LOG_G))[:, None, None], (H, 1, RET_DK)), F32)
    hh = lambda p, h: h * (1 - p) + (H - 1) * p
    blk = lambda k: pl.BlockSpec((nb, RET_DK), lambda p, h: (0, k * H + hh(p, h)))
    in_specs = [blk(0), blk(1), blk(2), blk(3),
                pl.BlockSpec((8, RET_DK), lambda p, h: (0, 0)),
                pl.BlockSpec((8, RET_DK), lambda p, h: (0, 0)),
                pl.BlockSpec((1, RET_DK), lambda p, h: (0, hh(p, h))),
                pl.BlockSpec((1, 1, RET_DK), lambda p, h: (hh(p, h), 0, 0)),
                pl.BlockSpec((1, nb, 1, RET_DK, RET_DK), lambda p, h: (l, 0, hh(p, h), 0, 0))]
    body, phases, extra_specs, extra_args, aliases = _state_phases(_ret_sample_body, len(in_specs),
                                                                   len(in_specs) + 1, carried)
    return pl.pallas_call(
        body,
        grid=(phases, H),
        in_specs=in_specs + extra_specs,
        out_specs=[pl.BlockSpec((nb, RET_DK), lambda p, h: (0, hh(p, h))),
                   pl.BlockSpec((1, nb, 1, RET_DK, RET_DK), lambda p, h: (l + p, 0, h, 0, 0))],
        out_shape=[jax.ShapeDtypeStruct((nb, RET_W), BF16),
                   jax.ShapeDtypeStruct(state.shape, F32)],
        scratch_shapes=[pltpu.VMEM((nb, RET_DK), F32)],
        input_output_aliases=aliases,
        compiler_params=_params(("arbitrary", "arbitrary")),
        name="ret_sample",
    )(proj2, proj2, proj2, proj2, cos, sin, gnw, dec, state, *extra_args)


def _gdn_prompt_body(dq_ref, dk_ref, dv_ref, dz_ref, ab_ref, cw_ref, alog_ref, dtb_ref, nw_ref,
                     o_ref, s_ref, conv_ref, xbuf, S):
    c = pl.program_id(1)
    C = GDN_CHUNK
    H = GDN_HEADS
    dk = GDN_DK
    nbt = dq_ref.shape[0]

    @pl.when(c == 0)
    def _():
        S[...] = jnp.zeros_like(S)
        xbuf[:, 0:8, :] = jnp.zeros((nbt, 8, GDN_CONV_C), F32)

    row = lax.broadcasted_iota(jnp.int32, (C, dk), 0)
    lane = lax.broadcasted_iota(jnp.int32, (C, dk), 1)
    tri = lane <= row
    strict = lane < row
    r3 =lax.broadcasted_iota(jnp.int32, (C, 3 * C), 0)
    c3 = lax.broadcasted_iota(jnp.int32, (C, 3 * C), 1) % C
    tril_b = jnp.where(r3 >= c3, 1.0, 0.0).astype(BF16)

    def cumsum_rows(x):
        hi = x.astype(BF16)
        r1 = x - hi.astype(F32)
        mid = r1.astype(BF16)
        lo = (r1 - mid.astype(F32)).astype(BF16)
        return jnp.dot(tril_b, jnp.concatenate([hi, mid, lo], axis=0), preferred_element_type=F32)

    def lanes(x, j):
        return jnp.broadcast_to(x[:, j:j + 1], (x.shape[0], dk))

    cw = cw_ref[...]
    zrow_b = jnp.zeros((C, dk), BF16)

    items = []
    for bt in range(nbt):
        xbuf[bt, 8:8 + C, 0:GDN_W] = dq_ref[bt]
        xbuf[bt, 8:8 + C, GDN_W:2 * GDN_W] = dk_ref[bt]
        xbuf[bt, 8:8 + C, 2 * GDN_W:3 * GDN_W] = dv_ref[bt]

        def conv(off, bt=bt):
            y = xbuf[bt, 5:5 + C, off:off + dk] * cw[0:1, off:off + dk]
            for i in range(1, CONV_W):
                y = y + xbuf[bt, 5 + i:5 + i + C, off:off + dk] * cw[i:i + 1, off:off + dk]
            return _silu(y)

        ab = ab_ref[bt]
        g_c = -jnp.exp(alog_ref[...]) * jax.nn.softplus(ab + dtb_ref[...])
        gc_c = cumsum_rows(g_c)
        beta_c = jax.nn.sigmoid(ab)
        egc_c = jnp.exp(gc_c)
        gl_c = gc_c[C - 1:C, :]
        kdec_c = jnp.exp(gl_c - gc_c)
        egl_c = jnp.exp(gl_c)
        gc_t = jnp.concatenate([gc_c, jnp.zeros_like(gc_c)], axis=0).T

        for h in range(H):
            sl = slice(h * dk, (h + 1) * dk)
            q = conv(h * dk)
            k = conv(GDN_W + h * dk)
            v = conv(2 * GDN_W + h * dk)
            q = q * lax.rsqrt(jnp.sum(q * q, axis=-1, keepdims=True) + L2_EPS) * (dk ** -0.5)
            k = k * lax.rsqrt(jnp.sum(k * k, axis=-1, keepdims=True) + L2_EPS)
            beta = lanes(beta_c, H + h)
            egc = lanes(egc_c, h)
            kb = k * beta
            kq = jnp.concatenate([kb, q], axis=0).astype(BF16)
            k2 = jnp.concatenate([k.astype(BF16), zrow_b], axis=0)
            kkqk = lax.dot_general(kq, k2, NT, preferred_element_type=F32)
            dgc = lanes(gc_c, h) - jnp.broadcast_to(gc_t[h:h + 1, :], (C, dk))
            decay = jnp.exp(jnp.where(tri, dgc, -jnp.inf))
            a = jnp.where(strict, kkqk[0:C] * decay, 0.0)
            attn = (kkqk[C:2 * C] * decay)[:, 0:C].astype(BF16)
            rhs = jnp.concatenate([v * beta, kb * egc], axis=1).astype(BF16)
            items.append(dict(bt=bt, h=h, a=a, rhs=rhs, attn=attn, qd=(q * egc).astype(BF16),
                              kd=(k * lanes(kdec_c, h)).astype(BF16), egl=lanes(egl_c, h)))

    pair_col = lane % C
    eye_pair = (pair_col == row).astype(F32)
    left = lane < C

    def level_mask(s):
        br = row // s
        return (pair_col // s) == jnp.where(br % 2 == 1, br - 1, -1)

    def block_diag(x):
        zero = jnp.zeros_like(x)
        return jnp.concatenate([jnp.where(left, x, zero), jnp.where(left, zero, x)], axis=0)

    pairs = []
    for i in range(0, len(items), 2):
        a_pair = items[i]["a"] + pltpu.roll(items[i + 1]["a"], C, 1)
        pairs.append(dict(a=a_pair, d=eye_pair - jnp.where(level_mask(1), a_pair, 0.0)))
    s_blk = 2
    while s_blk < C:
        mask = level_mask(s_blk)
        for p in pairs:
            d = p["d"]
            sub = block_diag(jnp.where(mask, p["a"], 0.0)).astype(BF16)
            g = jnp.dot(d.astype(BF16), sub, preferred_element_type=F32)
            p["d"] = d - jnp.dot(g.astype(BF16), block_diag(d).astype(BF16), preferred_element_type=F32)
        s_blk *= 2
    for i, p in enumerate(pairs):
        r0, r1 = items[2 * i]["rhs"], items[2 * i + 1]["rhs"]
        zero = jnp.zeros_like(r0)
        rr = jnp.concatenate([jnp.concatenate([r0, zero], axis=1), jnp.concatenate([zero, r1], axis=1)], axis=0)
        sol = jnp.dot(p["d"].astype(BF16), rr, preferred_element_type=F32)
        items[2 * i]["sol"] = sol[:, 0:2 * dk]
        items[2 * i + 1]["sol"] = sol[:, 2 * dk:4 * dk]

    for it in items:
        bt, h = it["bt"], it["h"]
        sl = slice(h * dk, (h + 1) * dk)
        u = it["sol"][:, 0:dk]
        w = it["sol"][:, dk:2 * dk]
        s_prev = S[bt, h]
        wq = jnp.concatenate([w.astype(BF16), it["qd"]], axis=0)
        r = jnp.dot(wq, s_prev.astype(BF16), preferred_element_type=F32)
        v_new = u - r[0:C]
        o = r[C:2 * C] + jnp.dot(it["attn"], v_new.astype(BF16), preferred_element_type=F32)
        S[bt, h] = s_prev * it["egl"] + lax.dot_general(it["kd"], v_new.astype(BF16), TN, preferred_element_type=F32)
        o = o * lax.rsqrt(jnp.mean(o * o, axis=-1, keepdims=True) + RMS_EPS) * nw_ref[...]
        o_ref[bt, :, sl] = (o * _silu(dz_ref[bt, :, sl])).astype(BF16)

    for bt in range(nbt):
        xbuf[bt, 0:8, :] = xbuf[bt, C:C + 8, :]

    @pl.when(c == pl.num_programs(1) - 1)
    def _():
        s_ref[...] = S[...]
        conv_ref[...] = xbuf[:, C + 8 - (CONV_W - 1):C + 8, :]


def _gdn_prompt(proj, cw, alog_bc, dtb_bc, nw, nbt):
    B, L, _ = proj.shape
    C = GDN_CHUNK
    H = GDN_HEADS
    blk = lambda k: pl.BlockSpec((nbt, C, GDN_W), lambda b, c: (b, c, k))
    full = lambda shape: pl.BlockSpec(shape, lambda b, c: tuple(0 for _ in shape))
    return pl.pallas_call(
        _gdn_prompt_body,
        grid=(B // nbt, L // C),
        in_specs=[blk(2), blk(3), blk(4), blk(5),
                  pl.BlockSpec((nbt, C, 128), lambda b, c: (b, c, COL_AB // 128)),
                  full((CONV_W, GDN_CONV_C)), full((1, 128)), full((1, 128)), full((1, GDN_DK))],
        out_specs=[pl.BlockSpec((nbt, C, GDN_W), lambda b, c: (b, c, 0)),
                   pl.BlockSpec((nbt, H, GDN_DK, GDN_DK), lambda b, c: (b, 0, 0, 0)),
                   pl.BlockSpec((nbt, CONV_W - 1, GDN_CONV_C), lambda b, c: (b, 0, 0))],
        out_shape=[jax.ShapeDtypeStruct((B, L, GDN_W), BF16),
                   jax.ShapeDtypeStruct((B, H, GDN_DK, GDN_DK), F32),
                   jax.ShapeDtypeStruct((B, CONV_W - 1, GDN_CONV_C), F32)],
        scratch_shapes=[pltpu.VMEM((nbt, C + 8, GDN_CONV_C), F32),
                        pltpu.VMEM((nbt, H, GDN_DK, GDN_DK), F32)],
        compiler_params=_params(("parallel", "arbitrary")),
        name="gdn_prompt",
    )(proj, proj, proj, proj, proj, cw, alog_bc, dtb_bc, nw)


def _gdn_sample_body(dq_ref, dk_ref, dv_ref, dz_ref, ab_ref, csq_ref, csk_ref, csv_ref, cwq_ref, cwk_ref, cwv_ref,
                     alog_ref, dtb_ref, nw_ref, s0_ref, o_ref, s_ref, nq_ref, nk_ref, nv_ref, obuf):
    h = pl.program_id(1)
    nb = dq_ref.shape[0]
    dk = GDN_DK

    def conv(x_ref, cs_ref, w_ref, n_ref):
        x = x_ref[...]
        y = cs_ref[0] * w_ref[0:1, :] + cs_ref[1] * w_ref[1:2, :] + cs_ref[2] * w_ref[2:3, :] + x * w_ref[3:4, :]
        n_ref[0] = cs_ref[1]
        n_ref[1] = cs_ref[2]
        n_ref[2] = x
        return _silu(y)

    q = conv(dq_ref, csq_ref, cwq_ref, nq_ref)
    k = conv(dk_ref, csk_ref, cwk_ref, nk_ref)
    v = conv(dv_ref, csv_ref, cwv_ref, nv_ref)
    q = q * lax.rsqrt(jnp.sum(q * q, axis=-1, keepdims=True) + L2_EPS) * (dk ** -0.5)
    k = k * lax.rsqrt(jnp.sum(k * k, axis=-1, keepdims=True) + L2_EPS)
    er = lax.broadcasted_iota(jnp.int32, (128, dk), 0)
    ab = ab_ref[...]
    a_bc = jnp.dot(ab, (er == h).astype(F32), precision=HI, preferred_element_type=F32)
    b_bc = jnp.dot(ab, (er == h + GDN_HEADS).astype(F32), precision=HI, preferred_element_type=F32)
    g = -jnp.exp(alog_ref[0]) * jax.nn.softplus(a_bc + dtb_ref[0])
    eg = jnp.exp(g)
    beta = jax.nn.sigmoid(b_bc)
    qk = jnp.sum(q * k, axis=1, keepdims=True)
    qT = q.T
    kT = k.T
    for b in range(nb):
        s0 = s0_ref[0, b, 0]
        qcol = qT[:, b:b + 1]
        kcol = kT[:, b:b + 1]
        ks = jnp.sum(s0 * kcol, axis=0, keepdims=True)
        qs = jnp.sum(s0 * qcol, axis=0, keepdims=True)
        bb = beta[b:b + 1, :]
        egb = eg[b:b + 1, :]
        v_new = bb * v[b:b + 1, :] - bb * egb * ks
        s_ref[0, b, 0] = s0 * egb + kcol * v_new
        obuf[b:b + 1, :] = egb * qs + qk[b:b + 1, :] * v_new
    o = obuf[...]
    o = o * lax.rsqrt(jnp.mean(o * o, axis=-1, keepdims=True) + RMS_EPS) * nw_ref[...]
    o_ref[...] = (o * _silu(dz_ref[...])).astype(BF16)


def _gdn_sample(proj2, cs_t, cw, alog_t, dtb_t, nw, state, l, carried):
    nb = proj2.shape[0]
    H = GDN_HEADS
    dk = GDN_DK
    hh = lambda p, h: h * (1 - p) + (H - 1) * p
    blk = lambda k: pl.BlockSpec((nb, dk), lambda p, h: (0, k * H + hh(p, h)))
    csblk = lambda k: pl.BlockSpec((CONV_W - 1, nb, dk), lambda p, h: (0, 0, k * H + hh(p, h)))
    cwblk = lambda k: pl.BlockSpec((CONV_W, dk), lambda p, h: (0, k * H + hh(p, h)))
    tab = pl.BlockSpec((1, 1, dk), lambda p, h: (hh(p, h), 0, 0))
    nblk = pl.BlockSpec((CONV_W - 1, nb, dk), lambda p, h: (0, 0, hh(p, h)))
    nshape = jax.ShapeDtypeStruct((CONV_W - 1, nb, GDN_W), F32)
    in_specs = [blk(2), blk(3), blk(4), blk(5),
                pl.BlockSpec((nb, 128), lambda p, h: (0, COL_AB // 128)),
                csblk(0), csblk(1), csblk(2), cwblk(0), cwblk(1), cwblk(2),
                tab, tab, pl.BlockSpec((1, dk), lambda p, h: (0, 0)),
                pl.BlockSpec((1, nb, 1, dk, dk), lambda p, h: (l, 0, hh(p, h), 0, 0))]
    body, phases, extra_specs, extra_args, aliases = _state_phases(_gdn_sample_body, len(in_specs),
                                                                   len(in_specs) + 1, carried)
    return pl.pallas_call(
        body,
        grid=(phases, H),
        in_specs=in_specs + extra_specs,
        out_specs=[pl.BlockSpec((nb, dk), lambda p, h: (0, hh(p, h))),
                   pl.BlockSpec((1, nb, 1, dk, dk), lambda p, h: (l + p, 0, h, 0, 0)), nblk, nblk, nblk],
        out_shape=[jax.ShapeDtypeStruct((nb, GDN_W), BF16), jax.ShapeDtypeStruct(state.shape, F32),
                   nshape, nshape, nshape],
        scratch_shapes=[pltpu.VMEM((nb, dk), F32)],
        input_output_aliases=aliases,
        compiler_params=_params(("arbitrary", "arbitrary")),
        name="gdn_sample",
    )(proj2, proj2, proj2, proj2, proj2, cs_t, cs_t, cs_t, cw, cw, cw, alog_t, dtb_t, nw, state, *extra_args)


def _ssm_prep_body(are_ref, aim_ref, ldt_ref, bre_ref, bim_ref, lre_ref, lim_ref, obre_ref, obim_ref):
    ar = are_ref[0]
    ai = aim_ref[0]
    dt = jnp.exp(ldt_ref[0])
    er = jnp.exp(ar * dt)
    lr = er * jnp.cos(ai * dt)
    li = er * jnp.sin(ai * dt)
    lre_ref[0] = lr
    lim_ref[0] = li
    xr = lr - 1.0
    den = ar * ar + ai * ai
    cr = ((xr * ar + li * ai) / den)[:, None, :]
    ci = ((li * ar - xr * ai) / den)[:, None, :]
    br = bre_ref[0]
    bi = bim_ref[0]
    obre_ref[0] = cr * br - ci * bi
    obim_ref[0] = cr * bi + ci * br


def _ssm_prep(a_re, a_im, log_dt, b_re, b_im):
    G, N, P = SSM_GROUPS, SSM_N, SSM_GROUP
    ldt = jnp.broadcast_to(log_dt[:, :, None], (DEPTH, G, N))
    bt_re = jnp.transpose(b_re, (0, 1, 3, 2))
    bt_im = jnp.transpose(b_im, (0, 1, 3, 2))
    s2 = pl.BlockSpec((1, G, N), lambda l: (l, 0, 0))
    s3 = pl.BlockSpec((1, G, P, N), lambda l: (l, 0, 0, 0))
    return pl.pallas_call(
        _ssm_prep_body,
        grid=(DEPTH,),
        in_specs=[s2, s2, s2, s3, s3],
        out_specs=[s2, s2, s3, s3],
        out_shape=[jax.ShapeDtypeStruct((DEPTH, G, N), F32), jax.ShapeDtypeStruct((DEPTH, G, N), F32),
                   jax.ShapeDtypeStruct((DEPTH, G, P, N), F32), jax.ShapeDtypeStruct((DEPTH, G, P, N), F32)],
        name="ssm_prep",
    )(a_re, a_im, ldt, bt_re, bt_im)


def _ssm_tile_blockdiag(re, im):
    x = jnp.stack([re, im]).reshape(2, SSM_TILES, SSM_TILE_G, SSM_GROUP, SSM_N)
    eye = jnp.eye(SSM_TILE_G, dtype=x.dtype)
    y = x[:, :, :, :, None, :] * eye[None, None, :, None, :, None]
    y = jnp.transpose(y, (1, 2, 3, 0, 4, 5))
    return y.reshape(SSM_TILES, SSM_TILE_G * SSM_GROUP, SSM_TILE_W)


def _ssm_state_to_tiles(re, im):
    B = re.shape[0]
    x = jnp.stack([re, im], axis=1).reshape(B, 2, SSM_TILES, SSM_HALF)
    return jnp.transpose(x, (0, 2, 1, 3)).reshape(B, SSM_TILES * SSM_TILE_W)


def _ssm_tiles_to_state(x):
    B = x.shape[0]
    x = jnp.transpose(x.reshape(B, SSM_TILES, 2, SSM_HALF), (0, 2, 1, 3)).reshape(B, 2, SSM_GROUPS, SSM_N)
    return x[:, 0], x[:, 1]


def _cmul(ar, ai, br, bi):
    return ar * br - ai * bi, ar * bi + ai * br


def _s5_prompt_body(u_ref, bt_ref, ct_ref, lam_ref, d_ref, y_ref, hl_ref, buf):
    L = u_ref.shape[1]
    nseg = L // SSM_SEG
    ns = SSM_HALF // 128
    bmat = bt_ref[0].astype(BF16)
    for i in range(nseg):
        bu = jnp.dot(u_ref[0, i * SSM_SEG:(i + 1) * SSM_SEG, :].astype(BF16), bmat, preferred_element_type=F32)
        for s in range(2 * ns):
            buf[s, i * SSM_PITCH:i * SSM_PITCH + SSM_SEG, :] = bu[:, s * 128:(s + 1) * 128]

    lam = lam_ref[0]
    a_re = [jnp.broadcast_to(lam[:, s * 128:(s + 1) * 128], (nseg, 128)) for s in range(ns)]
    a_im = [jnp.broadcast_to(lam[:, SSM_HALF + s * 128:SSM_HALF + (s + 1) * 128], (nseg, 128)) for s in range(ns)]

    def scan_step(t, carry):
        out = []
        for s in range(ns):
            hr, hi = carry[2 * s], carry[2 * s + 1]
            pr, pi = _cmul(a_re[s], a_im[s], hr, hi)
            hr = pr + buf[s, pl.ds(t, nseg, stride=SSM_PITCH), :]
            hi = pi + buf[ns + s, pl.ds(t, nseg, stride=SSM_PITCH), :]
            buf[s, pl.ds(t, nseg, stride=SSM_PITCH), :] = hr
            buf[ns + s, pl.ds(t, nseg, stride=SSM_PITCH), :] = hi
            out += [hr, hi]
        return tuple(out)

    zero = jnp.zeros((nseg, 128), F32)
    ends = lax.fori_loop(0, SSM_SEG, scan_step, tuple(zero for _ in range(2 * ns)), unroll=SSM_UNROLL)

    sub = lax.broadcasted_iota(jnp.int32, (nseg, 128), 0)
    carries = []
    for s in range(ns):
        pr, pi = a_re[s], a_im[s]
        for _ in range(int(math.log2(SSM_SEG))):
            pr, pi = _cmul(pr, pi, pr, pi)
        xr, xi = ends[2 * s], ends[2 * s + 1]
        shift = 1
        while shift < nseg:
            sr = jnp.where(sub >= shift, pltpu.roll(xr, shift, 0), 0.0)
            si = jnp.where(sub >= shift, pltpu.roll(xi, shift, 0), 0.0)
            mr, mi = _cmul(pr, pi, sr, si)
            xr, xi = xr + mr, xi + mi
            pr, pi = _cmul(pr, pi, pr, pi)
            shift *= 2
        hl_ref[0, 0, :, s * 128:(s + 1) * 128] = xr[nseg - 1:nseg, :]
        hl_ref[0, 0, :, SSM_HALF + s * 128:SSM_HALF + (s + 1) * 128] = xi[nseg - 1:nseg, :]
        carries += [jnp.where(sub >= 1, pltpu.roll(xr, 1, 0), 0.0), jnp.where(sub >= 1, pltpu.roll(xi, 1, 0), 0.0)]

    def fix_step(t, pw):
        out = []
        for s in range(ns):
            pr, pi = pw[2 * s], pw[2 * s + 1]
            fr, fi = _cmul(pr, pi, carries[2 * s], carries[2 * s + 1])
            buf[s, pl.ds(t, nseg, stride=SSM_PITCH), :] += fr
            buf[ns + s, pl.ds(t, nseg, stride=SSM_PITCH), :] += fi
            nr, ni = _cmul(pr, pi, a_re[s], a_im[s])
            out += [nr, ni]
        return tuple(out)

    pw0 = []
    for s in range(ns):
        pw0 += [a_re[s], a_im[s]]
    lax.fori_loop(0, SSM_SEG, fix_step, tuple(pw0), unroll=SSM_UNROLL)

    cmat = ct_ref[0]
    cmat = jnp.concatenate([cmat[:, 0:SSM_HALF], -cmat[:, SSM_HALF:]], axis=1).astype(BF16)
    for i in range(nseg):
        rows = slice(i * SSM_PITCH, i * SSM_PITCH + SSM_SEG)
        h = jnp.concatenate([buf[s, rows, :].astype(BF16) for s in range(2 * ns)], axis=1)
        y = lax.dot_general(h, cmat, NT, preferred_element_type=F32)
        u = u_ref[0, i * SSM_SEG:(i + 1) * SSM_SEG, :]
        y_ref[0, i * SSM_SEG:(i + 1) * SSM_SEG, :] = jax.nn.gelu(y + d_ref[...] * u)


def _s5_prompt(proj, bt, ct, lam, d):
    B, L, _ = proj.shape
    nseg = L // SSM_SEG
    tile = lambda shape: pl.BlockSpec(shape, lambda b, j: (j,) + tuple(0 for _ in shape[1:]))
    return pl.pallas_call(
        _s5_prompt_body,
        grid=(B, SSM_TILES),
        in_specs=[pl.BlockSpec((1, L, 128), lambda b, j: (b, 0, COL_SU // 128 + j)),
                  tile((1, 128, SSM_TILE_W)), tile((1, 128, SSM_TILE_W)), tile((1, 1, SSM_TILE_W)),
                  pl.BlockSpec((1, 128), lambda b, j: (0, j))],
        out_specs=[pl.BlockSpec((1, L, 128), lambda b, j: (b, 0, j)),
                   pl.BlockSpec((1, 1, 1, SSM_TILE_W), lambda b, j: (b, j, 0, 0))],
        out_shape=[jax.ShapeDtypeStruct((B, L, SSM_W), F32),
                   jax.ShapeDtypeStruct((B, SSM_TILES, 1, SSM_TILE_W), F32)],
        scratch_shapes=[pltpu.VMEM((SSM_TILE_W // 128, nseg * SSM_PITCH, 128), F32)],
        compiler_params=_params(("parallel", "parallel")),
        name="s5_prompt",
    )(proj, bt, ct, lam, d)


def _s5_sample_body(u_ref, bt_ref, ct_ref, lam_ref, d_ref, h0_ref, y_ref, h_ref):
    u = u_ref[...]
    bu = jnp.dot(u, bt_ref[0], precision=HI, preferred_element_type=F32)
    lam = lam_ref[0]
    lr, li = lam[:, 0:SSM_HALF], lam[:, SSM_HALF:]
    h0r, h0i = h0_ref[:, 0:SSM_HALF], h0_ref[:, SSM_HALF:]
    pr, pi = _cmul(lr, li, h0r, h0i)
    hr = pr + bu[:, 0:SSM_HALF]
    hi = pi + bu[:, SSM_HALF:]
    h_ref[:, 0:SSM_HALF] = hr
    h_ref[:, SSM_HALF:] = hi
    ct = ct_ref[0]
    y = lax.dot_general(hr, ct[:, 0:SSM_HALF], NT, precision=HI, preferred_element_type=F32)
    y -= lax.dot_general(hi, ct[:, SSM_HALF:], NT, precision=HI, preferred_element_type=F32)
    y_ref[...] = jax.nn.gelu(y + d_ref[...] * u)


def _s5_sample(proj2, bt, ct, lam, d, h0):
    nb = proj2.shape[0]
    tile = lambda shape: pl.BlockSpec(shape, lambda j: (j,) + tuple(0 for _ in shape[1:]))
    hblk = pl.BlockSpec((nb, SSM_TILE_W), lambda j: (0, j))
    return pl.pallas_call(
        _s5_sample_body,
        grid=(SSM_TILES,),
        in_specs=[pl.BlockSpec((nb, 128), lambda j: (0, COL_SU // 128 + j)),
                  tile((1, 128, SSM_TILE_W)), tile((1, 128, SSM_TILE_W)), tile((1, 1, SSM_TILE_W)),
                  pl.BlockSpec((1, 128), lambda j: (0, j)), hblk],
        out_specs=[pl.BlockSpec((nb, 128), lambda j: (0, j)), hblk],
        out_shape=[jax.ShapeDtypeStruct((nb, SSM_W), F32), jax.ShapeDtypeStruct(h0.shape, F32)],
        compiler_params=_params(("parallel",)),
        name="s5_sample",
    )(proj2, bt, ct, lam, d, h0)


def kernel(x_prompt, x_sample, c_prompt, c_sample, state_ret, state_gdn, state_conv, state_ssm_re, state_ssm_im, w_mod, b_mod, w_in, conv_w, ret_gn_w, gdn_a_log, gdn_dt_bias, gdn_norm_w, ssm_a_re, ssm_a_im, ssm_log_dt, ssm_b_re, ssm_b_im, ssm_c_re, ssm_c_im, ssm_d, ssm_w_glu, w_out, ln1_w, ln1_b, w_ffn_in, w_ffn_out, ln2_w, ln2_b):
    Bp, Lp, _ = x_prompt.shape
    Bs = x_sample.shape[0]
    assert x_sample.shape[1] == 1 and Bs % 8 == 0

    pad_rows = (-(Bp + Bs)) % 8
    c_all = jnp.concatenate([c_prompt, c_sample, jnp.zeros((pad_rows, D_MODEL), F32)], axis=0)
    mod_all = _modulation(c_all, w_mod, b_mod)
    lam_re, lam_im, bbar_re, bbar_im = _ssm_prep(ssm_a_re, ssm_a_im, ssm_log_dt, ssm_b_re, ssm_b_im)
    cos_p, sin_p, cos_s, sin_s = _rope_tables(Lp)

    w_in_b = w_in.astype(BF16)
    w_out_b = _to_bf16(w_out, rows=512)
    w_glu_b = ssm_w_glu.astype(BF16)
    w_ffi_b = _to_bf16(w_ffn_in, rows=128)
    w_ffo_b = _to_bf16(w_ffn_out, rows=512)

    yp = x_prompt
    ys = x_sample.reshape(1, Bs, D_MODEL)
    outs_p, outs_s = [], []
    ret_s = gdn_s = None
    for l in range(DEPTH):
        w_tail = jnp.concatenate(
            [w_in_b[l, :, COL_SU + 2 * GDN_HEADS:], w_in_b[l, :, COL_SU:COL_SU + 2 * GDN_HEADS],
             jnp.zeros((D_MODEL, N_PROJ - N_IN), BF16)], axis=1)
        mod_p = mod_all[l, :Bp][:, None, :]
        mod_s = mod_all[l, Bp:Bp + Bs][None]
        gnw = ret_gn_w[l][None, :]
        nw = gdn_norm_w[l][None, :]
        alog_bc = jnp.pad(gdn_a_log[l], (0, 128 - GDN_HEADS))[None, :]
        dtb_bc = jnp.pad(gdn_dt_bias[l], (0, 128 - GDN_HEADS))[None, :]
        alog_t = jnp.broadcast_to(gdn_a_log[l][:, None, None], (GDN_HEADS, 1, GDN_DK))
        dtb_t = jnp.broadcast_to(gdn_dt_bias[l][:, None, None], (GDN_HEADS, 1, GDN_DK))
        ssm_bt = _ssm_tile_blockdiag(bbar_re[l], bbar_im[l])
        ssm_ct = _ssm_tile_blockdiag(ssm_c_re[l], ssm_c_im[l])
        ssm_lam = jnp.concatenate([lam_re[l].reshape(SSM_TILES, 1, SSM_HALF),
                                   lam_im[l].reshape(SSM_TILES, 1, SSM_HALF)], axis=-1)
        ssm_dl = ssm_d[l][None, :]
        lnw1, lnb1 = ln1_w[l][None, :], ln1_b[l][None, :]
        lnw2, lnb2 = ln2_w[l][None, :], ln2_b[l][None, :]

        proj = _inproj(yp, mod_p, w_in_b, l, w_tail, tm=1024, tn=PROJ_TN)
        ro, ret_p = _ret_prompt(proj, cos_p, sin_p, gnw, nbt=4)
        go, gdn_p, conv_p = _gdn_prompt(proj, conv_w[l], alog_bc, dtb_bc, nw, nbt=4)
        so, hl = _s5_prompt(proj, ssm_bt, ssm_ct, ssm_lam, ssm_dl)
        re_p, im_p = _ssm_tiles_to_state(hl.reshape(Bp, SSM_TILES * SSM_TILE_W))
        x1 = _outproj(ro, go, so, yp, mod_p, w_glu_b, w_out_b, l, lnw1, lnb1, tm=512)
        yp = _ffn(x1, mod_p, w_ffi_b, w_ffo_b, l, lnw2, lnb2, tm=512, tf=512)
        outs_p.append((ret_p, gdn_p, conv_p, re_p, im_p))

        proj = _inproj(ys, mod_s, w_in_b, l, w_tail, tm=Bs, tn=PROJ_TN)
        proj2 = proj.reshape(Bs, N_PROJ)
        ro, ret_s = _ret_sample(proj2, cos_s, sin_s, gnw, state_ret, l, ret_s)
        cs_t = jnp.transpose(state_conv[l], (1, 0, 2))
        go, gdn_s, nq, nk, nv = _gdn_sample(proj2, cs_t, conv_w[l], alog_t, dtb_t, nw, state_gdn, l, gdn_s)
        conv_s = jnp.transpose(jnp.concatenate([nq, nk, nv], axis=-1), (1, 0, 2))
        h0 = _ssm_state_to_tiles(state_ssm_re[l], state_ssm_im[l])
        so, hn = _s5_sample(proj2, ssm_bt, ssm_ct, ssm_lam, ssm_dl, h0)
        re_s, im_s = _ssm_tiles_to_state(hn)
        x1 = _outproj(ro[None], go[None], so[None], ys, mod_s, w_glu_b, w_out_b, l, lnw1, lnb1, tm=Bs)
        ys = _ffn(x1, mod_s, w_ffi_b, w_ffo_b, l, lnw2, lnb2, tm=Bs, tf=512)
        outs_s.append((conv_s, re_s, im_s))

    ret_p, gdn_p, conv_p, re_p, im_p = [jnp.stack(t) for t in zip(*outs_p)]
    conv_s, re_s, im_s = [jnp.stack(t) for t in zip(*outs_s)]
    return (yp, ys.reshape(Bs, 1, D_MODEL), ret_p, gdn_p, conv_p, re_p, im_p,
            ret_s, gdn_s, conv_s, re_s, im_s)
```

```python
import functools
import math

import numpy as np
import jax
import jax.numpy as jnp
from jax import lax
from jax.experimental import pallas as pl
from jax.experimental.pallas import tpu as pltpu

F32 = jnp.float32
BF16 = jnp.bfloat16

D_MODEL = 2048
DEPTH = 2
PAST_LEN = 16384
RET_HEADS = 4
RET_DK = 128
RET_W = 512
RET_CHUNK = 128
ROPE_BASE = 10000.0
GDN_HEADS = 8
GDN_DK = 128
GDN_W = 1024
GDN_CHUNK = 64
CONV_W = 4
GDN_CONV_C = 3072
SSM_W = 512
SSM_GROUP = 16
SSM_GROUPS = 32
SSM_N = 64
D_FF = 5632
ALPHA = (2 * DEPTH) ** 0.25
LN_EPS = 1e-5
RMS_EPS = 1e-6
L2_EPS = 1e-6

COL_SU = 6144
COL_AB = 6656
N_IN = 6672
PROJ_TN = 768
N_PROJ = COL_SU + PROJ_TN

SSM_TILES = 4
SSM_TILE_G = 8
SSM_HALF = SSM_TILE_G * SSM_N
SSM_TILE_W = 2 * SSM_HALF
SSM_SEG = 256
SSM_PITCH = 260
SSM_UNROLL = 4

VMEM_LIMIT = 56 * 1024 * 1024
OUTPROJ_ROW_CHUNK = 256
FFN_ROW_CHUNK = 256

RET_LOG_G = [math.log1p(-(2.0 ** (-5.0 - h))) for h in range(RET_HEADS)]

NT = (((1,), (1,)), ((), ()))
TN = (((0,), (0,)), ((), ()))
HI = lax.Precision.HIGHEST


def _params(sem):
    return pltpu.CompilerParams(dimension_semantics=sem, vmem_limit_bytes=VMEM_LIMIT)


def _silu(x):
    return x * jax.nn.sigmoid(x)


def _bdot(a, b):
    return jnp.dot(a.astype(BF16), b.astype(BF16), preferred_element_type=F32)


def _hdot(a, b):
    return jnp.dot(a, b, precision=HI, preferred_element_type=F32)


def _bdot_nt(a, b):
    return lax.dot_general(a.astype(BF16), b.astype(BF16), NT, preferred_element_type=F32)


def _bdot_tn(a, b):
    return lax.dot_general(a.astype(BF16), b.astype(BF16), TN, preferred_element_type=F32)


def _layernorm(z, w, b):
    mu = jnp.mean(z, axis=-1, keepdims=True)
    zc = z - mu
    var = jnp.mean(zc * zc, axis=-1, keepdims=True)
    return zc * lax.rsqrt(var + LN_EPS) * w + b


def _cast_body(x_ref, o_ref):
    o_ref[...] = x_ref[...].astype(o_ref.dtype)


def _to_bf16(w, rows):
    depth, r, n = w.shape
    spec = pl.BlockSpec((1, rows, n), lambda l, i: (l, i, 0))
    return pl.pallas_call(
        _cast_body,
        grid=(depth, r // rows),
        in_specs=[spec],
        out_specs=spec,
        out_shape=jax.ShapeDtypeStruct(w.shape, BF16),
        compiler_params=_params(("parallel", "parallel")),
        name="to_bf16",
    )(w)


def _mod_body(c_ref, w_ref, b_ref, o_ref):
    c = c_ref[...]
    o_ref[0] = _bdot(_silu(c), w_ref[0]) + b_ref[0]


def _modulation(c_all, w_mod, b_mod):
    rows = c_all.shape[0]
    tn = 1024
    n = 6 * D_MODEL
    return pl.pallas_call(
        _mod_body,
        grid=(DEPTH, n // tn),
        in_specs=[pl.BlockSpec((rows, D_MODEL), lambda l, j: (0, 0)),
                  pl.BlockSpec((1, D_MODEL, tn), lambda l, j: (l, 0, j)),
                  pl.BlockSpec((1, 1, tn), lambda l, j: (l, 0, j))],
        out_specs=pl.BlockSpec((1, rows, tn), lambda l, j: (l, 0, j)),
        out_shape=jax.ShapeDtypeStruct((DEPTH, rows, n), F32),
        compiler_params=_params(("parallel", "parallel")),
        name="modulation",
    )(c_all, w_mod, b_mod.reshape(DEPTH, 1, n))


def _mod_spec(mod3, tm, k, grid_rank):
    per_token = mod3.shape[1] != 1
    lmb = tm if per_token else 1
    if grid_rank == 3:
        return pl.BlockSpec((1, lmb, D_MODEL), lambda b, i, j: (b, i if per_token else 0, k))
    return pl.BlockSpec((1, lmb, D_MODEL), lambda b, i: (b, i if per_token else 0, k))


def _inproj_body(x_ref, sc_ref, sh_ref, w_ref, wt_ref, o_ref, h_ref, *, n_main):
    j = pl.program_id(2)
    tm = x_ref.shape[1]
    per_token = sc_ref.shape[1] != 1

    @pl.when(j == 0)
    def _():
        step = min(tm, FFN_ROW_CHUNK)
        for r in range(0, tm, step):
            rows = slice(r, r + step)
            sc = sc_ref[0, rows, :] if per_token else sc_ref[0]
            sh = sh_ref[0, rows, :] if per_token else sh_ref[0]
            h = (x_ref[0, rows, :] * (1.0 + sc) + sh).astype(BF16)
            h_ref[rows, :] = h
            o_ref[0, rows, :] = jnp.dot(h, w_ref[0], preferred_element_type=F32)

    @pl.when(jnp.logical_and(j > 0, j < n_main))
    def _():
        o_ref[0] = jnp.dot(h_ref[...], w_ref[0], preferred_element_type=F32)

    @pl.when(j == n_main)
    def _():
        o_ref[0] = jnp.dot(h_ref[...], wt_ref[...], preferred_element_type=F32)


def _inproj(x, mod3, w_in, l, w_tail, tm, tn):
    B, L, _ = x.shape
    n_main = COL_SU // tn
    assert n_main * tn == COL_SU and w_tail.shape == (D_MODEL, tn) and (n_main + 1) * tn == N_PROJ
    return pl.pallas_call(
        functools.partial(_inproj_body, n_main=n_main),
        grid=(B, L // tm, n_main + 1),
        in_specs=[pl.BlockSpec((1, tm, D_MODEL), lambda b, i, j: (b, i, 0)),
                  _mod_spec(mod3, tm, 1, 3),
                  _mod_spec(mod3, tm, 0, 3),
                  pl.BlockSpec((1, D_MODEL, tn), lambda b, i, j: (l, 0, jnp.minimum(j, n_main - 1))),
                  pl.BlockSpec((D_MODEL, tn), lambda b, i, j: (0, 0))],
        out_specs=pl.BlockSpec((1, tm, tn), lambda b, i, j: (b, i, j)),
        out_shape=jax.ShapeDtypeStruct((B, L, N_PROJ), F32),
        scratch_shapes=[pltpu.VMEM((tm, D_MODEL), BF16)],
        compiler_params=_params(("parallel", "parallel", "arbitrary")),
        name="inproj",
    )(x, mod3, mod3, w_in, w_tail)


def _outproj_body(ro_ref, go_ref, y_ref, x_ref, g1_ref, wglu_ref, wout_ref, lnw_ref, lnb_ref, o_ref, *, row_chunk):
    tm = x_ref.shape[1]
    for r in range(0, tm, row_chunk):
        rows = slice(r, r + row_chunk)
        y = y_ref[0, rows, :]
        so = y * jax.nn.sigmoid(_bdot(y, wglu_ref[0]))
        acc = jnp.dot(ro_ref[0, rows, :], wout_ref[0, 0:RET_W, :], preferred_element_type=F32)
        acc += jnp.dot(go_ref[0, rows, :], wout_ref[0, RET_W:RET_W + GDN_W, :], preferred_element_type=F32)
        acc += jnp.dot(so.astype(BF16), wout_ref[0, RET_W + GDN_W:, :], preferred_element_type=F32)
        g1 = g1_ref[0] if g1_ref.shape[1] == 1 else g1_ref[0, rows, :]
        z = ALPHA * x_ref[0, rows, :] + g1 * acc
        o_ref[0, rows, :] = _layernorm(z, lnw_ref[...], lnb_ref[...])


def _outproj(ro, go, y, x, mod3, wglu, wout, l, lnw, lnb, tm):
    B, L, _ = x.shape
    return pl.pallas_call(
        functools.partial(_outproj_body, row_chunk=min(tm, OUTPROJ_ROW_CHUNK)),
        grid=(B, L // tm),
        in_specs=[pl.BlockSpec((1, tm, RET_W), lambda b, i: (b, i, 0)),
                  pl.BlockSpec((1, tm, GDN_W), lambda b, i: (b, i, 0)),
                  pl.BlockSpec((1, tm, SSM_W), lambda b, i: (b, i, 0)),
                  pl.BlockSpec((1, tm, D_MODEL), lambda b, i: (b, i, 0)),
                  _mod_spec(mod3, tm, 2, 2),
                  pl.BlockSpec((1, SSM_W, SSM_W), lambda b, i: (l, 0, 0)),
                  pl.BlockSpec((1, D_MODEL, D_MODEL), lambda b, i: (l, 0, 0)),
                  pl.BlockSpec((1, D_MODEL), lambda b, i: (0, 0)),
                  pl.BlockSpec((1, D_MODEL), lambda b, i: (0, 0))],
        out_specs=pl.BlockSpec((1, tm, D_MODEL), lambda b, i: (b, i, 0)),
        out_shape=jax.ShapeDtypeStruct((B, L, D_MODEL), F32),
        compiler_params=_params(("parallel", "parallel")),
        name="outproj_ln1",
    )(ro, go, y, x, mod3, wglu, wout, lnw, lnb)


def _ffn_body(x_ref, sc_ref, sh_ref, g2_ref, wg_ref, wu_ref, wo_ref, lnw_ref, lnb_ref, o_ref, h_ref):
    f = pl.program_id(2)
    last = pl.num_programs(2) - 1
    tm = x_ref.shape[1]
    chunks = [slice(r, r + min(tm, FFN_ROW_CHUNK)) for r in range(0, tm, min(tm, FFN_ROW_CHUNK))]
    per_token = sc_ref.shape[1] != 1

    def mod(ref, rows):
        return ref[0, rows, :] if per_token else ref[0]

    def contribution(h):
        gate = jnp.dot(h, wg_ref[0], preferred_element_type=F32)
        up = jnp.dot(h, wu_ref[0], preferred_element_type=F32)
        act = (_silu(gate) * up).astype(BF16)
        return jnp.dot(act, wo_ref[0], preferred_element_type=F32)

    @pl.when(f == 0)
    def _():
        for rows in chunks:
            h = (x_ref[0, rows, :] * (1.0 + mod(sc_ref, rows)) + mod(sh_ref, rows)).astype(BF16)
            h_ref[rows, :] = h
            o_ref[0, rows, :] = contribution(h)

    @pl.when(jnp.logical_and(f > 0, f < last))
    def _():
        o_ref[0] += contribution(h_ref[...])

    @pl.when(f == last)
    def _():
        for rows in chunks:
            acc = o_ref[0, rows, :] + contribution(h_ref[rows, :])
            z = ALPHA * x_ref[0, rows, :] + mod(g2_ref, rows) * acc
            o_ref[0, rows, :] = _layernorm(z, lnw_ref[...], lnb_ref[...])


def _ffn(x, mod3, w_in, w_out, l, lnw, lnb, tm, tf):
    B, L, _ = x.shape
    nf = D_FF // tf
    return pl.pallas_call(
        _ffn_body,
        grid=(B, L // tm, nf),
        in_specs=[pl.BlockSpec((1, tm, D_MODEL), lambda b, i, f: (b, i, 0)),
                  _mod_spec(mod3, tm, 4, 3),
                  _mod_spec(mod3, tm, 3, 3),
                  _mod_spec(mod3, tm, 5, 3),
                  pl.BlockSpec((1, D_MODEL, tf), lambda b, i, f: (l, 0, f)),
                  pl.BlockSpec((1, D_MODEL, tf), lambda b, i, f: (l, 0, f + nf)),
                  pl.BlockSpec((1, tf, D_MODEL), lambda b, i, f: (l, f, 0)),
                  pl.BlockSpec((1, D_MODEL), lambda b, i, f: (0, 0)),
                  pl.BlockSpec((1, D_MODEL), lambda b, i, f: (0, 0))],
        out_specs=pl.BlockSpec((1, tm, D_MODEL), lambda b, i, f: (b, i, 0)),
        out_shape=jax.ShapeDtypeStruct((B, L, D_MODEL), F32),
        scratch_shapes=[pltpu.VMEM((tm, D_MODEL), BF16)],
        compiler_params=_params(("parallel", "parallel", "arbitrary")),
        name="ffn_ln2",
    )(x, mod3, mod3, mod3, w_in, w_in, w_out, lnw, lnb)


def _rope_body(cp_ref, sp_ref, cs_ref, ss_ref):
    half = RET_DK // 2
    for c_ref, s_ref, base, step in ((cp_ref, sp_ref, 0, 1), (cs_ref, ss_ref, PAST_LEN, 0)):
        shape = c_ref.shape
        lane = lax.broadcasted_iota(jnp.int32, shape, 1)
        row = lax.broadcasted_iota(jnp.int32, shape, 0)
        freq = jnp.exp(jnp.where(lane < half, lane, lane - half).astype(F32) * (-math.log(ROPE_BASE) / half))
        ang = (row * step + base).astype(F32) * freq
        c_ref[...] = jnp.cos(ang)
        s = jnp.sin(ang)
        s_ref[...] = jnp.where(lane < RET_DK // 2, -s, s)


def _rope_tables(seq):
    return pl.pallas_call(
        _rope_body,
        out_shape=[jax.ShapeDtypeStruct((seq, RET_DK), F32), jax.ShapeDtypeStruct((seq, RET_DK), F32),
                   jax.ShapeDtypeStruct((8, RET_DK), F32), jax.ShapeDtypeStruct((8, RET_DK), F32)],
        name="rope_tables",
    )()


def _rope(x, cos, sin_signed):
    return x * cos + pltpu.roll(x, RET_DK // 2, 1) * sin_signed


def _ret_prompt_body(q_ref, k_ref, v_ref, g_ref, cos_ref, sin_ref, gnw_ref, o_ref, s_ref, S):
    c = pl.program_id(1)
    C = RET_CHUNK

    @pl.when(c == 0)
    def _():
        S[...] = jnp.zeros_like(S)

    cos = cos_ref[...]
    sin = sin_ref[...]
    row = lax.broadcasted_iota(jnp.int32, (C, C), 0).astype(F32)
    col = lax.broadcasted_iota(jnp.int32, (C, C), 1).astype(F32)
    diff = row - col
    for h in range(RET_HEADS):
        lg = RET_LOG_G[h]
        sl = slice(h * RET_DK, (h + 1) * RET_DK)
        dmask = jnp.exp(jnp.where(diff >= 0, lg * diff, -jnp.inf))
        k_decay = jnp.exp(lg * (C - 1 - row))
        q_decay = jnp.exp(lg * (row + 1.0))
        for bt in range(q_ref.shape[0]):
            q = _rope(q_ref[bt, :, sl], cos, sin)
            k = _rope(k_ref[bt, :, sl], cos, sin) * (RET_DK ** -0.5)
            v = v_ref[bt, :, sl]
            scores = _bdot_nt(q, k) * dmask
            intra = _bdot(scores, v)
            kv = _bdot_tn(k * k_decay, v)
            s_prev = S[bt, h]
            cross = _bdot(q * q_decay, s_prev)
            S[bt, h] = s_prev * math.exp(lg * C) + kv
            ro = intra + cross
            mu = jnp.mean(ro, axis=-1, keepdims=True)
            rc = ro - mu
            var = jnp.mean(rc * rc, axis=-1, keepdims=True)
            rn = rc * lax.rsqrt(var + LN_EPS) * gnw_ref[:, sl] * _silu(g_ref[bt, :, sl])
            o_ref[bt, :, sl] = rn.astype(BF16)

    @pl.when(c == pl.num_programs(1) - 1)
    def _():
        s_ref[...] = S[...]


def _ret_prompt(proj, cos, sin, gnw, nbt):
    B, L, _ = proj.shape
    C = RET_CHUNK
    blk = lambda k: pl.BlockSpec((nbt, C, RET_W), lambda b, c: (b, c, k))
    return pl.pallas_call(
        _ret_prompt_body,
        grid=(B // nbt, L // C),
        in_specs=[blk(0), blk(1), blk(2), blk(3),
                  pl.BlockSpec((C, RET_DK), lambda b, c: (c, 0)),
                  pl.BlockSpec((C, RET_DK), lambda b, c: (c, 0)),
                  pl.BlockSpec((1, RET_W), lambda b, c: (0, 0))],
        out_specs=[pl.BlockSpec((nbt, C, RET_W), lambda b, c: (b, c, 0)),
                   pl.BlockSpec((nbt, RET_HEADS, RET_DK, RET_DK), lambda b, c: (b, 0, 0, 0))],
        out_shape=[jax.ShapeDtypeStruct((B, L, RET_W), BF16),
                   jax.ShapeDtypeStruct((B, RET_HEADS, RET_DK, RET_DK), F32)],
        scratch_shapes=[pltpu.VMEM((nbt, RET_HEADS, RET_DK, RET_DK), F32)],
        compiler_params=_params(("parallel", "arbitrary")),
        name="ret_prompt",
    )(proj, proj, proj, proj, cos, sin, gnw)


def _ret_sample_body(q_ref, k_ref, v_ref, g_ref, cos_ref, sin_ref, gnw_ref, dec_ref, s0_ref, o_ref, s_ref, obuf):
    nb = q_ref.shape[0]
    cos = cos_ref[0:1, :]
    sin = sin_ref[0:1, :]
    gdec = dec_ref[0]
    q = _rope(q_ref[...], cos, sin)
    k = _rope(k_ref[...], cos, sin) * (RET_DK ** -0.5)
    v = v_ref[...]
    qk = jnp.sum(q * k, axis=1, keepdims=True)
    qT = q.T
    kT = k.T
    for b in range(nb):
        s0 = s0_ref[0, b, 0]
        qcol = qT[:, b:b + 1]
        kcol = kT[:, b:b + 1]
        vrow = v[b:b + 1, :]
        cross = gdec * jnp.sum(s0 * qcol, axis=0, keepdims=True)
        s_ref[0, b, 0] = s0 * gdec + kcol * vrow
        obuf[b:b + 1, :] = qk[b:b + 1, :] * vrow + cross
    ro = obuf[...]
    mu = jnp.mean(ro, axis=-1, keepdims=True)
    rc = ro - mu
    var = jnp.mean(rc * rc, axis=-1, keepdims=True)
    o_ref[...] = (rc * lax.rsqrt(var + LN_EPS) * gnw_ref[...] * _silu(g_ref[...])).astype(BF16)


def _state_phases(body, n_in, state_pos, carried):
    def phased(*refs):
        p = pl.program_id(0)

        @pl.when(p == 0)
        def _():
            body(*refs)

        @pl.when(p == 1)
        def _():
            refs[state_pos][...] = jnp.zeros(refs[state_pos].shape, F32)

    if carried is None:
        return phased, 2, [], [], {}

    def carried_body(*refs):
        phased(*refs[:n_in], *refs[n_in + 1:])

    return carried_body, 1, [pl.BlockSpec(memory_space=pl.ANY)], [carried], {n_in: 1}


def _ret_sample(proj2, cos, sin, gnw, state, l, carried):
    nb = proj2.shape[0]
    H = RET_HEADS
    dec = jnp.asarray(np.broadcast_to(np.exp(np.array(RET_LOG_G))[:, None, None], (H, 1, RET_DK)), F32)
    hh = lambda p, h: h * (1 - p) + (H - 1) * p
    blk = lambda k: pl.BlockSpec((nb, RET_DK), lambda p, h: (0, k * H + hh(p, h)))
    in_specs = [blk(0), blk(1), blk(2), blk(3),
                pl.BlockSpec((8, RET_DK), lambda p, h: (0, 0)),
                pl.BlockSpec((8, RET_DK), lambda p, h: (0, 0)),
                pl.BlockSpec((1, RET_DK), lambda p, h: (0, hh(p, h))),
                pl.BlockSpec((1, 1, RET_DK), lambda p, h: (hh(p, h), 0, 0)),
                pl.BlockSpec((1, nb, 1, RET_DK, RET_DK), lambda p, h: (l, 0, hh(p, h), 0, 0))]
    body, phases, extra_specs, extra_args, aliases = _state_phases(_ret_sample_body, len(in_specs),
                                                                   len(in_specs) + 1, carried)
    return pl.pallas_call(
        body,
        grid=(phases, H),
        in_specs=in_specs + extra_specs,
        out_specs=[pl.BlockSpec((nb, RET_DK), lambda p, h: (0, hh(p, h))),
                   pl.BlockSpec((1, nb, 1, RET_DK, RET_DK), lambda p, h: (l + p, 0, h, 0, 0))],
        out_shape=[jax.ShapeDtypeStruct((nb, RET_W), BF16),
                   jax.ShapeDtypeStruct(state.shape, F32)],
        scratch_shapes=[pltpu.VMEM((nb, RET_DK), F32)],
        input_output_aliases=aliases,
        compiler_params=_params(("arbitrary", "arbitrary")),
        name="ret_sample",
    )(proj2, proj2, proj2, proj2, cos, sin, gnw, dec, state, *extra_args)


def _gdn_prompt_body(dq_ref, dk_ref, dv_ref, dz_ref, ab_ref, cw_ref, alog_ref, dtb_ref, nw_ref,
                     o_ref, s_ref, conv_ref, xbuf, S):
    c = pl.program_id(1)
    C = GDN_CHUNK
    H = GDN_HEADS
    dk = GDN_DK
    nbt = dq_ref.shape[0]

    @pl.when(c == 0)
    def _():
        S[...] = jnp.zeros_like(S)
        xbuf[:, 0:8, :] = jnp.zeros((nbt, 8, GDN_CONV_C), F32)

    row = lax.broadcasted_iota(jnp.int32, (C, dk), 0)
    lane = lax.broadcasted_iota(jnp.int32, (C, dk), 1)
    tri = lane <= row
    strict = lane < row
    r3 =lax.broadcasted_iota(jnp.int32, (C, 3 * C), 0)
    c3 = lax.broadcasted_iota(jnp.int32, (C, 3 * C), 1) % C
    tril_b = jnp.where(r3 >= c3, 1.0, 0.0).astype(BF16)

    def cumsum_rows(x):
        hi = x.astype(BF16)
        r1 = x - hi.astype(F32)
        mid = r1.astype(BF16)
        lo = (r1 - mid.astype(F32)).astype(BF16)
        return jnp.dot(tril_b, jnp.concatenate([hi, mid, lo], axis=0), preferred_element_type=F32)

    def lanes(x, j):
        return jnp.broadcast_to(x[:, j:j + 1], (x.shape[0], dk))

    cw = cw_ref[...]
    zrow_b = jnp.zeros((C, dk), BF16)

    items = []
    for bt in range(nbt):
        xbuf[bt, 8:8 + C, 0:GDN_W] = dq_ref[bt]
        xbuf[bt, 8:8 + C, GDN_W:2 * GDN_W] = dk_ref[bt]
        xbuf[bt, 8:8 + C, 2 * GDN_W:3 * GDN_W] = dv_ref[bt]

        def conv(off, bt=bt):
            y = xbuf[bt, 5:5 + C, off:off + dk] * cw[0:1, off:off + dk]
            for i in range(1, CONV_W):
                y = y + xbuf[bt, 5 + i:5 + i + C, off:off + dk] * cw[i:i + 1, off:off + dk]
            return _silu(y)

        ab = ab_ref[bt]
        g_c = -jnp.exp(alog_ref[...]) * jax.nn.softplus(ab + dtb_ref[...])
        gc_c = cumsum_rows(g_c)
        beta_c = jax.nn.sigmoid(ab)
        egc_c = jnp.exp(gc_c)
        gl_c = gc_c[C - 1:C, :]
        kdec_c = jnp.exp(gl_c - gc_c)
        egl_c = jnp.exp(gl_c)
        gc_t = jnp.concatenate([gc_c, jnp.zeros_like(gc_c)], axis=0).T

        for h in range(H):
            sl = slice(h * dk, (h + 1) * dk)
            q = conv(h * dk)
            k = conv(GDN_W + h * dk)
            v = conv(2 * GDN_W + h * dk)
            q = q * lax.rsqrt(jnp.sum(q * q, axis=-1, keepdims=True) + L2_EPS) * (dk ** -0.5)
            k = k * lax.rsqrt(jnp.sum(k * k, axis=-1, keepdims=True) + L2_EPS)
            beta = lanes(beta_c, H + h)
            egc = lanes(egc_c, h)
            kb = k * beta
            kq = jnp.concatenate([kb, q], axis=0).astype(BF16)
            k2 = jnp.concatenate([k.astype(BF16), zrow_b], axis=0)
            kkqk = lax.dot_general(kq, k2, NT, preferred_element_type=F32)
            dgc = lanes(gc_c, h) - jnp.broadcast_to(gc_t[h:h + 1, :], (C, dk))
            decay = jnp.exp(jnp.where(tri, dgc, -jnp.inf))
            a = jnp.where(strict, kkqk[0:C] * decay, 0.0)
            attn = (kkqk[C:2 * C] * decay)[:, 0:C].astype(BF16)
            rhs = jnp.concatenate([v * beta, kb * egc], axis=1).astype(BF16)
            items.append(dict(bt=bt, h=h, a=a, rhs=rhs, attn=attn, qd=(q * egc).astype(BF16),
                              kd=(k * lanes(kdec_c, h)).astype(BF16), egl=lanes(egl_c, h)))

    pair_col = lane % C
    eye_pair = (pair_col == row).astype(F32)
    left = lane < C

    def level_mask(s):
        br = row // s
        return (pair_col // s) == jnp.where(br % 2 == 1, br - 1, -1)

    def block_diag(x):
        zero = jnp.zeros_like(x)
        return jnp.concatenate([jnp.where(left, x, zero), jnp.where(left, zero, x)], axis=0)

    pairs = []
    for i in range(0, len(items), 2):
        a_pair = items[i]["a"] + pltpu.roll(items[i + 1]["a"], C, 1)
        pairs.append(dict(a=a_pair, d=eye_pair - jnp.where(level_mask(1), a_pair, 0.0)))
    s_blk = 2
    while s_blk < C:
        mask = level_mask(s_blk)
        for p in pairs:
            d = p["d"]
            sub = block_diag(jnp.where(mask, p["a"], 0.0)).astype(BF16)
            g = jnp.dot(d.astype(BF16), sub, preferred_element_type=F32)
            p["d"] = d - jnp.dot(g.astype(BF16), block_diag(d).astype(BF16), preferred_element_type=F32)
        s_blk *= 2
    for i, p in enumerate(pairs):
        r0, r1 = items[2 * i]["rhs"], items[2 * i + 1]["rhs"]
        zero = jnp.zeros_like(r0)
        rr = jnp.concatenate([jnp.concatenate([r0, zero], axis=1), jnp.concatenate([zero, r1], axis=1)], axis=0)
        sol = jnp.dot(p["d"].astype(BF16), rr, preferred_element_type=F32)
        items[2 * i]["sol"] = sol[:, 0:2 * dk]
        items[2 * i + 1]["sol"] = sol[:, 2 * dk:4 * dk]

    for it in items:
        bt, h = it["bt"], it["h"]
        sl = slice(h * dk, (h + 1) * dk)
        u = it["sol"][:, 0:dk]
        w = it["sol"][:, dk:2 * dk]
        s_prev = S[bt, h]
        wq = jnp.concatenate([w.astype(BF16), it["qd"]], axis=0)
        r = jnp.dot(wq, s_prev.astype(BF16), preferred_element_type=F32)
        v_new = u - r[0:C]
        o = r[C:2 * C] + jnp.dot(it["attn"], v_new.astype(BF16), preferred_element_type=F32)
        S[bt, h] = s_prev * it["egl"] + lax.dot_general(it["kd"], v_new.astype(BF16), TN, preferred_element_type=F32)
        o = o * lax.rsqrt(jnp.mean(o * o, axis=-1, keepdims=True) + RMS_EPS) * nw_ref[...]
        o_ref[bt, :, sl] = (o * _silu(dz_ref[bt, :, sl])).astype(BF16)

    for bt in range(nbt):
        xbuf[bt, 0:8, :] = xbuf[bt, C:C + 8, :]

    @pl.when(c == pl.num_programs(1) - 1)
    def _():
        s_ref[...] = S[...]
        conv_ref[...] = xbuf[:, C + 8 - (CONV_W - 1):C + 8, :]


def _gdn_prompt(proj, cw, alog_bc, dtb_bc, nw, nbt):
    B, L, _ = proj.shape
    C = GDN_CHUNK
    H = GDN_HEADS
    blk = lambda k: pl.BlockSpec((nbt, C, GDN_W), lambda b, c: (b, c, k))
    full = lambda shape: pl.BlockSpec(shape, lambda b, c: tuple(0 for _ in shape))
    return pl.pallas_call(
        _gdn_prompt_body,
        grid=(B // nbt, L // C),
        in_specs=[blk(2), blk(3), blk(4), blk(5),
                  pl.BlockSpec((nbt, C, 128), lambda b, c: (b, c, COL_AB // 128)),
                  full((CONV_W, GDN_CONV_C)), full((1, 128)), full((1, 128)), full((1, GDN_DK))],
        out_specs=[pl.BlockSpec((nbt, C, GDN_W), lambda b, c: (b, c, 0)),
                   pl.BlockSpec((nbt, H, GDN_DK, GDN_DK), lambda b, c: (b, 0, 0, 0)),
                   pl.BlockSpec((nbt, CONV_W - 1, GDN_CONV_C), lambda b, c: (b, 0, 0))],
        out_shape=[jax.ShapeDtypeStruct((B, L, GDN_W), BF16),
                   jax.ShapeDtypeStruct((B, H, GDN_DK, GDN_DK), F32),
                   jax.ShapeDtypeStruct((B, CONV_W - 1, GDN_CONV_C), F32)],
        scratch_shapes=[pltpu.VMEM((nbt, C + 8, GDN_CONV_C), F32),
                        pltpu.VMEM((nbt, H, GDN_DK, GDN_DK), F32)],
        compiler_params=_params(("parallel", "arbitrary")),
        name="gdn_prompt",
    )(proj, proj, proj, proj, proj, cw, alog_bc, dtb_bc, nw)


def _gdn_sample_body(dq_ref, dk_ref, dv_ref, dz_ref, ab_ref, csq_ref, csk_ref, csv_ref, cwq_ref, cwk_ref, cwv_ref,
                     alog_ref, dtb_ref, nw_ref, s0_ref, o_ref, s_ref, nq_ref, nk_ref, nv_ref, obuf):
    h = pl.program_id(1)
    nb = dq_ref.shape[0]
    dk = GDN_DK

    def conv(x_ref, cs_ref, w_ref, n_ref):
        x = x_ref[...]
        y = cs_ref[0] * w_ref[0:1, :] + cs_ref[1] * w_ref[1:2, :] + cs_ref[2] * w_ref[2:3, :] + x * w_ref[3:4, :]
        n_ref[0] = cs_ref[1]
        n_ref[1] = cs_ref[2]
        n_ref[2] = x
        return _silu(y)

    q = conv(dq_ref, csq_ref, cwq_ref, nq_ref)
    k = conv(dk_ref, csk_ref, cwk_ref, nk_ref)
    v = conv(dv_ref, csv_ref, cwv_ref, nv_ref)
    q = q * lax.rsqrt(jnp.sum(q * q, axis=-1, keepdims=True) + L2_EPS) * (dk ** -0.5)
    k = k * lax.rsqrt(jnp.sum(k * k, axis=-1, keepdims=True) + L2_EPS)
    er = lax.broadcasted_iota(jnp.int32, (128, dk), 0)
    ab = ab_ref[...]
    a_bc = jnp.dot(ab, (er == h).astype(F32), precision=HI, preferred_element_type=F32)
    b_bc = jnp.dot(ab, (er == h + GDN_HEADS).astype(F32), precision=HI, preferred_element_type=F32)
    g = -jnp.exp(alog_ref[0]) * jax.nn.softplus(a_bc + dtb_ref[0])
    eg = jnp.exp(g)
    beta = jax.nn.sigmoid(b_bc)
    qk = jnp.sum(q * k, axis=1, keepdims=True)
    qT = q.T
    kT = k.T
    for b in range(nb):
        s0 = s0_ref[0, b, 0]
        qcol = qT[:, b:b + 1]
        kcol = kT[:, b:b + 1]
        ks = jnp.sum(s0 * kcol, axis=0, keepdims=True)
        qs = jnp.sum(s0 * qcol, axis=0, keepdims=True)
        bb = beta[b:b + 1, :]
        egb = eg[b:b + 1, :]
        v_new = bb * v[b:b + 1, :] - bb * egb * ks
        s_ref[0, b, 0] = s0 * egb + kcol * v_new
        obuf[b:b + 1, :] = egb * qs + qk[b:b + 1, :] * v_new
    o = obuf[...]
    o = o * lax.rsqrt(jnp.mean(o * o, axis=-1, keepdims=True) + RMS_EPS) * nw_ref[...]
    o_ref[...] = (o * _silu(dz_ref[...])).astype(BF16)


def _gdn_sample(proj2, cs_t, cw, alog_t, dtb_t, nw, state, l, carried):
    nb = proj2.shape[0]
    H = GDN_HEADS
    dk = GDN_DK
    hh = lambda p, h: h * (1 - p) + (H - 1) * p
    blk = lambda k: pl.BlockSpec((nb, dk), lambda p, h: (0, k * H + hh(p, h)))
    csblk = lambda k: pl.BlockSpec((CONV_W - 1, nb, dk), lambda p, h: (0, 0, k * H + hh(p, h)))
    cwblk = lambda k: pl.BlockSpec((CONV_W, dk), lambda p, h: (0, k * H + hh(p, h)))
    tab = pl.BlockSpec((1, 1, dk), lambda p, h: (hh(p, h), 0, 0))
    nblk = pl.BlockSpec((CONV_W - 1, nb, dk), lambda p, h: (0, 0, hh(p, h)))
    nshape = jax.ShapeDtypeStruct((CONV_W - 1, nb, GDN_W), F32)
    in_specs = [blk(2), blk(3), blk(4), blk(5),
                pl.BlockSpec((nb, 128), lambda p, h: (0, COL_AB // 128)),
                csblk(0), csblk(1), csblk(2), cwblk(0), cwblk(1), cwblk(2),
                tab, tab, pl.BlockSpec((1, dk), lambda p, h: (0, 0)),
                pl.BlockSpec((1, nb, 1, dk, dk), lambda p, h: (l, 0, hh(p, h), 0, 0))]
    body, phases, extra_specs, extra_args, aliases = _state_phases(_gdn_sample_body, len(in_specs),
                                                                   len(in_specs) + 1, carried)
    return pl.pallas_call(
        body,
        grid=(phases, H),
        in_specs=in_specs + extra_specs,
        out_specs=[pl.BlockSpec((nb, dk), lambda p, h: (0, hh(p, h))),
                   pl.BlockSpec((1, nb, 1, dk, dk), lambda p, h: (l + p, 0, h, 0, 0)), nblk, nblk, nblk],
        out_shape=[jax.ShapeDtypeStruct((nb, GDN_W), BF16), jax.ShapeDtypeStruct(state.shape, F32),
                   nshape, nshape, nshape],
        scratch_shapes=[pltpu.VMEM((nb, dk), F32)],
        input_output_aliases=aliases,
        compiler_params=_params(("arbitrary", "arbitrary")),
        name="gdn_sample",
    )(proj2, proj2, proj2, proj2, proj2, cs_t, cs_t, cs_t, cw, cw, cw, alog_t, dtb_t, nw, state, *extra_args)


def _ssm_prep_body(are_ref, aim_ref, ldt_ref, bre_ref, bim_ref, lre_ref, lim_ref, obre_ref, obim_ref):
    ar = are_ref[0]
    ai = aim_ref[0]
    dt = jnp.exp(ldt_ref[0])
    er = jnp.exp(ar * dt)
    lr = er * jnp.cos(ai * dt)
    li = er * jnp.sin(ai * dt)
    lre_ref[0] = lr
    lim_ref[0] = li
    xr = lr - 1.0
    den = ar * ar + ai * ai
    cr = ((xr * ar + li * ai) / den)[:, None, :]
    ci = ((li * ar - xr * ai) / den)[:, None, :]
    br = bre_ref[0]
    bi = bim_ref[0]
    obre_ref[0] = cr * br - ci * bi
    obim_ref[0] = cr * bi + ci * br


def _ssm_prep(a_re, a_im, log_dt, b_re, b_im):
    G, N, P = SSM_GROUPS, SSM_N, SSM_GROUP
    ldt = jnp.broadcast_to(log_dt[:, :, None], (DEPTH, G, N))
    bt_re = jnp.transpose(b_re, (0, 1, 3, 2))
    bt_im = jnp.transpose(b_im, (0, 1, 3, 2))
    s2 = pl.BlockSpec((1, G, N), lambda l: (l, 0, 0))
    s3 = pl.BlockSpec((1, G, P, N), lambda l: (l, 0, 0, 0))
    return pl.pallas_call(
        _ssm_prep_body,
        grid=(DEPTH,),
        in_specs=[s2, s2, s2, s3, s3],
        out_specs=[s2, s2, s3, s3],
        out_shape=[jax.ShapeDtypeStruct((DEPTH, G, N), F32), jax.ShapeDtypeStruct((DEPTH, G, N), F32),
                   jax.ShapeDtypeStruct((DEPTH, G, P, N), F32), jax.ShapeDtypeStruct((DEPTH, G, P, N), F32)],
        name="ssm_prep",
    )(a_re, a_im, ldt, bt_re, bt_im)


def _ssm_tile_blockdiag(re, im):
    x = jnp.stack([re, im]).reshape(2, SSM_TILES, SSM_TILE_G, SSM_GROUP, SSM_N)
    eye = jnp.eye(SSM_TILE_G, dtype=x.dtype)
    y = x[:, :, :, :, None, :] * eye[None, None, :, None, :, None]
    y = jnp.transpose(y, (1, 2, 3, 0, 4, 5))
    return y.reshape(SSM_TILES, SSM_TILE_G * SSM_GROUP, SSM_TILE_W)


def _ssm_state_to_tiles(re, im):
    B = re.shape[0]
    x = jnp.stack([re, im], axis=1).reshape(B, 2, SSM_TILES, SSM_HALF)
    return jnp.transpose(x, (0, 2, 1, 3)).reshape(B, SSM_TILES * SSM_TILE_W)


def _ssm_tiles_to_state(x):
    B = x.shape[0]
    x = jnp.transpose(x.reshape(B, SSM_TILES, 2, SSM_HALF), (0, 2, 1, 3)).reshape(B, 2, SSM_GROUPS, SSM_N)
    return x[:, 0], x[:, 1]


def _cmul(ar, ai, br, bi):
    return ar * br - ai * bi, ar * bi + ai * br


def _s5_prompt_body(u_ref, bt_ref, ct_ref, lam_ref, d_ref, y_ref, hl_ref, buf):
    L = u_ref.shape[1]
    nseg = L // SSM_SEG
    ns = SSM_HALF // 128
    bmat = bt_ref[0].astype(BF16)
    for i in range(nseg):
        bu = jnp.dot(u_ref[0, i * SSM_SEG:(i + 1) * SSM_SEG, :].astype(BF16), bmat, preferred_element_type=F32)
        for s in range(2 * ns):
            buf[s, i * SSM_PITCH:i * SSM_PITCH + SSM_SEG, :] = bu[:, s * 128:(s + 1) * 128]

    lam = lam_ref[0]
    a_re = [jnp.broadcast_to(lam[:, s * 128:(s + 1) * 128], (nseg, 128)) for s in range(ns)]
    a_im = [jnp.broadcast_to(lam[:, SSM_HALF + s * 128:SSM_HALF + (s + 1) * 128], (nseg, 128)) for s in range(ns)]

    def scan_step(t, carry):
        out = []
        for s in range(ns):
            hr, hi = carry[2 * s], carry[2 * s + 1]
            pr, pi = _cmul(a_re[s], a_im[s], hr, hi)
            hr = pr + buf[s, pl.ds(t, nseg, stride=SSM_PITCH), :]
            hi = pi + buf[ns + s, pl.ds(t, nseg, stride=SSM_PITCH), :]
            buf[s, pl.ds(t, nseg, stride=SSM_PITCH), :] = hr
            buf[ns + s, pl.ds(t, nseg, stride=SSM_PITCH), :] = hi
            out += [hr, hi]
        return tuple(out)

    zero = jnp.zeros((nseg, 128), F32)
    ends = lax.fori_loop(0, SSM_SEG, scan_step, tuple(zero for _ in range(2 * ns)), unroll=SSM_UNROLL)

    sub = lax.broadcasted_iota(jnp.int32, (nseg, 128), 0)
    carries = []
    for s in range(ns):
        pr, pi = a_re[s], a_im[s]
        for _ in range(int(math.log2(SSM_SEG))):
            pr, pi = _cmul(pr, pi, pr, pi)
        xr, xi = ends[2 * s], ends[2 * s + 1]
        shift = 1
        while shift < nseg:
            sr = jnp.where(sub >= shift, pltpu.roll(xr, shift, 0), 0.0)
            si = jnp.where(sub >= shift, pltpu.roll(xi, shift, 0), 0.0)
            mr, mi = _cmul(pr, pi, sr, si)
            xr, xi = xr + mr, xi + mi
            pr, pi = _cmul(pr, pi, pr, pi)
            shift *= 2
        hl_ref[0, 0, :, s * 128:(s + 1) * 128] = xr[nseg - 1:nseg, :]
        hl_ref[0, 0, :, SSM_HALF + s * 128:SSM_HALF + (s + 1) * 128] = xi[nseg - 1:nseg, :]
        carries += [jnp.where(sub >= 1, pltpu.roll(xr, 1, 0), 0.0), jnp.where(sub >= 1, pltpu.roll(xi, 1, 0), 0.0)]

    def fix_step(t, pw):
        out = []
        for s in range(ns):
            pr, pi = pw[2 * s], pw[2 * s + 1]
            fr, fi = _cmul(pr, pi, carries[2 * s], carries[2 * s + 1])
            buf[s, pl.ds(t, nseg, stride=SSM_PITCH), :] += fr
            buf[ns + s, pl.ds(t, nseg, stride=SSM_PITCH), :] += fi
            nr, ni = _cmul(pr, pi, a_re[s], a_im[s])
            out += [nr, ni]
        return tuple(out)

    pw0 = []
    for s in range(ns):
        pw0 += [a_re[s], a_im[s]]
    lax.fori_loop(0, SSM_SEG, fix_step, tuple(pw0), unroll=SSM_UNROLL)

    cmat = ct_ref[0]
    cmat = jnp.concatenate([cmat[:, 0:SSM_HALF], -cmat[:, SSM_HALF:]], axis=1).astype(BF16)
    for i in range(nseg):
        rows = slice(i * SSM_PITCH, i * SSM_PITCH + SSM_SEG)
        h = jnp.concatenate([buf[s, rows, :].astype(BF16) for s in range(2 * ns)], axis=1)
        y = lax.dot_general(h, cmat, NT, preferred_element_type=F32)
        u = u_ref[0, i * SSM_SEG:(i + 1) * SSM_SEG, :]
        y_ref[0, i * SSM_SEG:(i + 1) * SSM_SEG, :] = jax.nn.gelu(y + d_ref[...] * u)


def _s5_prompt(proj, bt, ct, lam, d):
    B, L, _ = proj.shape
    nseg = L // SSM_SEG
    tile = lambda shape: pl.BlockSpec(shape, lambda b, j: (j,) + tuple(0 for _ in shape[1:]))
    return pl.pallas_call(
        _s5_prompt_body,
        grid=(B, SSM_TILES),
        in_specs=[pl.BlockSpec((1, L, 128), lambda b, j: (b, 0, COL_SU // 128 + j)),
                  tile((1, 128, SSM_TILE_W)), tile((1, 128, SSM_TILE_W)), tile((1, 1, SSM_TILE_W)),
                  pl.BlockSpec((1, 128), lambda b, j: (0, j))],
        out_specs=[pl.BlockSpec((1, L, 128), lambda b, j: (b, 0, j)),
                   pl.BlockSpec((1, 1, 1, SSM_TILE_W), lambda b, j: (b, j, 0, 0))],
        out_shape=[jax.ShapeDtypeStruct((B, L, SSM_W), F32),
                   jax.ShapeDtypeStruct((B, SSM_TILES, 1, SSM_TILE_W), F32)],
        scratch_shapes=[pltpu.VMEM((SSM_TILE_W // 128, nseg * SSM_PITCH, 128), F32)],
        compiler_params=_params(("parallel", "parallel")),
        name="s5_prompt",
    )(proj, bt, ct, lam, d)


def _s5_sample_body(u_ref, bt_ref, ct_ref, lam_ref, d_ref, h0_ref, y_ref, h_ref):
    u = u_ref[...]
    bu = jnp.dot(u, bt_ref[0], precision=HI, preferred_element_type=F32)
    lam = lam_ref[0]
    lr, li = lam[:, 0:SSM_HALF], lam[:, SSM_HALF:]
    h0r, h0i = h0_ref[:, 0:SSM_HALF], h0_ref[:, SSM_HALF:]
    pr, pi = _cmul(lr, li, h0r, h0i)
    hr = pr + bu[:, 0:SSM_HALF]
    hi = pi + bu[:, SSM_HALF:]
    h_ref[:, 0:SSM_HALF] = hr
    h_ref[:, SSM_HALF:] = hi
    ct = ct_ref[0]
    y = lax.dot_general(hr, ct[:, 0:SSM_HALF], NT, precision=HI, preferred_element_type=F32)
    y -= lax.dot_general(hi, ct[:, SSM_HALF:], NT, precision=HI, preferred_element_type=F32)
    y_ref[...] = jax.nn.gelu(y + d_ref[...] * u)


def _s5_sample(proj2, bt, ct, lam, d, h0):
    nb = proj2.shape[0]
    tile = lambda shape: pl.BlockSpec(shape, lambda j: (j,) + tuple(0 for _ in shape[1:]))
    hblk = pl.BlockSpec((nb, SSM_TILE_W), lambda j: (0, j))
    return pl.pallas_call(
        _s5_sample_body,
        grid=(SSM_TILES,),
        in_specs=[pl.BlockSpec((nb, 128), lambda j: (0, COL_SU // 128 + j)),
                  tile((1, 128, SSM_TILE_W)), tile((1, 128, SSM_TILE_W)), tile((1, 1, SSM_TILE_W)),
                  pl.BlockSpec((1, 128), lambda j: (0, j)), hblk],
        out_specs=[pl.BlockSpec((nb, 128), lambda j: (0, j)), hblk],
        out_shape=[jax.ShapeDtypeStruct((nb, SSM_W), F32), jax.ShapeDtypeStruct(h0.shape, F32)],
        compiler_params=_params(("parallel",)),
        name="s5_sample",
    )(proj2, bt, ct, lam, d, h0)


def kernel(x_prompt, x_sample, c_prompt, c_sample, state_ret, state_gdn, state_conv, state_ssm_re, state_ssm_im, w_mod, b_mod, w_in, conv_w, ret_gn_w, gdn_a_log, gdn_dt_bias, gdn_norm_w, ssm_a_re, ssm_a_im, ssm_log_dt, ssm_b_re, ssm_b_im, ssm_c_re, ssm_c_im, ssm_d, ssm_w_glu, w_out, ln1_w, ln1_b, w_ffn_in, w_ffn_out, ln2_w, ln2_b):
    Bp, Lp, _ = x_prompt.shape
    Bs = x_sample.shape[0]
    assert x_sample.shape[1] == 1 and Bs % 8 == 0

    pad_rows = (-(Bp + Bs)) % 8
    c_all = jnp.concatenate([c_prompt, c_sample, jnp.zeros((pad_rows, D_MODEL), F32)], axis=0)
    mod_all = _modulation(c_all, w_mod, b_mod)
    lam_re, lam_im, bbar_re, bbar_im = _ssm_prep(ssm_a_re, ssm_a_im, ssm_log_dt, ssm_b_re, ssm_b_im)
    cos_p, sin_p, cos_s, sin_s = _rope_tables(Lp)

    w_in_b = w_in.astype(BF16)
    w_out_b = _to_bf16(w_out, rows=512)
    w_glu_b = ssm_w_glu.astype(BF16)
    w_ffi_b = _to_bf16(w_ffn_in, rows=128)
    w_ffo_b = _to_bf16(w_ffn_out, rows=512)

    yp = x_prompt
    ys = x_sample.reshape(1, Bs, D_MODEL)
    outs_p, outs_s = [], []
    ret_s = gdn_s = None
    for l in range(DEPTH):
        w_tail = jnp.concatenate(
            [w_in_b[l, :, COL_SU + 2 * GDN_HEADS:], w_in_b[l, :, COL_SU:COL_SU + 2 * GDN_HEADS],
             jnp.zeros((D_MODEL, N_PROJ - N_IN), BF16)], axis=1)
        mod_p = mod_all[l, :Bp][:, None, :]
        mod_s = mod_all[l, Bp:Bp + Bs][None]
        gnw = ret_gn_w[l][None, :]
        nw = gdn_norm_w[l][None, :]
        alog_bc = jnp.pad(gdn_a_log[l], (0, 128 - GDN_HEADS))[None, :]
        dtb_bc = jnp.pad(gdn_dt_bias[l], (0, 128 - GDN_HEADS))[None, :]
        alog_t = jnp.broadcast_to(gdn_a_log[l][:, None, None], (GDN_HEADS, 1, GDN_DK))
        dtb_t = jnp.broadcast_to(gdn_dt_bias[l][:, None, None], (GDN_HEADS, 1, GDN_DK))
        ssm_bt = _ssm_tile_blockdiag(bbar_re[l], bbar_im[l])
        ssm_ct = _ssm_tile_blockdiag(ssm_c_re[l], ssm_c_im[l])
        ssm_lam = jnp.concatenate([lam_re[l].reshape(SSM_TILES, 1, SSM_HALF),
                                   lam_im[l].reshape(SSM_TILES, 1, SSM_HALF)], axis=-1)
        ssm_dl = ssm_d[l][None, :]
        lnw1, lnb1 = ln1_w[l][None, :], ln1_b[l][None, :]
        lnw2, lnb2 = ln2_w[l][None, :], ln2_b[l][None, :]

        proj = _inproj(yp, mod_p, w_in_b, l, w_tail, tm=1024, tn=PROJ_TN)
        ro, ret_p = _ret_prompt(proj, cos_p, sin_p, gnw, nbt=4)
        go, gdn_p, conv_p = _gdn_prompt(proj, conv_w[l], alog_bc, dtb_bc, nw, nbt=4)
        so, hl = _s5_prompt(proj, ssm_bt, ssm_ct, ssm_lam, ssm_dl)
        re_p, im_p = _ssm_tiles_to_state(hl.reshape(Bp, SSM_TILES * SSM_TILE_W))
        x1 = _outproj(ro, go, so, yp, mod_p, w_glu_b, w_out_b, l, lnw1, lnb1, tm=512)
        yp = _ffn(x1, mod_p, w_ffi_b, w_ffo_b, l, lnw2, lnb2, tm=512, tf=512)
        outs_p.append((ret_p, gdn_p, conv_p, re_p, im_p))

        proj = _inproj(ys, mod_s, w_in_b, l, w_tail, tm=Bs, tn=PROJ_TN)
        proj2 = proj.reshape(Bs, N_PROJ)
        ro, ret_s = _ret_sample(proj2, cos_s, sin_s, gnw, state_ret, l, ret_s)
        cs_t = jnp.transpose(state_conv[l], (1, 0, 2))
        go, gdn_s, nq, nk, nv = _gdn_sample(proj2, cs_t, conv_w[l], alog_t, dtb_t, nw, state_gdn, l, gdn_s)
        conv_s = jnp.transpose(jnp.concatenate([nq, nk, nv], axis=-1), (1, 0, 2))
        h0 = _ssm_state_to_tiles(state_ssm_re[l], state_ssm_im[l])
        so, hn = _s5_sample(proj2, ssm_bt, ssm_ct, ssm_lam, ssm_dl, h0)
        re_s, im_s = _ssm_tiles_to_state(hn)
        x1 = _outproj(ro[None], go[None], so[None], ys, mod_s, w_glu_b, w_out_b, l, lnw1, lnb1, tm=Bs)
        ys = _ffn(x1, mod_s, w_ffi_b, w_ffo_b, l, lnw2, lnb2, tm=Bs, tf=512)
        outs_s.append((conv_s, re_s, im_s))

    ret_p, gdn_p, conv_p, re_p, im_p = [jnp.stack(t) for t in zip(*outs_p)]
    conv_s, re_s, im_s = [jnp.stack(t) for t in zip(*outs_s)]
    return (yp, ys.reshape(Bs, 1, D_MODEL), ret_p, gdn_p, conv_p, re_p, im_p,
            ret_s, gdn_s, conv_s, re_s, im_s)
```

```python
import functools
import math

import numpy as np
import jax
import jax.numpy as jnp
from jax import lax
from jax.experimental import pallas as pl
from jax.experimental.pallas import tpu as pltpu

F32 = jnp.float32
BF16 = jnp.bfloat16

D_MODEL = 2048
DEPTH = 2
PAST_LEN = 16384
RET_HEADS = 4
RET_DK = 128
RET_W = 512
RET_CHUNK = 128
ROPE_BASE = 10000.0
GDN_HEADS = 8
GDN_DK = 128
GDN_W = 1024
GDN_CHUNK = 64
CONV_W = 4
GDN_CONV_C = 3072
SSM_W = 512
SSM_GROUP = 16
SSM_GROUPS = 32
SSM_N = 64
D_FF = 5632
ALPHA = (2 * DEPTH) ** 0.25
LN_EPS = 1e-5
RMS_EPS = 1e-6
L2_EPS = 1e-6

COL_SU = 6144
COL_AB = 6656
N_IN = 6672
PROJ_TN = 768
N_PROJ = COL_SU + PROJ_TN

SSM_TILES = 4
SSM_TILE_G = 8
SSM_HALF = SSM_TILE_G * SSM_N
SSM_TILE_W = 2 * SSM_HALF
SSM_SEG = 256
SSM_PITCH = 260
SSM_UNROLL = 4

VMEM_LIMIT = 56 * 1024 * 1024
OUTPROJ_ROW_CHUNK = 256
FFN_ROW_CHUNK = 256
FFN_TF = 512

RET_LOG_G = [math.log1p(-(2.0 ** (-5.0 - h))) for h in range(RET_HEADS)]

NT = (((1,), (1,)), ((), ()))
TN = (((0,), (0,)), ((), ()))
HI = lax.Precision.HIGHEST


def _params(sem):
    return pltpu.CompilerParams(dimension_semantics=sem, vmem_limit_bytes=VMEM_LIMIT)


def _silu(x):
    return x * jax.nn.sigmoid(x)


def _bdot(a, b):
    return jnp.dot(a.astype(BF16), b.astype(BF16), preferred_element_type=F32)


def _hdot(a, b):
    return jnp.dot(a, b, precision=HI, preferred_element_type=F32)


def _bdot_nt(a, b):
    return lax.dot_general(a.astype(BF16), b.astype(BF16), NT, preferred_element_type=F32)


def _bdot_tn(a, b):
    return lax.dot_general(a.astype(BF16), b.astype(BF16), TN, preferred_element_type=F32)


def _layernorm(z, w, b):
    mu = jnp.mean(z, axis=-1, keepdims=True)
    zc = z - mu
    var = jnp.mean(zc * zc, axis=-1, keepdims=True)
    return zc * lax.rsqrt(var + LN_EPS) * w + b


def _cast_body(x_ref, o_ref):
    o_ref[...] = x_ref[...].astype(o_ref.dtype)


def _to_bf16(w, rows):
    depth, r, n = w.shape
    spec = pl.BlockSpec((1, rows, n), lambda l, i: (l, i, 0))
    return pl.pallas_call(
        _cast_body,
        grid=(depth, r // rows),
        in_specs=[spec],
        out_specs=spec,
        out_shape=jax.ShapeDtypeStruct(w.shape, BF16),
        compiler_params=_params(("parallel", "parallel")),
        name="to_bf16",
    )(w)


def _cast_col_tiles_body(x_ref, o_ref):
    tc = o_ref.shape[3]
    for t in range(o_ref.shape[1]):
        o_ref[0, t] = x_ref[0, :, t * tc:(t + 1) * tc].astype(o_ref.dtype)


def _to_bf16_col_tiles(w, rows, tc):
    depth, r, n = w.shape
    nt = n // tc
    return pl.pallas_call(
        _cast_col_tiles_body,
        grid=(depth, r // rows),
        in_specs=[pl.BlockSpec((1, rows, n), lambda l, i: (l, i, 0))],
        out_specs=pl.BlockSpec((1, nt, rows, tc), lambda l, i: (l, 0, i, 0)),
        out_shape=jax.ShapeDtypeStruct((depth, nt, r, tc), BF16),
        compiler_params=_params(("parallel", "parallel")),
        name="to_bf16_col_tiles",
    )(w)


def _mod_body(c_ref, w_ref, b_ref, o_ref):
    c = c_ref[...]
    o_ref[0] = _bdot(_silu(c), w_ref[0]) + b_ref[0]


def _modulation(c_all, w_mod, b_mod):
    rows = c_all.shape[0]
    tn = 1024
    n = 6 * D_MODEL
    return pl.pallas_call(
        _mod_body,
        grid=(DEPTH, n // tn),
        in_specs=[pl.BlockSpec((rows, D_MODEL), lambda l, j: (0, 0)),
                  pl.BlockSpec((1, D_MODEL, tn), lambda l, j: (l, 0, j)),
                  pl.BlockSpec((1, 1, tn), lambda l, j: (l, 0, j))],
        out_specs=pl.BlockSpec((1, rows, tn), lambda l, j: (l, 0, j)),
        out_shape=jax.ShapeDtypeStruct((DEPTH, rows, n), F32),
        compiler_params=_params(("parallel", "parallel")),
        name="modulation",
    )(c_all, w_mod, b_mod.reshape(DEPTH, 1, n))


def _mod_spec(mod3, tm, k, grid_rank):
    per_token = mod3.shape[1] != 1
    lmb = tm if per_token else 1
    if grid_rank == 3:
        return pl.BlockSpec((1, lmb, D_MODEL), lambda b, i, j: (b, i if per_token else 0, k))
    return pl.BlockSpec((1, lmb, D_MODEL), lambda b, i: (b, i if per_token else 0, k))


def _inproj_body(x_ref, sc_ref, sh_ref, w_ref, wt_ref, o_ref, h_ref, *, n_main):
    j = pl.program_id(2)
    tm = x_ref.shape[1]
    per_token = sc_ref.shape[1] != 1

    @pl.when(j == 0)
    def _():
        step = min(tm, FFN_ROW_CHUNK)
        for r in range(0, tm, step):
            rows = slice(r, r + step)
            sc = sc_ref[0, rows, :] if per_token else sc_ref[0]
            sh = sh_ref[0, rows, :] if per_token else sh_ref[0]
            h = (x_ref[0, rows, :] * (1.0 + sc) + sh).astype(BF16)
            h_ref[rows, :] = h
            o_ref[0, rows, :] = jnp.dot(h, w_ref[0, 0], preferred_element_type=F32)

    @pl.when(jnp.logical_and(j > 0, j < n_main))
    def _():
        o_ref[0] = jnp.dot(h_ref[...], w_ref[0, 0], preferred_element_type=F32)

    @pl.when(j == n_main)
    def _():
        o_ref[0] = jnp.dot(h_ref[...], wt_ref[...], preferred_element_type=F32)


def _inproj(x, mod3, w_in, l, w_tail, tm, tn):
    B, L, _ = x.shape
    n_main = COL_SU // tn
    assert n_main * tn == COL_SU and w_tail.shape == (D_MODEL, tn) and (n_main + 1) * tn == N_PROJ
    return pl.pallas_call(
        functools.partial(_inproj_body, n_main=n_main),
        grid=(B, L // tm, n_main + 1),
        in_specs=[pl.BlockSpec((1, tm, D_MODEL), lambda b, i, j: (b, i, 0)),
                  _mod_spec(mod3, tm, 1, 3),
                  _mod_spec(mod3, tm, 0, 3),
                  pl.BlockSpec((1, 1, D_MODEL, tn), lambda b, i, j: (l, jnp.minimum(j, n_main - 1), 0, 0)),
                  pl.BlockSpec((D_MODEL, tn), lambda b, i, j: (0, 0))],
        out_specs=pl.BlockSpec((1, tm, tn), lambda b, i, j: (b, i, j)),
        out_shape=jax.ShapeDtypeStruct((B, L, N_PROJ), F32),
        scratch_shapes=[pltpu.VMEM((tm, D_MODEL), BF16)],
        compiler_params=_params(("parallel", "parallel", "arbitrary")),
        name="inproj",
    )(x, mod3, mod3, w_in, w_tail)


def _outproj_body(ro_ref, go_ref, y_ref, x_ref, g1_ref, wglu_ref, wout_ref, lnw_ref, lnb_ref, o_ref, *, row_chunk):
    tm = x_ref.shape[1]
    for r in range(0, tm, row_chunk):
        rows = slice(r, r + row_chunk)
        y = y_ref[0, rows, :]
        so = y * jax.nn.sigmoid(_bdot(y, wglu_ref[0]))
        acc = jnp.dot(ro_ref[0, rows, :], wout_ref[0, 0:RET_W, :], preferred_element_type=F32)
        acc += jnp.dot(go_ref[0, rows, :], wout_ref[0, RET_W:RET_W + GDN_W, :], preferred_element_type=F32)
        acc += jnp.dot(so.astype(BF16), wout_ref[0, RET_W + GDN_W:, :], preferred_element_type=F32)
        g1 = g1_ref[0] if g1_ref.shape[1] == 1 else g1_ref[0, rows, :]
        z = ALPHA * x_ref[0, rows, :] + g1 * acc
        o_ref[0, rows, :] = _layernorm(z, lnw_ref[...], lnb_ref[...])


def _outproj(ro, go, y, x, mod3, wglu, wout, l, lnw, lnb, tm):
    B, L, _ = x.shape
    return pl.pallas_call(
        functools.partial(_outproj_body, row_chunk=min(tm, OUTPROJ_ROW_CHUNK)),
        grid=(B, L // tm),
        in_specs=[pl.BlockSpec((1, tm, RET_W), lambda b, i: (b, i, 0)),
                  pl.BlockSpec((1, tm, GDN_W), lambda b, i: (b, i, 0)),
                  pl.BlockSpec((1, tm, SSM_W), lambda b, i: (b, i, 0)),
                  pl.BlockSpec((1, tm, D_MODEL), lambda b, i: (b, i, 0)),
                  _mod_spec(mod3, tm, 2, 2),
                  pl.BlockSpec((1, SSM_W, SSM_W), lambda b, i: (l, 0, 0)),
                  pl.BlockSpec((1, D_MODEL, D_MODEL), lambda b, i: (l, 0, 0)),
                  pl.BlockSpec((1, D_MODEL), lambda b, i: (0, 0)),
                  pl.BlockSpec((1, D_MODEL), lambda b, i: (0, 0))],
        out_specs=pl.BlockSpec((1, tm, D_MODEL), lambda b, i: (b, i, 0)),
        out_shape=jax.ShapeDtypeStruct((B, L, D_MODEL), F32),
        compiler_params=_params(("parallel", "parallel")),
        name="outproj_ln1",
    )(ro, go, y, x, mod3, wglu, wout, lnw, lnb)


def _ffn_body(x_ref, sc_ref, sh_ref, g2_ref, wg_ref, wu_ref, wo_ref, lnw_ref, lnb_ref, o_ref, h_ref):
    f = pl.program_id(2)
    last = pl.num_programs(2) - 1
    tm = x_ref.shape[1]
    chunks = [slice(r, r + min(tm, FFN_ROW_CHUNK)) for r in range(0, tm, min(tm, FFN_ROW_CHUNK))]
    per_token = sc_ref.shape[1] != 1

    def mod(ref, rows):
        return ref[0, rows, :] if per_token else ref[0]

    def contribution(h):
        gate = jnp.dot(h, wg_ref[0, 0], preferred_element_type=F32)
        up = jnp.dot(h, wu_ref[0, 0], preferred_element_type=F32)
        act = (_silu(gate) * up).astype(BF16)
        return jnp.dot(act, wo_ref[0], preferred_element_type=F32)

    @pl.when(f == 0)
    def _():
        for rows in chunks:
            h = (x_ref[0, rows, :] * (1.0 + mod(sc_ref, rows)) + mod(sh_ref, rows)).astype(BF16)
            h_ref[rows, :] = h
            o_ref[0, rows, :] = contribution(h)

    @pl.when(jnp.logical_and(f > 0, f < last))
    def _():
        o_ref[0] += contribution(h_ref[...])

    @pl.when(f == last)
    def _():
        for rows in chunks:
            acc = o_ref[0, rows, :] + contribution(h_ref[rows, :])
            z = ALPHA * x_ref[0, rows, :] + mod(g2_ref, rows) * acc
            o_ref[0, rows, :] = _layernorm(z, lnw_ref[...], lnb_ref[...])


def _ffn(x, mod3, w_in, w_out, l, lnw, lnb, tm, tf):
    B, L, _ = x.shape
    nf = D_FF // tf
    return pl.pallas_call(
        _ffn_body,
        grid=(B, L // tm, nf),
        in_specs=[pl.BlockSpec((1, tm, D_MODEL), lambda b, i, f: (b, i, 0)),
                  _mod_spec(mod3, tm, 4, 3),
                  _mod_spec(mod3, tm, 3, 3),
                  _mod_spec(mod3, tm, 5, 3),
                  pl.BlockSpec((1, 1, D_MODEL, tf), lambda b, i, f: (l, f, 0, 0)),
                  pl.BlockSpec((1, 1, D_MODEL, tf), lambda b, i, f: (l, f + nf, 0, 0)),
                  pl.BlockSpec((1, tf, D_MODEL), lambda b, i, f: (l, f, 0)),
                  pl.BlockSpec((1, D_MODEL), lambda b, i, f: (0, 0)),
                  pl.BlockSpec((1, D_MODEL), lambda b, i, f: (0, 0))],
        out_specs=pl.BlockSpec((1, tm, D_MODEL), lambda b, i, f: (b, i, 0)),
        out_shape=jax.ShapeDtypeStruct((B, L, D_MODEL), F32),
        scratch_shapes=[pltpu.VMEM((tm, D_MODEL), BF16)],
        compiler_params=_params(("parallel", "parallel", "arbitrary")),
        name="ffn_ln2",
    )(x, mod3, mod3, mod3, w_in, w_in, w_out, lnw, lnb)


def _rope_body(cp_ref, sp_ref, cs_ref, ss_ref):
    half = RET_DK // 2
    for c_ref, s_ref, base, step in ((cp_ref, sp_ref, 0, 1), (cs_ref, ss_ref, PAST_LEN, 0)):
        shape = c_ref.shape
        lane = lax.broadcasted_iota(jnp.int32, shape, 1)
        row = lax.broadcasted_iota(jnp.int32, shape, 0)
        freq = jnp.exp(jnp.where(lane < half, lane, lane - half).astype(F32) * (-math.log(ROPE_BASE) / half))
        ang = (row * step + base).astype(F32) * freq
        c_ref[...] = jnp.cos(ang)
        s = jnp.sin(ang)
        s_ref[...] = jnp.where(lane < RET_DK // 2, -s, s)


def _rope_tables(seq):
    return pl.pallas_call(
        _rope_body,
        out_shape=[jax.ShapeDtypeStruct((seq, RET_DK), F32), jax.ShapeDtypeStruct((seq, RET_DK), F32),
                   jax.ShapeDtypeStruct((8, RET_DK), F32), jax.ShapeDtypeStruct((8, RET_DK), F32)],
        name="rope_tables",
    )()


def _rope(x, cos, sin_signed):
    return x * cos + pltpu.roll(x, RET_DK // 2, 1) * sin_signed


def _ret_prompt_body(q_ref, k_ref, v_ref, g_ref, cos_ref, sin_ref, gnw_ref, o_ref, s_ref, S):
    c = pl.program_id(1)
    C = RET_CHUNK

    @pl.when(c == 0)
    def _():
        S[...] = jnp.zeros_like(S)

    cos = cos_ref[...]
    sin = sin_ref[...]
    row = lax.broadcasted_iota(jnp.int32, (C, C), 0).astype(F32)
    col = lax.broadcasted_iota(jnp.int32, (C, C), 1).astype(F32)
    diff = row - col
    for h in range(RET_HEADS):
        lg = RET_LOG_G[h]
        sl = slice(h * RET_DK, (h + 1) * RET_DK)
        dmask = jnp.exp(jnp.where(diff >= 0, lg * diff, -jnp.inf))
        k_decay = jnp.exp(lg * (C - 1 - row))
        q_decay = jnp.exp(lg * (row + 1.0))
        for bt in range(q_ref.shape[0]):
            q = _rope(q_ref[bt, :, sl], cos, sin)
            k = _rope(k_ref[bt, :, sl], cos, sin) * (RET_DK ** -0.5)
            v = v_ref[bt, :, sl]
            scores = _bdot_nt(q, k) * dmask
            intra = _bdot(scores, v)
            kv = _bdot_tn(k * k_decay, v)
            s_prev = S[bt, h]
            cross = _bdot(q * q_decay, s_prev)
            S[bt, h] = s_prev * math.exp(lg * C) + kv
            ro = intra + cross
            mu = jnp.mean(ro, axis=-1, keepdims=True)
            rc = ro - mu
            var = jnp.mean(rc * rc, axis=-1, keepdims=True)
            rn = rc * lax.rsqrt(var + LN_EPS) * gnw_ref[:, sl] * _silu(g_ref[bt, :, sl])
            o_ref[bt, :, sl] = rn.astype(BF16)

    @pl.when(c == pl.num_programs(1) - 1)
    def _():
        s_ref[...] = S[...]


def _ret_prompt(proj, cos, sin, gnw, nbt):
    B, L, _ = proj.shape
    C = RET_CHUNK
    blk = lambda k: pl.BlockSpec((nbt, C, RET_W), lambda b, c: (b, c, k))
    return pl.pallas_call(
        _ret_prompt_body,
        grid=(B // nbt, L // C),
        in_specs=[blk(0), blk(1), blk(2), blk(3),
                  pl.BlockSpec((C, RET_DK), lambda b, c: (c, 0)),
                  pl.BlockSpec((C, RET_DK), lambda b, c: (c, 0)),
                  pl.BlockSpec((1, RET_W), lambda b, c: (0, 0))],
        out_specs=[pl.BlockSpec((nbt, C, RET_W), lambda b, c: (b, c, 0)),
                   pl.BlockSpec((nbt, RET_HEADS, RET_DK, RET_DK), lambda b, c: (b, 0, 0, 0))],
        out_shape=[jax.ShapeDtypeStruct((B, L, RET_W), BF16),
                   jax.ShapeDtypeStruct((B, RET_HEADS, RET_DK, RET_DK), F32)],
        scratch_shapes=[pltpu.VMEM((nbt, RET_HEADS, RET_DK, RET_DK), F32)],
        compiler_params=_params(("parallel", "arbitrary")),
        name="ret_prompt",
    )(proj, proj, proj, proj, cos, sin, gnw)


def _ret_sample_body(q_ref, k_ref, v_ref, g_ref, cos_ref, sin_ref, gnw_ref, dec_ref, s0_ref, o_ref, s_ref, obuf):
    nb = q_ref.shape[0]
    cos = cos_ref[0:1, :]
    sin = sin_ref[0:1, :]
    gdec = dec_ref[0]
    q = _rope(q_ref[...], cos, sin)
    k = _rope(k_ref[...], cos, sin) * (RET_DK ** -0.5)
    v = v_ref[...]
    qk = jnp.sum(q * k, axis=1, keepdims=True)
    qT = q.T
    kT = k.T
    for b in range(nb):
        s0 = s0_ref[0, b, 0]
        qcol = qT[:, b:b + 1]
        kcol = kT[:, b:b + 1]
        vrow = v[b:b + 1, :]
        cross = gdec * jnp.sum(s0 * qcol, axis=0, keepdims=True)
        s_ref[0, b, 0] = s0 * gdec + kcol * vrow
        obuf[b:b + 1, :] = qk[b:b + 1, :] * vrow + cross
    ro = obuf[...]
    mu = jnp.mean(ro, axis=-1, keepdims=True)
    rc = ro - mu
    var = jnp.mean(rc * rc, axis=-1, keepdims=True)
    o_ref[...] = (rc * lax.rsqrt(var + LN_EPS) * gnw_ref[...] * _silu(g_ref[...])).astype(BF16)


def _state_phases(body, n_in, state_pos, carried):
    def phased(*refs):
        p = pl.program_id(0)

        @pl.when(p == 0)
        def _():
            body(*refs)

        @pl.when(p == 1)
        def _():
            refs[state_pos][...] = jnp.zeros(refs[state_pos].shape, F32)

    if carried is None:
        return phased, 2, [], [], {}

    def carried_body(*refs):
        phased(*refs[:n_in], *refs[n_in + 1:])

    return carried_body, 1, [pl.BlockSpec(memory_space=pl.ANY)], [carried], {n_in: 1}


def _ret_sample(proj2, cos, sin, gnw, state, l, carried):
    nb = proj2.shape[0]
    H = RET_HEADS
    dec = jnp.asarray(np.broadcast_to(np.exp(np.array(RET_LOG_G))[:, None, None], (H, 1, RET_DK)), F32)
    hh = lambda p, h: h * (1 - p) + (H - 1) * p
    blk = lambda k: pl.BlockSpec((nb, RET_DK), lambda p, h: (0, k * H + hh(p, h)))
    in_specs = [blk(0), blk(1), blk(2), blk(3),
                pl.BlockSpec((8, RET_DK), lambda p, h: (0, 0)),
                pl.BlockSpec((8, RET_DK), lambda p, h: (0, 0)),
                pl.BlockSpec((1, RET_DK), lambda p, h: (0, hh(p, h))),
                pl.BlockSpec((1, 1, RET_DK), lambda p, h: (hh(p, h), 0, 0)),
                pl.BlockSpec((1, nb, 1, RET_DK, RET_DK), lambda p, h: (l, 0, hh(p, h), 0, 0))]
    body, phases, extra_specs, extra_args, aliases = _state_phases(_ret_sample_body, len(in_specs),
                                                                   len(in_specs) + 1, carried)
    return pl.pallas_call(
        body,
        grid=(phases, H),
        in_specs=in_specs + extra_specs,
        out_specs=[pl.BlockSpec((nb, RET_DK), lambda p, h: (0, hh(p, h))),
                   pl.BlockSpec((1, nb, 1, RET_DK, RET_DK), lambda p, h: (l + p, 0, h, 0, 0))],
        out_shape=[jax.ShapeDtypeStruct((nb, RET_W), BF16),
                   jax.ShapeDtypeStruct(state.shape, F32)],
        scratch_shapes=[pltpu.VMEM((nb, RET_DK), F32)],
        input_output_aliases=aliases,
        compiler_params=_params(("arbitrary", "arbitrary")),
        name="ret_sample",
    )(proj2, proj2, proj2, proj2, cos, sin, gnw, dec, state, *extra_args)


def _gdn_prompt_body(dq_ref, dk_ref, dv_ref, dz_ref, ab_ref, cw_ref, alog_ref, dtb_ref, nw_ref,
                     o_ref, s_ref, conv_ref, xbuf, S):
    c = pl.program_id(1)
    C = GDN_CHUNK
    H = GDN_HEADS
    dk = GDN_DK
    nbt = dq_ref.shape[0]

    @pl.when(c == 0)
    def _():
        S[...] = jnp.zeros_like(S)
        xbuf[:, 0:8, :] = jnp.zeros((nbt, 8, GDN_CONV_C), F32)

    row = lax.broadcasted_iota(jnp.int32, (C, dk), 0)
    lane = lax.broadcasted_iota(jnp.int32, (C, dk), 1)
    tri = lane <= row
    strict = lane < row
    r3 =lax.broadcasted_iota(jnp.int32, (C, 3 * C), 0)
    c3 = lax.broadcasted_iota(jnp.int32, (C, 3 * C), 1) % C
    tril_b = jnp.where(r3 >= c3, 1.0, 0.0).astype(BF16)

    def cumsum_rows(x):
        hi = x.astype(BF16)
        r1 = x - hi.astype(F32)
        mid = r1.astype(BF16)
        lo = (r1 - mid.astype(F32)).astype(BF16)
        return jnp.dot(tril_b, jnp.concatenate([hi, mid, lo], axis=0), preferred_element_type=F32)

    def lanes(x, j):
        return jnp.broadcast_to(x[:, j:j + 1], (x.shape[0], dk))

    cw = cw_ref[...]
    zrow_b = jnp.zeros((C, dk), BF16)

    items = []
    for bt in range(nbt):
        xbuf[bt, 8:8 + C, 0:GDN_W] = dq_ref[bt]
        xbuf[bt, 8:8 + C, GDN_W:2 * GDN_W] = dk_ref[bt]
        xbuf[bt, 8:8 + C, 2 * GDN_W:3 * GDN_W] = dv_ref[bt]

        def conv(off, bt=bt):
            y = xbuf[bt, 5:5 + C, off:off + dk] * cw[0:1, off:off + dk]
            for i in range(1, CONV_W):
                y = y + xbuf[bt, 5 + i:5 + i + C, off:off + dk] * cw[i:i + 1, off:off + dk]
            return _silu(y)

        ab = ab_ref[bt]
        g_c = -jnp.exp(alog_ref[...]) * jax.nn.softplus(ab + dtb_ref[...])
        gc_c = cumsum_rows(g_c)
        beta_c = jax.nn.sigmoid(ab)
        egc_c = jnp.exp(gc_c)
        gl_c = gc_c[C - 1:C, :]
        kdec_c = jnp.exp(gl_c - gc_c)
        egl_c = jnp.exp(gl_c)
        gc_t = jnp.concatenate([gc_c, jnp.zeros_like(gc_c)], axis=0).T

        for h in range(H):
            sl = slice(h * dk, (h + 1) * dk)
            q = conv(h * dk)
            k = conv(GDN_W + h * dk)
            v = conv(2 * GDN_W + h * dk)
            q = q * lax.rsqrt(jnp.sum(q * q, axis=-1, keepdims=True) + L2_EPS) * (dk ** -0.5)
            k = k * lax.rsqrt(jnp.sum(k * k, axis=-1, keepdims=True) + L2_EPS)
            beta = lanes(beta_c, H + h)
            egc = lanes(egc_c, h)
            kb = k * beta
            kq = jnp.concatenate([kb, q], axis=0).astype(BF16)
            k2 = jnp.concatenate([k.astype(BF16), zrow_b], axis=0)
            kkqk = lax.dot_general(kq, k2, NT, preferred_element_type=F32)
            dgc = lanes(gc_c, h) - jnp.broadcast_to(gc_t[h:h + 1, :], (C, dk))
            decay = jnp.exp(jnp.where(tri, dgc, -jnp.inf))
            a = jnp.where(strict, kkqk[0:C] * decay, 0.0)
            attn = (kkqk[C:2 * C] * decay)[:, 0:C].astype(BF16)
            rhs = jnp.concatenate([v * beta, kb * egc], axis=1).astype(BF16)
            items.append(dict(bt=bt, h=h, a=a, rhs=rhs, attn=attn, qd=(q * egc).astype(BF16),
                              kd=(k * lanes(kdec_c, h)).astype(BF16), egl=lanes(egl_c, h)))

    pair_col = lane % C
    eye_pair = (pair_col == row).astype(F32)
    left = lane < C

    def level_mask(s):
        br = row // s
        return (pair_col // s) == jnp.where(br % 2 == 1, br - 1, -1)

    def block_diag(x):
        zero = jnp.zeros_like(x)
        return jnp.concatenate([jnp.where(left, x, zero), jnp.where(left, zero, x)], axis=0)

    pairs = []
    for i in range(0, len(items), 2):
        a_pair = items[i]["a"] + pltpu.roll(items[i + 1]["a"], C, 1)
        pairs.append(dict(a=a_pair, d=eye_pair - jnp.where(level_mask(1), a_pair, 0.0)))
    s_blk = 2
    while s_blk < C:
        mask = level_mask(s_blk)
        for p in pairs:
            d = p["d"]
            sub = block_diag(jnp.where(mask, p["a"], 0.0)).astype(BF16)
            g = jnp.dot(d.astype(BF16), sub, preferred_element_type=F32)
            p["d"] = d - jnp.dot(g.astype(BF16), block_diag(d).astype(BF16), preferred_element_type=F32)
        s_blk *= 2
    for i, p in enumerate(pairs):
        r0, r1 = items[2 * i]["rhs"], items[2 * i + 1]["rhs"]
        zero = jnp.zeros_like(r0)
        rr = jnp.concatenate([jnp.concatenate([r0, zero], axis=1), jnp.concatenate([zero, r1], axis=1)], axis=0)
        sol = jnp.dot(p["d"].astype(BF16), rr, preferred_element_type=F32)
        items[2 * i]["sol"] = sol[:, 0:2 * dk]
        items[2 * i + 1]["sol"] = sol[:, 2 * dk:4 * dk]

    for it in items:
        bt, h = it["bt"], it["h"]
        sl = slice(h * dk, (h + 1) * dk)
        u = it["sol"][:, 0:dk]
        w = it["sol"][:, dk:2 * dk]
        s_prev = S[bt, h]
        wq = jnp.concatenate([w.astype(BF16), it["qd"]], axis=0)
        r = jnp.dot(wq, s_prev.astype(BF16), preferred_element_type=F32)
        v_new = u - r[0:C]
        o = r[C:2 * C] + jnp.dot(it["attn"], v_new.astype(BF16), preferred_element_type=F32)
        S[bt, h] = s_prev * it["egl"] + lax.dot_general(it["kd"], v_new.astype(BF16), TN, preferred_element_type=F32)
        o = o * lax.rsqrt(jnp.mean(o * o, axis=-1, keepdims=True) + RMS_EPS) * nw_ref[...]
        o_ref[bt, :, sl] = (o * _silu(dz_ref[bt, :, sl])).astype(BF16)

    for bt in range(nbt):
        xbuf[bt, 0:8, :] = xbuf[bt, C:C + 8, :]

    @pl.when(c == pl.num_programs(1) - 1)
    def _():
        s_ref[...] = S[...]
        conv_ref[...] = xbuf[:, C + 8 - (CONV_W - 1):C + 8, :]


def _gdn_prompt(proj, cw, alog_bc, dtb_bc, nw, nbt):
    B, L, _ = proj.shape
    C = GDN_CHUNK
    H = GDN_HEADS
    blk = lambda k: pl.BlockSpec((nbt, C, GDN_W), lambda b, c: (b, c, k))
    full = lambda shape: pl.BlockSpec(shape, lambda b, c: tuple(0 for _ in shape))
    return pl.pallas_call(
        _gdn_prompt_body,
        grid=(B // nbt, L // C),
        in_specs=[blk(2), blk(3), blk(4), blk(5),
                  pl.BlockSpec((nbt, C, 128), lambda b, c: (b, c, COL_AB // 128)),
                  full((CONV_W, GDN_CONV_C)), full((1, 128)), full((1, 128)), full((1, GDN_DK))],
        out_specs=[pl.BlockSpec((nbt, C, GDN_W), lambda b, c: (b, c, 0)),
                   pl.BlockSpec((nbt, H, GDN_DK, GDN_DK), lambda b, c: (b, 0, 0, 0)),
                   pl.BlockSpec((nbt, CONV_W - 1, GDN_CONV_C), lambda b, c: (b, 0, 0))],
        out_shape=[jax.ShapeDtypeStruct((B, L, GDN_W), BF16),
                   jax.ShapeDtypeStruct((B, H, GDN_DK, GDN_DK), F32),
                   jax.ShapeDtypeStruct((B, CONV_W - 1, GDN_CONV_C), F32)],
        scratch_shapes=[pltpu.VMEM((nbt, C + 8, GDN_CONV_C), F32),
                        pltpu.VMEM((nbt, H, GDN_DK, GDN_DK), F32)],
        compiler_params=_params(("parallel", "arbitrary")),
        name="gdn_prompt",
    )(proj, proj, proj, proj, proj, cw, alog_bc, dtb_bc, nw)


def _gdn_sample_body(dq_ref, dk_ref, dv_ref, dz_ref, ab_ref, csq_ref, csk_ref, csv_ref, cwq_ref, cwk_ref, cwv_ref,
                     alog_ref, dtb_ref, nw_ref, s0_ref, o_ref, s_ref, nq_ref, nk_ref, nv_ref, obuf):
    h = pl.program_id(1)
    nb = dq_ref.shape[0]
    dk = GDN_DK

    def conv(x_ref, cs_ref, w_ref, n_ref):
        x = x_ref[...]
        y = cs_ref[0] * w_ref[0:1, :] + cs_ref[1] * w_ref[1:2, :] + cs_ref[2] * w_ref[2:3, :] + x * w_ref[3:4, :]
        n_ref[0] = cs_ref[1]
        n_ref[1] = cs_ref[2]
        n_ref[2] = x
        return _silu(y)

    q = conv(dq_ref, csq_ref, cwq_ref, nq_ref)
    k = conv(dk_ref, csk_ref, cwk_ref, nk_ref)
    v = conv(dv_ref, csv_ref, cwv_ref, nv_ref)
    q = q * lax.rsqrt(jnp.sum(q * q, axis=-1, keepdims=True) + L2_EPS) * (dk ** -0.5)
    k = k * lax.rsqrt(jnp.sum(k * k, axis=-1, keepdims=True) + L2_EPS)
    er = lax.broadcasted_iota(jnp.int32, (128, dk), 0)
    ab = ab_ref[...]
    a_bc = jnp.dot(ab, (er == h).astype(F32), precision=HI, preferred_element_type=F32)
    b_bc = jnp.dot(ab, (er == h + GDN_HEADS).astype(F32), precision=HI, preferred_element_type=F32)
    g = -jnp.exp(alog_ref[0]) * jax.nn.softplus(a_bc + dtb_ref[0])
    eg = jnp.exp(g)
    beta = jax.nn.sigmoid(b_bc)
    qk = jnp.sum(q * k, axis=1, keepdims=True)
    qT = q.T
    kT = k.T
    for b in range(nb):
        s0 = s0_ref[0, b, 0]
        qcol = qT[:, b:b + 1]
        kcol = kT[:, b:b + 1]
        ks = jnp.sum(s0 * kcol, axis=0, keepdims=True)
        qs = jnp.sum(s0 * qcol, axis=0, keepdims=True)
        bb = beta[b:b + 1, :]
        egb = eg[b:b + 1, :]
        v_new = bb * v[b:b + 1, :] - bb * egb * ks
        s_ref[0, b, 0] = s0 * egb + kcol * v_new
        obuf[b:b + 1, :] = egb * qs + qk[b:b + 1, :] * v_new
    o = obuf[...]
    o = o * lax.rsqrt(jnp.mean(o * o, axis=-1, keepdims=True) + RMS_EPS) * nw_ref[...]
    o_ref[...] = (o * _silu(dz_ref[...])).astype(BF16)


def _gdn_sample(proj2, cs_t, cw, alog_t, dtb_t, nw, state, l, carried):
    nb = proj2.shape[0]
    H = GDN_HEADS
    dk = GDN_DK
    hh = lambda p, h: h * (1 - p) + (H - 1) * p
    blk = lambda k: pl.BlockSpec((nb, dk), lambda p, h: (0, k * H + hh(p, h)))
    csblk = lambda k: pl.BlockSpec((CONV_W - 1, nb, dk), lambda p, h: (0, 0, k * H + hh(p, h)))
    cwblk = lambda k: pl.BlockSpec((CONV_W, dk), lambda p, h: (0, k * H + hh(p, h)))
    tab = pl.BlockSpec((1, 1, dk), lambda p, h: (hh(p, h), 0, 0))
    nblk = pl.BlockSpec((CONV_W - 1, nb, dk), lambda p, h: (0, 0, hh(p, h)))
    nshape = jax.ShapeDtypeStruct((CONV_W - 1, nb, GDN_W), F32)
    in_specs = [blk(2), blk(3), blk(4), blk(5),
                pl.BlockSpec((nb, 128), lambda p, h: (0, COL_AB // 128)),
                csblk(0), csblk(1), csblk(2), cwblk(0), cwblk(1), cwblk(2),
                tab, tab, pl.BlockSpec((1, dk), lambda p, h: (0, 0)),
                pl.BlockSpec((1, nb, 1, dk, dk), lambda p, h: (l, 0, hh(p, h), 0, 0))]
    body, phases, extra_specs, extra_args, aliases = _state_phases(_gdn_sample_body, len(in_specs),
                                                                   len(in_specs) + 1, carried)
    return pl.pallas_call(
        body,
        grid=(phases, H),
        in_specs=in_specs + extra_specs,
        out_specs=[pl.BlockSpec((nb, dk), lambda p, h: (0, hh(p, h))),
                   pl.BlockSpec((1, nb, 1, dk, dk), lambda p, h: (l + p, 0, h, 0, 0)), nblk, nblk, nblk],
        out_shape=[jax.ShapeDtypeStruct((nb, GDN_W), BF16), jax.ShapeDtypeStruct(state.shape, F32),
                   nshape, nshape, nshape],
        scratch_shapes=[pltpu.VMEM((nb, dk), F32)],
        input_output_aliases=aliases,
        compiler_params=_params(("arbitrary", "arbitrary")),
        name="gdn_sample",
    )(proj2, proj2, proj2, proj2, proj2, cs_t, cs_t, cs_t, cw, cw, cw, alog_t, dtb_t, nw, state, *extra_args)


def _ssm_prep_body(are_ref, aim_ref, ldt_ref, bre_ref, bim_ref, lre_ref, lim_ref, obre_ref, obim_ref):
    ar = are_ref[0]
    ai = aim_ref[0]
    dt = jnp.exp(ldt_ref[0])
    er = jnp.exp(ar * dt)
    lr = er * jnp.cos(ai * dt)
    li = er * jnp.sin(ai * dt)
    lre_ref[0] = lr
    lim_ref[0] = li
    xr = lr - 1.0
    den = ar * ar + ai * ai
    cr = ((xr * ar + li * ai) / den)[:, None, :]
    ci = ((li * ar - xr * ai) / den)[:, None, :]
    br = bre_ref[0]
    bi = bim_ref[0]
    obre_ref[0] = cr * br - ci * bi
    obim_ref[0] = cr * bi + ci * br


def _ssm_prep(a_re, a_im, log_dt, b_re, b_im):
    G, N, P = SSM_GROUPS, SSM_N, SSM_GROUP
    ldt = jnp.broadcast_to(log_dt[:, :, None], (DEPTH, G, N))
    bt_re = jnp.transpose(b_re, (0, 1, 3, 2))
    bt_im = jnp.transpose(b_im, (0, 1, 3, 2))
    s2 = pl.BlockSpec((1, G, N), lambda l: (l, 0, 0))
    s3 = pl.BlockSpec((1, G, P, N), lambda l: (l, 0, 0, 0))
    return pl.pallas_call(
        _ssm_prep_body,
        grid=(DEPTH,),
        in_specs=[s2, s2, s2, s3, s3],
        out_specs=[s2, s2, s3, s3],
        out_shape=[jax.ShapeDtypeStruct((DEPTH, G, N), F32), jax.ShapeDtypeStruct((DEPTH, G, N), F32),
                   jax.ShapeDtypeStruct((DEPTH, G, P, N), F32), jax.ShapeDtypeStruct((DEPTH, G, P, N), F32)],
        name="ssm_prep",
    )(a_re, a_im, ldt, bt_re, bt_im)


def _ssm_tile_blockdiag(re, im):
    x = jnp.stack([re, im]).reshape(2, SSM_TILES, SSM_TILE_G, SSM_GROUP, SSM_N)
    eye = jnp.eye(SSM_TILE_G, dtype=x.dtype)
    y = x[:, :, :, :, None, :] * eye[None, None, :, None, :, None]
    y = jnp.transpose(y, (1, 2, 3, 0, 4, 5))
    return y.reshape(SSM_TILES, SSM_TILE_G * SSM_GROUP, SSM_TILE_W)


def _ssm_state_to_tiles(re, im):
    B = re.shape[0]
    x = jnp.stack([re, im], axis=1).reshape(B, 2, SSM_TILES, SSM_HALF)
    return jnp.transpose(x, (0, 2, 1, 3)).reshape(B, SSM_TILES * SSM_TILE_W)


def _ssm_tiles_to_state(x):
    B = x.shape[0]
    x = jnp.transpose(x.reshape(B, SSM_TILES, 2, SSM_HALF), (0, 2, 1, 3)).reshape(B, 2, SSM_GROUPS, SSM_N)
    return x[:, 0], x[:, 1]


def _cmul(ar, ai, br, bi):
    return ar * br - ai * bi, ar * bi + ai * br


def _s5_prompt_body(u_ref, bt_ref, ct_ref, lam_ref, d_ref, y_ref, hl_ref, buf):
    L = u_ref.shape[1]
    nseg = L // SSM_SEG
    ns = SSM_HALF // 128
    bmat = bt_ref[0].astype(BF16)
    for i in range(nseg):
        bu = jnp.dot(u_ref[0, i * SSM_SEG:(i + 1) * SSM_SEG, :].astype(BF16), bmat, preferred_element_type=F32)
        for s in range(2 * ns):
            buf[s, i * SSM_PITCH:i * SSM_PITCH + SSM_SEG, :] = bu[:, s * 128:(s + 1) * 128]

    lam = lam_ref[0]
    a_re = [jnp.broadcast_to(lam[:, s * 128:(s + 1) * 128], (nseg, 128)) for s in range(ns)]
    a_im = [jnp.broadcast_to(lam[:, SSM_HALF + s * 128:SSM_HALF + (s + 1) * 128], (nseg, 128)) for s in range(ns)]

    def scan_step(t, carry):
        out = []
        for s in range(ns):
            hr, hi = carry[2 * s], carry[2 * s + 1]
            pr, pi = _cmul(a_re[s], a_im[s], hr, hi)
            hr = pr + buf[s, pl.ds(t, nseg, stride=SSM_PITCH), :]
            hi = pi + buf[ns + s, pl.ds(t, nseg, stride=SSM_PITCH), :]
            buf[s, pl.ds(t, nseg, stride=SSM_PITCH), :] = hr
            buf[ns + s, pl.ds(t, nseg, stride=SSM_PITCH), :] = hi
            out += [hr, hi]
        return tuple(out)

    zero = jnp.zeros((nseg, 128), F32)
    ends = lax.fori_loop(0, SSM_SEG, scan_step, tuple(zero for _ in range(2 * ns)), unroll=SSM_UNROLL)

    sub = lax.broadcasted_iota(jnp.int32, (nseg, 128), 0)
    carries = []
    for s in range(ns):
        pr, pi = a_re[s], a_im[s]
        for _ in range(int(math.log2(SSM_SEG))):
            pr, pi = _cmul(pr, pi, pr, pi)
        xr, xi = ends[2 * s], ends[2 * s + 1]
        shift = 1
        while shift < nseg:
            sr = jnp.where(sub >= shift, pltpu.roll(xr, shift, 0), 0.0)
            si = jnp.where(sub >= shift, pltpu.roll(xi, shift, 0), 0.0)
            mr, mi = _cmul(pr, pi, sr, si)
            xr, xi = xr + mr, xi + mi
            pr, pi = _cmul(pr, pi, pr, pi)
            shift *= 2
        hl_ref[0, 0, :, s * 128:(s + 1) * 128] = xr[nseg - 1:nseg, :]
        hl_ref[0, 0, :, SSM_HALF + s * 128:SSM_HALF + (s + 1) * 128] = xi[nseg - 1:nseg, :]
        carries += [jnp.where(sub >= 1, pltpu.roll(xr, 1, 0), 0.0), jnp.where(sub >= 1, pltpu.roll(xi, 1, 0), 0.0)]

    def fix_step(t, pw):
        out = []
        for s in range(ns):
            pr, pi = pw[2 * s], pw[2 * s + 1]
            fr, fi = _cmul(pr, pi, carries[2 * s], carries[2 * s + 1])
            buf[s, pl.ds(t, nseg, stride=SSM_PITCH), :] += fr
            buf[ns + s, pl.ds(t, nseg, stride=SSM_PITCH), :] += fi
            nr, ni = _cmul(pr, pi, a_re[s], a_im[s])
            out += [nr, ni]
        return tuple(out)

    pw0 = []
    for s in range(ns):
        pw0 += [a_re[s], a_im[s]]
    lax.fori_loop(0, SSM_SEG, fix_step, tuple(pw0), unroll=SSM_UNROLL)

    cmat = ct_ref[0]
    cmat = jnp.concatenate([cmat[:, 0:SSM_HALF], -cmat[:, SSM_HALF:]], axis=1).astype(BF16)
    for i in range(nseg):
        rows = slice(i * SSM_PITCH, i * SSM_PITCH + SSM_SEG)
        h = jnp.concatenate([buf[s, rows, :].astype(BF16) for s in range(2 * ns)], axis=1)
        y = lax.dot_general(h, cmat, NT, preferred_element_type=F32)
        u = u_ref[0, i * SSM_SEG:(i + 1) * SSM_SEG, :]
        y_ref[0, i * SSM_SEG:(i + 1) * SSM_SEG, :] = jax.nn.gelu(y + d_ref[...] * u)


def _s5_prompt(proj, bt, ct, lam, d):
    B, L, _ = proj.shape
    nseg = L // SSM_SEG
    tile = lambda shape: pl.BlockSpec(shape, lambda b, j: (j,) + tuple(0 for _ in shape[1:]))
    return pl.pallas_call(
        _s5_prompt_body,
        grid=(B, SSM_TILES),
        in_specs=[pl.BlockSpec((1, L, 128), lambda b, j: (b, 0, COL_SU // 128 + j)),
                  tile((1, 128, SSM_TILE_W)), tile((1, 128, SSM_TILE_W)), tile((1, 1, SSM_TILE_W)),
                  pl.BlockSpec((1, 128), lambda b, j: (0, j))],
        out_specs=[pl.BlockSpec((1, L, 128), lambda b, j: (b, 0, j)),
                   pl.BlockSpec((1, 1, 1, SSM_TILE_W), lambda b, j: (b, j, 0, 0))],
        out_shape=[jax.ShapeDtypeStruct((B, L, SSM_W), F32),
                   jax.ShapeDtypeStruct((B, SSM_TILES, 1, SSM_TILE_W), F32)],
        scratch_shapes=[pltpu.VMEM((SSM_TILE_W // 128, nseg * SSM_PITCH, 128), F32)],
        compiler_params=_params(("parallel", "parallel")),
        name="s5_prompt",
    )(proj, bt, ct, lam, d)


def _s5_sample_body(u_ref, bt_ref, ct_ref, lam_ref, d_ref, h0_ref, y_ref, h_ref):
    u = u_ref[...]
    bu = jnp.dot(u, bt_ref[0], precision=HI, preferred_element_type=F32)
    lam = lam_ref[0]
    lr, li = lam[:, 0:SSM_HALF], lam[:, SSM_HALF:]
    h0r, h0i = h0_ref[:, 0:SSM_HALF], h0_ref[:, SSM_HALF:]
    pr, pi = _cmul(lr, li, h0r, h0i)
    hr = pr + bu[:, 0:SSM_HALF]
    hi = pi + bu[:, SSM_HALF:]
    h_ref[:, 0:SSM_HALF] = hr
    h_ref[:, SSM_HALF:] = hi
    ct = ct_ref[0]
    y = lax.dot_general(hr, ct[:, 0:SSM_HALF], NT, precision=HI, preferred_element_type=F32)
    y -= lax.dot_general(hi, ct[:, SSM_HALF:], NT, precision=HI, preferred_element_type=F32)
    y_ref[...] = jax.nn.gelu(y + d_ref[...] * u)


def _s5_sample(proj2, bt, ct, lam, d, h0):
    nb = proj2.shape[0]
    tile = lambda shape: pl.BlockSpec(shape, lambda j: (j,) + tuple(0 for _ in shape[1:]))
    hblk = pl.BlockSpec((nb, SSM_TILE_W), lambda j: (0, j))
    return pl.pallas_call(
        _s5_sample_body,
        grid=(SSM_TILES,),
        in_specs=[pl.BlockSpec((nb, 128), lambda j: (0, COL_SU // 128 + j)),
                  tile((1, 128, SSM_TILE_W)), tile((1, 128, SSM_TILE_W)), tile((1, 1, SSM_TILE_W)),
                  pl.BlockSpec((1, 128), lambda j: (0, j)), hblk],
        out_specs=[pl.BlockSpec((nb, 128), lambda j: (0, j)), hblk],
        out_shape=[jax.ShapeDtypeStruct((nb, SSM_W), F32), jax.ShapeDtypeStruct(h0.shape, F32)],
        compiler_params=_params(("parallel",)),
        name="s5_sample",
    )(proj2, bt, ct, lam, d, h0)


def kernel(x_prompt, x_sample, c_prompt, c_sample, state_ret, state_gdn, state_conv, state_ssm_re, state_ssm_im, w_mod, b_mod, w_in, conv_w, ret_gn_w, gdn_a_log, gdn_dt_bias, gdn_norm_w, ssm_a_re, ssm_a_im, ssm_log_dt, ssm_b_re, ssm_b_im, ssm_c_re, ssm_c_im, ssm_d, ssm_w_glu, w_out, ln1_w, ln1_b, w_ffn_in, w_ffn_out, ln2_w, ln2_b):
    Bp, Lp, _ = x_prompt.shape
    Bs = x_sample.shape[0]
    assert x_sample.shape[1] == 1 and Bs % 8 == 0

    pad_rows = (-(Bp + Bs)) % 8
    c_all = jnp.concatenate([c_prompt, c_sample, jnp.zeros((pad_rows, D_MODEL), F32)], axis=0)
    mod_all = _modulation(c_all, w_mod, b_mod)
    lam_re, lam_im, bbar_re, bbar_im = _ssm_prep(ssm_a_re, ssm_a_im, ssm_log_dt, ssm_b_re, ssm_b_im)
    cos_p, sin_p, cos_s, sin_s = _rope_tables(Lp)

    w_in_b = jnp.transpose(w_in[:, :, :COL_SU].reshape(DEPTH, D_MODEL, COL_SU // PROJ_TN, PROJ_TN),
                           (0, 2, 1, 3)).astype(BF16)
    w_in_tail = w_in[:, :, COL_SU:].astype(BF16)
    w_out_b = _to_bf16(w_out, rows=512)
    w_glu_b = ssm_w_glu.astype(BF16)
    w_ffi_b = _to_bf16_col_tiles(w_ffn_in, rows=128, tc=FFN_TF)
    w_ffo_b = _to_bf16(w_ffn_out, rows=512)

    yp = x_prompt
    ys = x_sample.reshape(1, Bs, D_MODEL)
    outs_p, outs_s = [], []
    ret_s = gdn_s = None
    for l in range(DEPTH):
        w_tail = jnp.concatenate(
            [w_in_tail[l, :, 2 * GDN_HEADS:], w_in_tail[l, :, :2 * GDN_HEADS],
             jnp.zeros((D_MODEL, N_PROJ - N_IN), BF16)], axis=1)
        mod_p = mod_all[l, :Bp][:, None, :]
        mod_s = mod_all[l, Bp:Bp + Bs][None]
        gnw = ret_gn_w[l][None, :]
        nw = gdn_norm_w[l][None, :]
        alog_bc = jnp.pad(gdn_a_log[l], (0, 128 - GDN_HEADS))[None, :]
        dtb_bc = jnp.pad(gdn_dt_bias[l], (0, 128 - GDN_HEADS))[None, :]
        alog_t = jnp.broadcast_to(gdn_a_log[l][:, None, None], (GDN_HEADS, 1, GDN_DK))
        dtb_t = jnp.broadcast_to(gdn_dt_bias[l][:, None, None], (GDN_HEADS, 1, GDN_DK))
        ssm_bt = _ssm_tile_blockdiag(bbar_re[l], bbar_im[l])
        ssm_ct = _ssm_tile_blockdiag(ssm_c_re[l], ssm_c_im[l])
        ssm_lam = jnp.concatenate([lam_re[l].reshape(SSM_TILES, 1, SSM_HALF),
                                   lam_im[l].reshape(SSM_TILES, 1, SSM_HALF)], axis=-1)
        ssm_dl = ssm_d[l][None, :]
        lnw1, lnb1 = ln1_w[l][None, :], ln1_b[l][None, :]
        lnw2, lnb2 = ln2_w[l][None, :], ln2_b[l][None, :]

        proj = _inproj(yp, mod_p, w_in_b, l, w_tail, tm=1024, tn=PROJ_TN)
        ro, ret_p = _ret_prompt(proj, cos_p, sin_p, gnw, nbt=4)
        go, gdn_p, conv_p = _gdn_prompt(proj, conv_w[l], alog_bc, dtb_bc, nw, nbt=4)
        so, hl = _s5_prompt(proj, ssm_bt, ssm_ct, ssm_lam, ssm_dl)
        re_p, im_p = _ssm_tiles_to_state(hl.reshape(Bp, SSM_TILES * SSM_TILE_W))
        x1 = _outproj(ro, go, so, yp, mod_p, w_glu_b, w_out_b, l, lnw1, lnb1, tm=512)
        yp = _ffn(x1, mod_p, w_ffi_b, w_ffo_b, l, lnw2, lnb2, tm=512, tf=FFN_TF)
        outs_p.append((ret_p, gdn_p, conv_p, re_p, im_p))

        proj = _inproj(ys, mod_s, w_in_b, l, w_tail, tm=Bs, tn=PROJ_TN)
        proj2 = proj.reshape(Bs, N_PROJ)
        ro, ret_s = _ret_sample(proj2, cos_s, sin_s, gnw, state_ret, l, ret_s)
        cs_t = jnp.transpose(state_conv[l], (1, 0, 2))
        go, gdn_s, nq, nk, nv = _gdn_sample(proj2, cs_t, conv_w[l], alog_t, dtb_t, nw, state_gdn, l, gdn_s)
        conv_s = jnp.transpose(jnp.concatenate([nq, nk, nv], axis=-1), (1, 0, 2))
        h0 = _ssm_state_to_tiles(state_ssm_re[l], state_ssm_im[l])
        so, hn = _s5_sample(proj2, ssm_bt, ssm_ct, ssm_lam, ssm_dl, h0)
        re_s, im_s = _ssm_tiles_to_state(hn)
        x1 = _outproj(ro[None], go[None], so[None], ys, mod_s, w_glu_b, w_out_b, l, lnw1, lnb1, tm=Bs)
        ys = _ffn(x1, mod_s, w_ffi_b, w_ffo_b, l, lnw2, lnb2, tm=Bs, tf=FFN_TF)
        outs_s.append((conv_s, re_s, im_s))

    ret_p, gdn_p, conv_p, re_p, im_p = [jnp.stack(t) for t in zip(*outs_p)]
    conv_s, re_s, im_s = [jnp.stack(t) for t in zip(*outs_s)]
    return (yp, ys.reshape(Bs, 1, D_MODEL), ret_p, gdn_p, conv_p, re_p, im_p,
            ret_s, gdn_s, conv_s, re_s, im_s)
```

```python
import functools
import math

import numpy as np
import jax
import jax.numpy as jnp
from jax import lax
from jax.experimental import pallas as pl
from jax.experimental.pallas import tpu as pltpu

F32 = jnp.float32
BF16 = jnp.bfloat16

D_MODEL = 2048
DEPTH = 2
PAST_LEN = 16384
RET_HEADS = 4
RET_DK = 128
RET_W = 512
RET_CHUNK = 128
ROPE_BASE = 10000.0
GDN_HEADS = 8
GDN_DK = 128
GDN_W = 1024
GDN_CHUNK = 64
CONV_W = 4
GDN_CONV_C = 3072
SSM_W = 512
SSM_GROUP = 16
SSM_GROUPS = 32
SSM_N = 64
D_FF = 5632
ALPHA = (2 * DEPTH) ** 0.25
LN_EPS = 1e-5
RMS_EPS = 1e-6
L2_EPS = 1e-6

COL_SU = 6144
COL_AB = 6656
N_IN = 6672
PROJ_TN = 768
N_PROJ = COL_SU + PROJ_TN

SSM_TILES = 4
SSM_TILE_G = 8
SSM_HALF = SSM_TILE_G * SSM_N
SSM_TILE_W = 2 * SSM_HALF
SSM_SEG = 256
SSM_PITCH = 260
SSM_UNROLL = 4

VMEM_LIMIT = 60 * 1024 * 1024
OUTPROJ_ROW_CHUNK = 256
FFN_ROW_CHUNK = 256
FFN_TF = 256

RET_LOG_G = [math.log1p(-(2.0 ** (-5.0 - h))) for h in range(RET_HEADS)]

NT = (((1,), (1,)), ((), ()))
TN = (((0,), (0,)), ((), ()))
HI = lax.Precision.HIGHEST


def _params(sem):
    return pltpu.CompilerParams(dimension_semantics=sem, vmem_limit_bytes=VMEM_LIMIT)


def _silu(x):
    return x * jax.nn.sigmoid(x)


def _bdot(a, b):
    return jnp.dot(a.astype(BF16), b.astype(BF16), preferred_element_type=F32)


def _hdot(a, b):
    return jnp.dot(a, b, precision=HI, preferred_element_type=F32)


def _bdot_nt(a, b):
    return lax.dot_general(a.astype(BF16), b.astype(BF16), NT, preferred_element_type=F32)


def _bdot_tn(a, b):
    return lax.dot_general(a.astype(BF16), b.astype(BF16), TN, preferred_element_type=F32)


def _layernorm(z, w, b):
    mu = jnp.mean(z, axis=-1, keepdims=True)
    zc = z - mu
    var = jnp.mean(zc * zc, axis=-1, keepdims=True)
    return zc * lax.rsqrt(var + LN_EPS) * w + b


def _cast_body(x_ref, o_ref):
    o_ref[...] = x_ref[...].astype(o_ref.dtype)


def _to_bf16(w, rows):
    depth, r, n = w.shape
    spec = pl.BlockSpec((1, rows, n), lambda l, i: (l, i, 0))
    return pl.pallas_call(
        _cast_body,
        grid=(depth, r // rows),
        in_specs=[spec],
        out_specs=spec,
        out_shape=jax.ShapeDtypeStruct(w.shape, BF16),
        compiler_params=_params(("parallel", "parallel")),
        name="to_bf16",
    )(w)


def _cast_col_tiles_body(x_ref, o_ref):
    tc = o_ref.shape[3]
    for t in range(o_ref.shape[1]):
        o_ref[0, t] = x_ref[0, :, t * tc:(t + 1) * tc].astype(o_ref.dtype)


def _to_bf16_col_tiles(w, rows, tc):
    depth, r, n = w.shape
    nt = n // tc
    return pl.pallas_call(
        _cast_col_tiles_body,
        grid=(depth, r // rows),
        in_specs=[pl.BlockSpec((1, rows, n), lambda l, i: (l, i, 0))],
        out_specs=pl.BlockSpec((1, nt, rows, tc), lambda l, i: (l, 0, i, 0)),
        out_shape=jax.ShapeDtypeStruct((depth, nt, r, tc), BF16),
        compiler_params=_params(("parallel", "parallel")),
        name="to_bf16_col_tiles",
    )(w)


def _mod_body(c_ref, w_ref, b_ref, o_ref):
    c = c_ref[...]
    o_ref[0] = _bdot(_silu(c), w_ref[0]) + b_ref[0]


def _modulation(c_all, w_mod, b_mod):
    rows = c_all.shape[0]
    tn = 1024
    n = 6 * D_MODEL
    return pl.pallas_call(
        _mod_body,
        grid=(DEPTH, n // tn),
        in_specs=[pl.BlockSpec((rows, D_MODEL), lambda l, j: (0, 0)),
                  pl.BlockSpec((1, D_MODEL, tn), lambda l, j: (l, 0, j)),
                  pl.BlockSpec((1, 1, tn), lambda l, j: (l, 0, j))],
        out_specs=pl.BlockSpec((1, rows, tn), lambda l, j: (l, 0, j)),
        out_shape=jax.ShapeDtypeStruct((DEPTH, rows, n), F32),
        compiler_params=_params(("parallel", "parallel")),
        name="modulation",
    )(c_all, w_mod, b_mod.reshape(DEPTH, 1, n))


def _mod_spec(mod3, tm, k, grid_rank):
    per_token = mod3.shape[1] != 1
    lmb = tm if per_token else 1
    if grid_rank == 3:
        return pl.BlockSpec((1, lmb, D_MODEL), lambda b, i, j: (b, i if per_token else 0, k))
    return pl.BlockSpec((1, lmb, D_MODEL), lambda b, i: (b, i if per_token else 0, k))


def _inproj_body(x_ref, sc_ref, sh_ref, w_ref, wt_ref, o_ref, h_ref, *, n_main):
    j = pl.program_id(2)
    tm = x_ref.shape[1]
    per_token = sc_ref.shape[1] != 1

    @pl.when(j == 0)
    def _():
        step = min(tm, FFN_ROW_CHUNK)
        for r in range(0, tm, step):
            rows = slice(r, r + step)
            sc = sc_ref[0, rows, :] if per_token else sc_ref[0]
            sh = sh_ref[0, rows, :] if per_token else sh_ref[0]
            h = (x_ref[0, rows, :] * (1.0 + sc) + sh).astype(BF16)
            h_ref[rows, :] = h
            o_ref[0, rows, :] = jnp.dot(h, w_ref[0, 0], preferred_element_type=F32)

    @pl.when(jnp.logical_and(j > 0, j < n_main))
    def _():
        o_ref[0] = jnp.dot(h_ref[...], w_ref[0, 0], preferred_element_type=F32)

    @pl.when(j == n_main)
    def _():
        o_ref[0] = jnp.dot(h_ref[...], wt_ref[...], preferred_element_type=F32)


def _inproj(x, mod3, w_in, l, w_tail, tm, tn):
    B, L, _ = x.shape
    n_main = COL_SU // tn
    assert n_main * tn == COL_SU and w_tail.shape == (D_MODEL, tn) and (n_main + 1) * tn == N_PROJ
    return pl.pallas_call(
        functools.partial(_inproj_body, n_main=n_main),
        grid=(B, L // tm, n_main + 1),
        in_specs=[pl.BlockSpec((1, tm, D_MODEL), lambda b, i, j: (b, i, 0)),
                  _mod_spec(mod3, tm, 1, 3),
                  _mod_spec(mod3, tm, 0, 3),
                  pl.BlockSpec((1, 1, D_MODEL, tn), lambda b, i, j: (l, jnp.minimum(j, n_main - 1), 0, 0)),
                  pl.BlockSpec((D_MODEL, tn), lambda b, i, j: (0, 0))],
        out_specs=pl.BlockSpec((1, tm, tn), lambda b, i, j: (b, i, j)),
        out_shape=jax.ShapeDtypeStruct((B, L, N_PROJ), F32),
        scratch_shapes=[pltpu.VMEM((tm, D_MODEL), BF16)],
        compiler_params=_params(("parallel", "parallel", "arbitrary")),
        name="inproj",
    )(x, mod3, mod3, w_in, w_tail)


def _outproj_body(ro_ref, go_ref, y_ref, x_ref, g1_ref, wglu_ref, wout_ref, lnw_ref, lnb_ref, o_ref, *, row_chunk):
    tm = x_ref.shape[1]
    for r in range(0, tm, row_chunk):
        rows = slice(r, r + row_chunk)
        y = y_ref[0, rows, :]
        so = y * jax.nn.sigmoid(_bdot(y, wglu_ref[0]))
        acc = jnp.dot(ro_ref[0, rows, :], wout_ref[0, 0:RET_W, :], preferred_element_type=F32)
        acc += jnp.dot(go_ref[0, rows, :], wout_ref[0, RET_W:RET_W + GDN_W, :], preferred_element_type=F32)
        acc += jnp.dot(so.astype(BF16), wout_ref[0, RET_W + GDN_W:, :], preferred_element_type=F32)
        g1 = g1_ref[0] if g1_ref.shape[1] == 1 else g1_ref[0, rows, :]
        z = ALPHA * x_ref[0, rows, :] + g1 * acc
        o_ref[0, rows, :] = _layernorm(z, lnw_ref[...], lnb_ref[...])


def _outproj(ro, go, y, x, mod3, wglu, wout, l, lnw, lnb, tm):
    B, L, _ = x.shape
    return pl.pallas_call(
        functools.partial(_outproj_body, row_chunk=min(tm, OUTPROJ_ROW_CHUNK)),
        grid=(B, L // tm),
        in_specs=[pl.BlockSpec((1, tm, RET_W), lambda b, i: (b, i, 0)),
                  pl.BlockSpec((1, tm, GDN_W), lambda b, i: (b, i, 0)),
                  pl.BlockSpec((1, tm, SSM_W), lambda b, i: (b, i, 0)),
                  pl.BlockSpec((1, tm, D_MODEL), lambda b, i: (b, i, 0)),
                  _mod_spec(mod3, tm, 2, 2),
                  pl.BlockSpec((1, SSM_W, SSM_W), lambda b, i: (l, 0, 0)),
                  pl.BlockSpec((1, D_MODEL, D_MODEL), lambda b, i: (l, 0, 0)),
                  pl.BlockSpec((1, D_MODEL), lambda b, i: (0, 0)),
                  pl.BlockSpec((1, D_MODEL), lambda b, i: (0, 0))],
        out_specs=pl.BlockSpec((1, tm, D_MODEL), lambda b, i: (b, i, 0)),
        out_shape=jax.ShapeDtypeStruct((B, L, D_MODEL), F32),
        compiler_params=_params(("parallel", "parallel")),
        name="outproj_ln1",
    )(ro, go, y, x, mod3, wglu, wout, lnw, lnb)


def _ffn_body(x_ref, sc_ref, sh_ref, g2_ref, wg_ref, wu_ref, wo_ref, lnw_ref, lnb_ref, o_ref, h_ref):
    f = pl.program_id(2)
    last = pl.num_programs(2) - 1
    tm = x_ref.shape[1]
    chunks = [slice(r, r + min(tm, FFN_ROW_CHUNK)) for r in range(0, tm, min(tm, FFN_ROW_CHUNK))]
    per_token = sc_ref.shape[1] != 1

    def mod(ref, rows):
        return ref[0, rows, :] if per_token else ref[0]

    def contribution(h):
        gate = jnp.dot(h, wg_ref[0, 0], preferred_element_type=F32)
        up = jnp.dot(h, wu_ref[0, 0], preferred_element_type=F32)
        act = (_silu(gate) * up).astype(BF16)
        return jnp.dot(act, wo_ref[0], preferred_element_type=F32)

    @pl.when(f == 0)
    def _():
        for rows in chunks:
            h = (x_ref[0, rows, :] * (1.0 + mod(sc_ref, rows)) + mod(sh_ref, rows)).astype(BF16)
            h_ref[rows, :] = h
            o_ref[0, rows, :] = contribution(h)

    @pl.when(jnp.logical_and(f > 0, f < last))
    def _():
        o_ref[0] += contribution(h_ref[...])

    @pl.when(f == last)
    def _():
        for rows in chunks:
            acc = o_ref[0, rows, :] + contribution(h_ref[rows, :])
            z = ALPHA * x_ref[0, rows, :] + mod(g2_ref, rows) * acc
            o_ref[0, rows, :] = _layernorm(z, lnw_ref[...], lnb_ref[...])


def _ffn(x, mod3, w_in, w_out, l, lnw, lnb, tm, tf):
    B, L, _ = x.shape
    nf = D_FF // tf
    return pl.pallas_call(
        _ffn_body,
        grid=(B, L // tm, nf),
        in_specs=[pl.BlockSpec((1, tm, D_MODEL), lambda b, i, f: (b, i, 0)),
                  _mod_spec(mod3, tm, 4, 3),
                  _mod_spec(mod3, tm, 3, 3),
                  _mod_spec(mod3, tm, 5, 3),
                  pl.BlockSpec((1, 1, D_MODEL, tf), lambda b, i, f: (l, f, 0, 0)),
                  pl.BlockSpec((1, 1, D_MODEL, tf), lambda b, i, f: (l, f + nf, 0, 0)),
                  pl.BlockSpec((1, tf, D_MODEL), lambda b, i, f: (l, f, 0)),
                  pl.BlockSpec((1, D_MODEL), lambda b, i, f: (0, 0)),
                  pl.BlockSpec((1, D_MODEL), lambda b, i, f: (0, 0))],
        out_specs=pl.BlockSpec((1, tm, D_MODEL), lambda b, i, f: (b, i, 0)),
        out_shape=jax.ShapeDtypeStruct((B, L, D_MODEL), F32),
        scratch_shapes=[pltpu.VMEM((tm, D_MODEL), BF16)],
        compiler_params=_params(("parallel", "parallel", "arbitrary")),
        name="ffn_ln2",
    )(x, mod3, mod3, mod3, w_in, w_in, w_out, lnw, lnb)


def _rope_body(cp_ref, sp_ref, cs_ref, ss_ref):
    half = RET_DK // 2
    for c_ref, s_ref, base, step in ((cp_ref, sp_ref, 0, 1), (cs_ref, ss_ref, PAST_LEN, 0)):
        shape = c_ref.shape
        lane = lax.broadcasted_iota(jnp.int32, shape, 1)
        row = lax.broadcasted_iota(jnp.int32, shape, 0)
        freq = jnp.exp(jnp.where(lane < half, lane, lane - half).astype(F32) * (-math.log(ROPE_BASE) / half))
        ang = (row * step + base).astype(F32) * freq
        c_ref[...] = jnp.cos(ang)
        s = jnp.sin(ang)
        s_ref[...] = jnp.where(lane < RET_DK // 2, -s, s)


def _rope_tables(seq):
    return pl.pallas_call(
        _rope_body,
        out_shape=[jax.ShapeDtypeStruct((seq, RET_DK), F32), jax.ShapeDtypeStruct((seq, RET_DK), F32),
                   jax.ShapeDtypeStruct((8, RET_DK), F32), jax.ShapeDtypeStruct((8, RET_DK), F32)],
        name="rope_tables",
    )()


def _rope(x, cos, sin_signed):
    return x * cos + pltpu.roll(x, RET_DK // 2, 1) * sin_signed


def _ret_prompt_body(q_ref, k_ref, v_ref, g_ref, cos_ref, sin_ref, gnw_ref, o_ref, s_ref, S):
    c = pl.program_id(1)
    C = RET_CHUNK

    @pl.when(c == 0)
    def _():
        S[...] = jnp.zeros_like(S)

    cos = cos_ref[...]
    sin = sin_ref[...]
    row = lax.broadcasted_iota(jnp.int32, (C, C), 0).astype(F32)
    col = lax.broadcasted_iota(jnp.int32, (C, C), 1).astype(F32)
    diff = row - col
    for h in range(RET_HEADS):
        lg = RET_LOG_G[h]
        sl = slice(h * RET_DK, (h + 1) * RET_DK)
        dmask = jnp.exp(jnp.where(diff >= 0, lg * diff, -jnp.inf))
        k_decay = jnp.exp(lg * (C - 1 - row))
        q_decay = jnp.exp(lg * (row + 1.0))
        for bt in range(q_ref.shape[0]):
            q = _rope(q_ref[bt, :, sl], cos, sin)
            k = _rope(k_ref[bt, :, sl], cos, sin) * (RET_DK ** -0.5)
            v = v_ref[bt, :, sl]
            scores = _bdot_nt(q, k) * dmask
            intra = _bdot(scores, v)
            kv = _bdot_tn(k * k_decay, v)
            s_prev = S[bt, h]
            cross = _bdot(q * q_decay, s_prev)
            S[bt, h] = s_prev * math.exp(lg * C) + kv
            ro = intra + cross
            mu = jnp.mean(ro, axis=-1, keepdims=True)
            rc = ro - mu
            var = jnp.mean(rc * rc, axis=-1, keepdims=True)
            rn = rc * lax.rsqrt(var + LN_EPS) * gnw_ref[:, sl] * _silu(g_ref[bt, :, sl])
            o_ref[bt, :, sl] = rn.astype(BF16)

    @pl.when(c == pl.num_programs(1) - 1)
    def _():
        s_ref[...] = S[...]


def _ret_prompt(proj, cos, sin, gnw, nbt):
    B, L, _ = proj.shape
    C = RET_CHUNK
    blk = lambda k: pl.BlockSpec((nbt, C, RET_W), lambda b, c: (b, c, k))
    return pl.pallas_call(
        _ret_prompt_body,
        grid=(B // nbt, L // C),
        in_specs=[blk(0), blk(1), blk(2), blk(3),
                  pl.BlockSpec((C, RET_DK), lambda b, c: (c, 0)),
                  pl.BlockSpec((C, RET_DK), lambda b, c: (c, 0)),
                  pl.BlockSpec((1, RET_W), lambda b, c: (0, 0))],
        out_specs=[pl.BlockSpec((nbt, C, RET_W), lambda b, c: (b, c, 0)),
                   pl.BlockSpec((nbt, RET_HEADS, RET_DK, RET_DK), lambda b, c: (b, 0, 0, 0))],
        out_shape=[jax.ShapeDtypeStruct((B, L, RET_W), BF16),
                   jax.ShapeDtypeStruct((B, RET_HEADS, RET_DK, RET_DK), F32)],
        scratch_shapes=[pltpu.VMEM((nbt, RET_HEADS, RET_DK, RET_DK), F32)],
        compiler_params=_params(("parallel", "arbitrary")),
        name="ret_prompt",
    )(proj, proj, proj, proj, cos, sin, gnw)


def _ret_sample_body(q_ref, k_ref, v_ref, g_ref, cos_ref, sin_ref, gnw_ref, dec_ref, s0_ref, o_ref, s_ref, obuf):
    nb = q_ref.shape[0]
    cos = cos_ref[0:1, :]
    sin = sin_ref[0:1, :]
    gdec = dec_ref[0]
    q = _rope(q_ref[...], cos, sin)
    k = _rope(k_ref[...], cos, sin) * (RET_DK ** -0.5)
    v = v_ref[...]
    qk = jnp.sum(q * k, axis=1, keepdims=True)
    qT = q.T
    kT = k.T
    for b in range(nb):
        s0 = s0_ref[0, b, 0]
        qcol = qT[:, b:b + 1]
        kcol = kT[:, b:b + 1]
        vrow = v[b:b + 1, :]
        cross = gdec * jnp.sum(s0 * qcol, axis=0, keepdims=True)
        s_ref[0, b, 0] = s0 * gdec + kcol * vrow
        obuf[b:b + 1, :] = qk[b:b + 1, :] * vrow + cross
    ro = obuf[...]
    mu = jnp.mean(ro, axis=-1, keepdims=True)
    rc = ro - mu
    var = jnp.mean(rc * rc, axis=-1, keepdims=True)
    o_ref[...] = (rc * lax.rsqrt(var + LN_EPS) * gnw_ref[...] * _silu(g_ref[...])).astype(BF16)


def _state_phases(body, n_in, state_pos, carried):
    def phased(*refs):
        p = pl.program_id(0)

        @pl.when(p == 0)
        def _():
            body(*refs)

        @pl.when(p == 1)
        def _():
            refs[state_pos][...] = jnp.zeros(refs[state_pos].shape, F32)

    if carried is None:
        return phased, 2, [], [], {}

    def carried_body(*refs):
        phased(*refs[:n_in], *refs[n_in + 1:])

    return carried_body, 1, [pl.BlockSpec(memory_space=pl.ANY)], [carried], {n_in: 1}


def _ret_sample(proj2, cos, sin, gnw, state, l, carried):
    nb = proj2.shape[0]
    H = RET_HEADS
    dec = jnp.asarray(np.broadcast_to(np.exp(np.array(RET_LOG_G))[:, None, None], (H, 1, RET_DK)), F32)
    hh = lambda p, h: h * (1 - p) + (H - 1) * p
    blk = lambda k: pl.BlockSpec((nb, RET_DK), lambda p, h: (0, k * H + hh(p, h)))
    in_specs = [blk(0), blk(1), blk(2), blk(3),
                pl.BlockSpec((8, RET_DK), lambda p, h: (0, 0)),
                pl.BlockSpec((8, RET_DK), lambda p, h: (0, 0)),
                pl.BlockSpec((1, RET_DK), lambda p, h: (0, hh(p, h))),
                pl.BlockSpec((1, 1, RET_DK), lambda p, h: (hh(p, h), 0, 0)),
                pl.BlockSpec((1, nb, 1, RET_DK, RET_DK), lambda p, h: (l, 0, hh(p, h), 0, 0))]
    body, phases, extra_specs, extra_args, aliases = _state_phases(_ret_sample_body, len(in_specs),
                                                                   len(in_specs) + 1, carried)
    return pl.pallas_call(
        body,
        grid=(phases, H),
        in_specs=in_specs + extra_specs,
        out_specs=[pl.BlockSpec((nb, RET_DK), lambda p, h: (0, hh(p, h))),
                   pl.BlockSpec((1, nb, 1, RET_DK, RET_DK), lambda p, h: (l + p, 0, h, 0, 0))],
        out_shape=[jax.ShapeDtypeStruct((nb, RET_W), BF16),
                   jax.ShapeDtypeStruct(state.shape, F32)],
        scratch_shapes=[pltpu.VMEM((nb, RET_DK), F32)],
        input_output_aliases=aliases,
        compiler_params=_params(("arbitrary", "arbitrary")),
        name="ret_sample",
    )(proj2, proj2, proj2, proj2, cos, sin, gnw, dec, state, *extra_args)


def _gdn_prompt_body(dq_ref, dk_ref, dv_ref, dz_ref, ab_ref, cw_ref, alog_ref, dtb_ref, nw_ref,
                     o_ref, s_ref, conv_ref, xbuf, S):
    c = pl.program_id(1)
    C = GDN_CHUNK
    H = GDN_HEADS
    dk = GDN_DK
    nbt = dq_ref.shape[0]

    @pl.when(c == 0)
    def _():
        S[...] = jnp.zeros_like(S)
        xbuf[:, 0:8, :] = jnp.zeros((nbt, 8, GDN_CONV_C), F32)

    row = lax.broadcasted_iota(jnp.int32, (C, dk), 0)
    lane = lax.broadcasted_iota(jnp.int32, (C, dk), 1)
    tri = lane <= row
    strict = lane < row
    r3 =lax.broadcasted_iota(jnp.int32, (C, 3 * C), 0)
    c3 = lax.broadcasted_iota(jnp.int32, (C, 3 * C), 1) % C
    tril_b = jnp.where(r3 >= c3, 1.0, 0.0).astype(BF16)

    def cumsum_rows(x):
        hi = x.astype(BF16)
        r1 = x - hi.astype(F32)
        mid = r1.astype(BF16)
        lo = (r1 - mid.astype(F32)).astype(BF16)
        return jnp.dot(tril_b, jnp.concatenate([hi, mid, lo], axis=0), preferred_element_type=F32)

    def lanes(x, j):
        return jnp.broadcast_to(x[:, j:j + 1], (x.shape[0], dk))

    cw = cw_ref[...]
    zrow_b = jnp.zeros((C, dk), BF16)

    items = []
    for bt in range(nbt):
        xbuf[bt, 8:8 + C, 0:GDN_W] = dq_ref[bt]
        xbuf[bt, 8:8 + C, GDN_W:2 * GDN_W] = dk_ref[bt]
        xbuf[bt, 8:8 + C, 2 * GDN_W:3 * GDN_W] = dv_ref[bt]

        def conv(off, bt=bt):
            y = xbuf[bt, 5:5 + C, off:off + dk] * cw[0:1, off:off + dk]
            for i in range(1, CONV_W):
                y = y + xbuf[bt, 5 + i:5 + i + C, off:off + dk] * cw[i:i + 1, off:off + dk]
            return _silu(y)

        ab = ab_ref[bt]
        g_c = -jnp.exp(alog_ref[...]) * jax.nn.softplus(ab + dtb_ref[...])
        gc_c = cumsum_rows(g_c)
        beta_c = jax.nn.sigmoid(ab)
        egc_c = jnp.exp(gc_c)
        gl_c = gc_c[C - 1:C, :]
        kdec_c = jnp.exp(gl_c - gc_c)
        egl_c = jnp.exp(gl_c)
        gc_t = jnp.concatenate([gc_c, jnp.zeros_like(gc_c)], axis=0).T

        for h in range(H):
            sl = slice(h * dk, (h + 1) * dk)
            q = conv(h * dk)
            k = conv(GDN_W + h * dk)
            v = conv(2 * GDN_W + h * dk)
            q = q * lax.rsqrt(jnp.sum(q * q, axis=-1, keepdims=True) + L2_EPS) * (dk ** -0.5)
            k = k * lax.rsqrt(jnp.sum(k * k, axis=-1, keepdims=True) + L2_EPS)
            beta = lanes(beta_c, H + h)
            egc = lanes(egc_c, h)
            kb = k * beta
            kq = jnp.concatenate([kb, q], axis=0).astype(BF16)
            k2 = jnp.concatenate([k.astype(BF16), zrow_b], axis=0)
            kkqk = lax.dot_general(kq, k2, NT, preferred_element_type=F32)
            dgc = lanes(gc_c, h) - jnp.broadcast_to(gc_t[h:h + 1, :], (C, dk))
            decay = jnp.exp(jnp.where(tri, dgc, -jnp.inf))
            a = jnp.where(strict, kkqk[0:C] * decay, 0.0)
            attn = (kkqk[C:2 * C] * decay)[:, 0:C].astype(BF16)
            rhs = jnp.concatenate([v * beta, kb * egc], axis=1).astype(BF16)
            items.append(dict(bt=bt, h=h, a=a, rhs=rhs, attn=attn, qd=(q * egc).astype(BF16),
                              kd=(k * lanes(kdec_c, h)).astype(BF16), egl=lanes(egl_c, h)))

    pair_col = lane % C
    eye_pair = (pair_col == row).astype(F32)
    left = lane < C

    def level_mask(s):
        br = row // s
        return (pair_col // s) == jnp.where(br % 2 == 1, br - 1, -1)

    def block_diag(x):
        zero = jnp.zeros_like(x)
        return jnp.concatenate([jnp.where(left, x, zero), jnp.where(left, zero, x)], axis=0)

    pairs = []
    for i in range(0, len(items), 2):
        a_pair = items[i]["a"] + pltpu.roll(items[i + 1]["a"], C, 1)
        pairs.append(dict(a=a_pair, d=eye_pair - jnp.where(level_mask(1), a_pair, 0.0)))
    s_blk = 2
    while s_blk < C:
        mask = level_mask(s_blk)
        for p in pairs:
            d = p["d"]
            sub = block_diag(jnp.where(mask, p["a"], 0.0)).astype(BF16)
            g = jnp.dot(d.astype(BF16), sub, preferred_element_type=F32)
            p["d"] = d - jnp.dot(g.astype(BF16), block_diag(d).astype(BF16), preferred_element_type=F32)
        s_blk *= 2
    for i, p in enumerate(pairs):
        r0, r1 = items[2 * i]["rhs"], items[2 * i + 1]["rhs"]
        zero = jnp.zeros_like(r0)
        rr = jnp.concatenate([jnp.concatenate([r0, zero], axis=1), jnp.concatenate([zero, r1], axis=1)], axis=0)
        sol = jnp.dot(p["d"].astype(BF16), rr, preferred_element_type=F32)
        items[2 * i]["sol"] = sol[:, 0:2 * dk]
        items[2 * i + 1]["sol"] = sol[:, 2 * dk:4 * dk]

    for it in items:
        bt, h = it["bt"], it["h"]
        sl = slice(h * dk, (h + 1) * dk)
        u = it["sol"][:, 0:dk]
        w = it["sol"][:, dk:2 * dk]
        s_prev = S[bt, h]
        wq = jnp.concatenate([w.astype(BF16), it["qd"]], axis=0)
        r = jnp.dot(wq, s_prev.astype(BF16), preferred_element_type=F32)
        v_new = u - r[0:C]
        o = r[C:2 * C] + jnp.dot(it["attn"], v_new.astype(BF16), preferred_element_type=F32)
        S[bt, h] = s_prev * it["egl"] + lax.dot_general(it["kd"], v_new.astype(BF16), TN, preferred_element_type=F32)
        o = o * lax.rsqrt(jnp.mean(o * o, axis=-1, keepdims=True) + RMS_EPS) * nw_ref[...]
        o_ref[bt, :, sl] = (o * _silu(dz_ref[bt, :, sl])).astype(BF16)

    for bt in range(nbt):
        xbuf[bt, 0:8, :] = xbuf[bt, C:C + 8, :]

    @pl.when(c == pl.num_programs(1) - 1)
    def _():
        s_ref[...] = S[...]
        conv_ref[...] = xbuf[:, C + 8 - (CONV_W - 1):C + 8, :]


def _gdn_prompt(proj, cw, alog_bc, dtb_bc, nw, nbt):
    B, L, _ = proj.shape
    C = GDN_CHUNK
    H = GDN_HEADS
    blk = lambda k: pl.BlockSpec((nbt, C, GDN_W), lambda b, c: (b, c, k))
    full = lambda shape: pl.BlockSpec(shape, lambda b, c: tuple(0 for _ in shape))
    return pl.pallas_call(
        _gdn_prompt_body,
        grid=(B // nbt, L // C),
        in_specs=[blk(2), blk(3), blk(4), blk(5),
                  pl.BlockSpec((nbt, C, 128), lambda b, c: (b, c, COL_AB // 128)),
                  full((CONV_W, GDN_CONV_C)), full((1, 128)), full((1, 128)), full((1, GDN_DK))],
        out_specs=[pl.BlockSpec((nbt, C, GDN_W), lambda b, c: (b, c, 0)),
                   pl.BlockSpec((nbt, H, GDN_DK, GDN_DK), lambda b, c: (b, 0, 0, 0)),
                   pl.BlockSpec((nbt, CONV_W - 1, GDN_CONV_C), lambda b, c: (b, 0, 0))],
        out_shape=[jax.ShapeDtypeStruct((B, L, GDN_W), BF16),
                   jax.ShapeDtypeStruct((B, H, GDN_DK, GDN_DK), F32),
                   jax.ShapeDtypeStruct((B, CONV_W - 1, GDN_CONV_C), F32)],
        scratch_shapes=[pltpu.VMEM((nbt, C + 8, GDN_CONV_C), F32),
                        pltpu.VMEM((nbt, H, GDN_DK, GDN_DK), F32)],
        compiler_params=_params(("parallel", "arbitrary")),
        name="gdn_prompt",
    )(proj, proj, proj, proj, proj, cw, alog_bc, dtb_bc, nw)


def _gdn_sample_body(dq_ref, dk_ref, dv_ref, dz_ref, ab_ref, csq_ref, csk_ref, csv_ref, cwq_ref, cwk_ref, cwv_ref,
                     alog_ref, dtb_ref, nw_ref, s0_ref, o_ref, s_ref, nq_ref, nk_ref, nv_ref, obuf):
    h = pl.program_id(1)
    nb = dq_ref.shape[0]
    dk = GDN_DK

    def conv(x_ref, cs_ref, w_ref, n_ref):
        x = x_ref[...]
        y = cs_ref[0] * w_ref[0:1, :] + cs_ref[1] * w_ref[1:2, :] + cs_ref[2] * w_ref[2:3, :] + x * w_ref[3:4, :]
        n_ref[0] = cs_ref[1]
        n_ref[1] = cs_ref[2]
        n_ref[2] = x
        return _silu(y)

    q = conv(dq_ref, csq_ref, cwq_ref, nq_ref)
    k = conv(dk_ref, csk_ref, cwk_ref, nk_ref)
    v = conv(dv_ref, csv_ref, cwv_ref, nv_ref)
    q = q * lax.rsqrt(jnp.sum(q * q, axis=-1, keepdims=True) + L2_EPS) * (dk ** -0.5)
    k = k * lax.rsqrt(jnp.sum(k * k, axis=-1, keepdims=True) + L2_EPS)
    er = lax.broadcasted_iota(jnp.int32, (128, dk), 0)
    ab = ab_ref[...]
    a_bc = jnp.dot(ab, (er == h).astype(F32), precision=HI, preferred_element_type=F32)
    b_bc = jnp.dot(ab, (er == h + GDN_HEADS).astype(F32), precision=HI, preferred_element_type=F32)
    g = -jnp.exp(alog_ref[0]) * jax.nn.softplus(a_bc + dtb_ref[0])
    eg = jnp.exp(g)
    beta = jax.nn.sigmoid(b_bc)
    qk = jnp.sum(q * k, axis=1, keepdims=True)
    qT = q.T
    kT = k.T
    for b in range(nb):
        s0 = s0_ref[0, b, 0]
        qcol = qT[:, b:b + 1]
        kcol = kT[:, b:b + 1]
        ks = jnp.sum(s0 * kcol, axis=0, keepdims=True)
        qs = jnp.sum(s0 * qcol, axis=0, keepdims=True)
        bb = beta[b:b + 1, :]
        egb = eg[b:b + 1, :]
        v_new = bb * v[b:b + 1, :] - bb * egb * ks
        s_ref[0, b, 0] = s0 * egb + kcol * v_new
        obuf[b:b + 1, :] = egb * qs + qk[b:b + 1, :] * v_new
    o = obuf[...]
    o = o * lax.rsqrt(jnp.mean(o * o, axis=-1, keepdims=True) + RMS_EPS) * nw_ref[...]
    o_ref[...] = (o * _silu(dz_ref[...])).astype(BF16)


def _gdn_sample(proj2, cs_t, cw, alog_t, dtb_t, nw, state, l, carried):
    nb = proj2.shape[0]
    H = GDN_HEADS
    dk = GDN_DK
    hh = lambda p, h: h * (1 - p) + (H - 1) * p
    blk = lambda k: pl.BlockSpec((nb, dk), lambda p, h: (0, k * H + hh(p, h)))
    csblk = lambda k: pl.BlockSpec((CONV_W - 1, nb, dk), lambda p, h: (0, 0, k * H + hh(p, h)))
    cwblk = lambda k: pl.BlockSpec((CONV_W, dk), lambda p, h: (0, k * H + hh(p, h)))
    tab = pl.BlockSpec((1, 1, dk), lambda p, h: (hh(p, h), 0, 0))
    nblk = pl.BlockSpec((CONV_W - 1, nb, dk), lambda p, h: (0, 0, hh(p, h)))
    nshape = jax.ShapeDtypeStruct((CONV_W - 1, nb, GDN_W), F32)
    in_specs = [blk(2), blk(3), blk(4), blk(5),
                pl.BlockSpec((nb, 128), lambda p, h: (0, COL_AB // 128)),
                csblk(0), csblk(1), csblk(2), cwblk(0), cwblk(1), cwblk(2),
                tab, tab, pl.BlockSpec((1, dk), lambda p, h: (0, 0)),
                pl.BlockSpec((1, nb, 1, dk, dk), lambda p, h: (l, 0, hh(p, h), 0, 0))]
    body, phases, extra_specs, extra_args, aliases = _state_phases(_gdn_sample_body, len(in_specs),
                                                                   len(in_specs) + 1, carried)
    return pl.pallas_call(
        body,
        grid=(phases, H),
        in_specs=in_specs + extra_specs,
        out_specs=[pl.BlockSpec((nb, dk), lambda p, h: (0, hh(p, h))),
                   pl.BlockSpec((1, nb, 1, dk, dk), lambda p, h: (l + p, 0, h, 0, 0)), nblk, nblk, nblk],
        out_shape=[jax.ShapeDtypeStruct((nb, GDN_W), BF16), jax.ShapeDtypeStruct(state.shape, F32),
                   nshape, nshape, nshape],
        scratch_shapes=[pltpu.VMEM((nb, dk), F32)],
        input_output_aliases=aliases,
        compiler_params=_params(("arbitrary", "arbitrary")),
        name="gdn_sample",
    )(proj2, proj2, proj2, proj2, proj2, cs_t, cs_t, cs_t, cw, cw, cw, alog_t, dtb_t, nw, state, *extra_args)


def _ssm_prep_body(are_ref, aim_ref, ldt_ref, bre_ref, bim_ref, lre_ref, lim_ref, obre_ref, obim_ref):
    ar = are_ref[0]
    ai = aim_ref[0]
    dt = jnp.exp(ldt_ref[0])
    er = jnp.exp(ar * dt)
    lr = er * jnp.cos(ai * dt)
    li = er * jnp.sin(ai * dt)
    lre_ref[0] = lr
    lim_ref[0] = li
    xr = lr - 1.0
    den = ar * ar + ai * ai
    cr = ((xr * ar + li * ai) / den)[:, None, :]
    ci = ((li * ar - xr * ai) / den)[:, None, :]
    br = bre_ref[0]
    bi = bim_ref[0]
    obre_ref[0] = cr * br - ci * bi
    obim_ref[0] = cr * bi + ci * br


def _ssm_prep(a_re, a_im, log_dt, b_re, b_im):
    G, N, P = SSM_GROUPS, SSM_N, SSM_GROUP
    ldt = jnp.broadcast_to(log_dt[:, :, None], (DEPTH, G, N))
    bt_re = jnp.transpose(b_re, (0, 1, 3, 2))
    bt_im = jnp.transpose(b_im, (0, 1, 3, 2))
    s2 = pl.BlockSpec((1, G, N), lambda l: (l, 0, 0))
    s3 = pl.BlockSpec((1, G, P, N), lambda l: (l, 0, 0, 0))
    return pl.pallas_call(
        _ssm_prep_body,
        grid=(DEPTH,),
        in_specs=[s2, s2, s2, s3, s3],
        out_specs=[s2, s2, s3, s3],
        out_shape=[jax.ShapeDtypeStruct((DEPTH, G, N), F32), jax.ShapeDtypeStruct((DEPTH, G, N), F32),
                   jax.ShapeDtypeStruct((DEPTH, G, P, N), F32), jax.ShapeDtypeStruct((DEPTH, G, P, N), F32)],
        name="ssm_prep",
    )(a_re, a_im, ldt, bt_re, bt_im)


def _ssm_tile_blockdiag(re, im):
    x = jnp.stack([re, im]).reshape(2, SSM_TILES, SSM_TILE_G, SSM_GROUP, SSM_N)
    eye = jnp.eye(SSM_TILE_G, dtype=x.dtype)
    y = x[:, :, :, :, None, :] * eye[None, None, :, None, :, None]
    y = jnp.transpose(y, (1, 2, 3, 0, 4, 5))
    return y.reshape(SSM_TILES, SSM_TILE_G * SSM_GROUP, SSM_TILE_W)


def _ssm_state_to_tiles(re, im):
    B = re.shape[0]
    x = jnp.stack([re, im], axis=1).reshape(B, 2, SSM_TILES, SSM_HALF)
    return jnp.transpose(x, (0, 2, 1, 3)).reshape(B, SSM_TILES * SSM_TILE_W)


def _ssm_tiles_to_state(x):
    B = x.shape[0]
    x = jnp.transpose(x.reshape(B, SSM_TILES, 2, SSM_HALF), (0, 2, 1, 3)).reshape(B, 2, SSM_GROUPS, SSM_N)
    return x[:, 0], x[:, 1]


def _cmul(ar, ai, br, bi):
    return ar * br - ai * bi, ar * bi + ai * br


def _s5_prompt_body(u_ref, bt_ref, ct_ref, lam_ref, d_ref, y_ref, hl_ref, buf):
    L = u_ref.shape[1]
    nseg = L // SSM_SEG
    ns = SSM_HALF // 128
    bmat = bt_ref[0].astype(BF16)
    for i in range(nseg):
        bu = jnp.dot(u_ref[0, i * SSM_SEG:(i + 1) * SSM_SEG, :].astype(BF16), bmat, preferred_element_type=F32)
        for s in range(2 * ns):
            buf[s, i * SSM_PITCH:i * SSM_PITCH + SSM_SEG, :] = bu[:, s * 128:(s + 1) * 128]

    lam = lam_ref[0]
    a_re = [jnp.broadcast_to(lam[:, s * 128:(s + 1) * 128], (nseg, 128)) for s in range(ns)]
    a_im = [jnp.broadcast_to(lam[:, SSM_HALF + s * 128:SSM_HALF + (s + 1) * 128], (nseg, 128)) for s in range(ns)]

    def scan_step(t, carry):
        out = []
        for s in range(ns):
            hr, hi = carry[2 * s], carry[2 * s + 1]
            pr, pi = _cmul(a_re[s], a_im[s], hr, hi)
            hr = pr + buf[s, pl.ds(t, nseg, stride=SSM_PITCH), :]
            hi = pi + buf[ns + s, pl.ds(t, nseg, stride=SSM_PITCH), :]
            buf[s, pl.ds(t, nseg, stride=SSM_PITCH), :] = hr
            buf[ns + s, pl.ds(t, nseg, stride=SSM_PITCH), :] = hi
            out += [hr, hi]
        return tuple(out)

    zero = jnp.zeros((nseg, 128), F32)
    ends = lax.fori_loop(0, SSM_SEG, scan_step, tuple(zero for _ in range(2 * ns)), unroll=SSM_UNROLL)

    sub = lax.broadcasted_iota(jnp.int32, (nseg, 128), 0)
    carries = []
    for s in range(ns):
        pr, pi = a_re[s], a_im[s]
        for _ in range(int(math.log2(SSM_SEG))):
            pr, pi = _cmul(pr, pi, pr, pi)
        xr, xi = ends[2 * s], ends[2 * s + 1]
        shift = 1
        while shift < nseg:
            sr = jnp.where(sub >= shift, pltpu.roll(xr, shift, 0), 0.0)
            si = jnp.where(sub >= shift, pltpu.roll(xi, shift, 0), 0.0)
            mr, mi = _cmul(pr, pi, sr, si)
            xr, xi = xr + mr, xi + mi
            pr, pi = _cmul(pr, pi, pr, pi)
            shift *= 2
        hl_ref[0, 0, :, s * 128:(s + 1) * 128] = xr[nseg - 1:nseg, :]
        hl_ref[0, 0, :, SSM_HALF + s * 128:SSM_HALF + (s + 1) * 128] = xi[nseg - 1:nseg, :]
        carries += [jnp.where(sub >= 1, pltpu.roll(xr, 1, 0), 0.0), jnp.where(sub >= 1, pltpu.roll(xi, 1, 0), 0.0)]

    def fix_step(t, pw):
        out = []
        for s in range(ns):
            pr, pi = pw[2 * s], pw[2 * s + 1]
            fr, fi = _cmul(pr, pi, carries[2 * s], carries[2 * s + 1])
            buf[s, pl.ds(t, nseg, stride=SSM_PITCH), :] += fr
            buf[ns + s, pl.ds(t, nseg, stride=SSM_PITCH), :] += fi
            nr, ni = _cmul(pr, pi, a_re[s], a_im[s])
            out += [nr, ni]
        return tuple(out)

    pw0 = []
    for s in range(ns):
        pw0 += [a_re[s], a_im[s]]
    lax.fori_loop(0, SSM_SEG, fix_step, tuple(pw0), unroll=SSM_UNROLL)

    cmat = ct_ref[0]
    cmat = jnp.concatenate([cmat[:, 0:SSM_HALF], -cmat[:, SSM_HALF:]], axis=1).astype(BF16)
    for i in range(nseg):
        rows = slice(i * SSM_PITCH, i * SSM_PITCH + SSM_SEG)
        h = jnp.concatenate([buf[s, rows, :].astype(BF16) for s in range(2 * ns)], axis=1)
        y = lax.dot_general(h, cmat, NT, preferred_element_type=F32)
        u = u_ref[0, i * SSM_SEG:(i + 1) * SSM_SEG, :]
        y_ref[0, i * SSM_SEG:(i + 1) * SSM_SEG, :] = jax.nn.gelu(y + d_ref[...] * u)


def _s5_prompt(proj, bt, ct, lam, d):
    B, L, _ = proj.shape
    nseg = L // SSM_SEG
    tile = lambda shape: pl.BlockSpec(shape, lambda b, j: (j,) + tuple(0 for _ in shape[1:]))
    return pl.pallas_call(
        _s5_prompt_body,
        grid=(B, SSM_TILES),
        in_specs=[pl.BlockSpec((1, L, 128), lambda b, j: (b, 0, COL_SU // 128 + j)),
                  tile((1, 128, SSM_TILE_W)), tile((1, 128, SSM_TILE_W)), tile((1, 1, SSM_TILE_W)),
                  pl.BlockSpec((1, 128), lambda b, j: (0, j))],
        out_specs=[pl.BlockSpec((1, L, 128), lambda b, j: (b, 0, j)),
                   pl.BlockSpec((1, 1, 1, SSM_TILE_W), lambda b, j: (b, j, 0, 0))],
        out_shape=[jax.ShapeDtypeStruct((B, L, SSM_W), F32),
                   jax.ShapeDtypeStruct((B, SSM_TILES, 1, SSM_TILE_W), F32)],
        scratch_shapes=[pltpu.VMEM((SSM_TILE_W // 128, nseg * SSM_PITCH, 128), F32)],
        compiler_params=_params(("parallel", "parallel")),
        name="s5_prompt",
    )(proj, bt, ct, lam, d)


def _s5_sample_body(u_ref, bt_ref, ct_ref, lam_ref, d_ref, h0_ref, y_ref, h_ref):
    u = u_ref[...]
    bu = jnp.dot(u, bt_ref[0], precision=HI, preferred_element_type=F32)
    lam = lam_ref[0]
    lr, li = lam[:, 0:SSM_HALF], lam[:, SSM_HALF:]
    h0r, h0i = h0_ref[:, 0:SSM_HALF], h0_ref[:, SSM_HALF:]
    pr, pi = _cmul(lr, li, h0r, h0i)
    hr = pr + bu[:, 0:SSM_HALF]
    hi = pi + bu[:, SSM_HALF:]
    h_ref[:, 0:SSM_HALF] = hr
    h_ref[:, SSM_HALF:] = hi
    ct = ct_ref[0]
    y = lax.dot_general(hr, ct[:, 0:SSM_HALF], NT, precision=HI, preferred_element_type=F32)
    y -= lax.dot_general(hi, ct[:, SSM_HALF:], NT, precision=HI, preferred_element_type=F32)
    y_ref[...] = jax.nn.gelu(y + d_ref[...] * u)


def _s5_sample(proj2, bt, ct, lam, d, h0):
    nb = proj2.shape[0]
    tile = lambda shape: pl.BlockSpec(shape, lambda j: (j,) + tuple(0 for _ in shape[1:]))
    hblk = pl.BlockSpec((nb, SSM_TILE_W), lambda j: (0, j))
    return pl.pallas_call(
        _s5_sample_body,
        grid=(SSM_TILES,),
        in_specs=[pl.BlockSpec((nb, 128), lambda j: (0, COL_SU // 128 + j)),
                  tile((1, 128, SSM_TILE_W)), tile((1, 128, SSM_TILE_W)), tile((1, 1, SSM_TILE_W)),
                  pl.BlockSpec((1, 128), lambda j: (0, j)), hblk],
        out_specs=[pl.BlockSpec((nb, 128), lambda j: (0, j)), hblk],
        out_shape=[jax.ShapeDtypeStruct((nb, SSM_W), F32), jax.ShapeDtypeStruct(h0.shape, F32)],
        compiler_params=_params(("parallel",)),
        name="s5_sample",
    )(proj2, bt, ct, lam, d, h0)


def kernel(x_prompt, x_sample, c_prompt, c_sample, state_ret, state_gdn, state_conv, state_ssm_re, state_ssm_im, w_mod, b_mod, w_in, conv_w, ret_gn_w, gdn_a_log, gdn_dt_bias, gdn_norm_w, ssm_a_re, ssm_a_im, ssm_log_dt, ssm_b_re, ssm_b_im, ssm_c_re, ssm_c_im, ssm_d, ssm_w_glu, w_out, ln1_w, ln1_b, w_ffn_in, w_ffn_out, ln2_w, ln2_b):
    Bp, Lp, _ = x_prompt.shape
    Bs = x_sample.shape[0]
    assert x_sample.shape[1] == 1 and Bs % 8 == 0

    pad_rows = (-(Bp + Bs)) % 8
    c_all = jnp.concatenate([c_prompt, c_sample, jnp.zeros((pad_rows, D_MODEL), F32)], axis=0)
    mod_all = _modulation(c_all, w_mod, b_mod)
    lam_re, lam_im, bbar_re, bbar_im = _ssm_prep(ssm_a_re, ssm_a_im, ssm_log_dt, ssm_b_re, ssm_b_im)
    cos_p, sin_p, cos_s, sin_s = _rope_tables(Lp)

    w_in_b = jnp.transpose(w_in[:, :, :COL_SU].reshape(DEPTH, D_MODEL, COL_SU // PROJ_TN, PROJ_TN),
                           (0, 2, 1, 3)).astype(BF16)
    w_in_tail = w_in[:, :, COL_SU:].astype(BF16)
    w_out_b = _to_bf16(w_out, rows=512)
    w_glu_b = ssm_w_glu.astype(BF16)
    w_ffi_b = _to_bf16_col_tiles(w_ffn_in, rows=128, tc=FFN_TF)
    w_ffo_b = _to_bf16(w_ffn_out, rows=512)

    yp = x_prompt
    ys = x_sample.reshape(1, Bs, D_MODEL)
    outs_p, outs_s = [], []
    ret_s = gdn_s = None
    for l in range(DEPTH):
        w_tail = jnp.concatenate(
            [w_in_tail[l, :, 2 * GDN_HEADS:], w_in_tail[l, :, :2 * GDN_HEADS],
             jnp.zeros((D_MODEL, N_PROJ - N_IN), BF16)], axis=1)
        mod_p = mod_all[l, :Bp][:, None, :]
        mod_s = mod_all[l, Bp:Bp + Bs][None]
        gnw = ret_gn_w[l][None, :]
        nw = gdn_norm_w[l][None, :]
        alog_bc = jnp.pad(gdn_a_log[l], (0, 128 - GDN_HEADS))[None, :]
        dtb_bc = jnp.pad(gdn_dt_bias[l], (0, 128 - GDN_HEADS))[None, :]
        alog_t = jnp.broadcast_to(gdn_a_log[l][:, None, None], (GDN_HEADS, 1, GDN_DK))
        dtb_t = jnp.broadcast_to(gdn_dt_bias[l][:, None, None], (GDN_HEADS, 1, GDN_DK))
        ssm_bt = _ssm_tile_blockdiag(bbar_re[l], bbar_im[l])
        ssm_ct = _ssm_tile_blockdiag(ssm_c_re[l], ssm_c_im[l])
        ssm_lam = jnp.concatenate([lam_re[l].reshape(SSM_TILES, 1, SSM_HALF),
                                   lam_im[l].reshape(SSM_TILES, 1, SSM_HALF)], axis=-1)
        ssm_dl = ssm_d[l][None, :]
        lnw1, lnb1 = ln1_w[l][None, :], ln1_b[l][None, :]
        lnw2, lnb2 = ln2_w[l][None, :], ln2_b[l][None, :]

        proj = _inproj(yp, mod_p, w_in_b, l, w_tail, tm=1024, tn=PROJ_TN)
        ro, ret_p = _ret_prompt(proj, cos_p, sin_p, gnw, nbt=4)
        go, gdn_p, conv_p = _gdn_prompt(proj, conv_w[l], alog_bc, dtb_bc, nw, nbt=4)
        so, hl = _s5_prompt(proj, ssm_bt, ssm_ct, ssm_lam, ssm_dl)
        re_p, im_p = _ssm_tiles_to_state(hl.reshape(Bp, SSM_TILES * SSM_TILE_W))
        x1 = _outproj(ro, go, so, yp, mod_p, w_glu_b, w_out_b, l, lnw1, lnb1, tm=512)
        yp = _ffn(x1, mod_p, w_ffi_b, w_ffo_b, l, lnw2, lnb2, tm=1024, tf=FFN_TF)
        outs_p.append((ret_p, gdn_p, conv_p, re_p, im_p))

        proj = _inproj(ys, mod_s, w_in_b, l, w_tail, tm=Bs, tn=PROJ_TN)
        proj2 = proj.reshape(Bs, N_PROJ)
        ro, ret_s = _ret_sample(proj2, cos_s, sin_s, gnw, state_ret, l, ret_s)
        cs_t = jnp.transpose(state_conv[l], (1, 0, 2))
        go, gdn_s, nq, nk, nv = _gdn_sample(proj2, cs_t, conv_w[l], alog_t, dtb_t, nw, state_gdn, l, gdn_s)
        conv_s = jnp.transpose(jnp.concatenate([nq, nk, nv], axis=-1), (1, 0, 2))
        h0 = _ssm_state_to_tiles(state_ssm_re[l], state_ssm_im[l])
        so, hn = _s5_sample(proj2, ssm_bt, ssm_ct, ssm_lam, ssm_dl, h0)
        re_s, im_s = _ssm_tiles_to_state(hn)
        x1 = _outproj(ro[None], go[None], so[None], ys, mod_s, w_glu_b, w_out_b, l, lnw1, lnb1, tm=Bs)
        ys = _ffn(x1, mod_s, w_ffi_b, w_ffo_b, l, lnw2, lnb2, tm=Bs, tf=FFN_TF)
        outs_s.append((conv_s, re_s, im_s))

    ret_p, gdn_p, conv_p, re_p, im_p = [jnp.stack(t) for t in zip(*outs_p)]
    conv_s, re_s, im_s = [jnp.stack(t) for t in zip(*outs_s)]
    return (yp, ys.reshape(Bs, 1, D_MODEL), ret_p, gdn_p, conv_p, re_p, im_p,
            ret_s, gdn_s, conv_s, re_s, im_s)
```

```python
import functools
import math

import numpy as np
import jax
import jax.numpy as jnp
from jax import lax
from jax.experimental import pallas as pl
from jax.experimental.pallas import tpu as pltpu

F32 = jnp.float32
BF16 = jnp.bfloat16

D_MODEL = 2048
DEPTH = 2
PAST_LEN = 16384
RET_HEADS = 4
RET_DK = 128
RET_W = 512
RET_CHUNK = 128
ROPE_BASE = 10000.0
GDN_HEADS = 8
GDN_DK = 128
GDN_W = 1024
GDN_CHUNK = 64
CONV_W = 4
GDN_CONV_C = 3072
SSM_W = 512
SSM_GROUP = 16
SSM_GROUPS = 32
SSM_N = 64
D_FF = 5632
ALPHA = (2 * DEPTH) ** 0.25
LN_EPS = 1e-5
RMS_EPS = 1e-6
L2_EPS = 1e-6

COL_SU = 6144
COL_AB = 6656
N_IN = 6672
PROJ_TN = 768
N_PROJ = COL_SU + PROJ_TN

SSM_TILES = 4
SSM_TILE_G = 8
SSM_HALF = SSM_TILE_G * SSM_N
SSM_TILE_W = 2 * SSM_HALF
SSM_SEG = 256
SSM_PITCH = 260
SSM_UNROLL = 4

VMEM_LIMIT = 56 * 1024 * 1024
OUTPROJ_ROW_CHUNK = 256
FFN_ROW_CHUNK = 256
FFN_TF = 512

RET_LOG_G = [math.log1p(-(2.0 ** (-5.0 - h))) for h in range(RET_HEADS)]

NT = (((1,), (1,)), ((), ()))
TN = (((0,), (0,)), ((), ()))
HI = lax.Precision.HIGHEST


def _params(sem):
    return pltpu.CompilerParams(dimension_semantics=sem, vmem_limit_bytes=VMEM_LIMIT)


def _silu(x):
    return x * jax.nn.sigmoid(x)


def _bdot(a, b):
    return jnp.dot(a.astype(BF16), b.astype(BF16), preferred_element_type=F32)


def _hdot(a, b):
    return jnp.dot(a, b, precision=HI, preferred_element_type=F32)


def _bdot_nt(a, b):
    return lax.dot_general(a.astype(BF16), b.astype(BF16), NT, preferred_element_type=F32)


def _bdot_tn(a, b):
    return lax.dot_general(a.astype(BF16), b.astype(BF16), TN, preferred_element_type=F32)


def _layernorm(z, w, b):
    mu = jnp.mean(z, axis=-1, keepdims=True)
    zc = z - mu
    var = jnp.mean(zc * zc, axis=-1, keepdims=True)
    return zc * lax.rsqrt(var + LN_EPS) * w + b


def _cast_body(x_ref, o_ref):
    o_ref[...] = x_ref[...].astype(o_ref.dtype)


def _to_bf16(w, rows):
    depth, r, n = w.shape
    spec = pl.BlockSpec((1, rows, n), lambda l, i: (l, i, 0))
    return pl.pallas_call(
        _cast_body,
        grid=(depth, r // rows),
        in_specs=[spec],
        out_specs=spec,
        out_shape=jax.ShapeDtypeStruct(w.shape, BF16),
        compiler_params=_params(("parallel", "parallel")),
        name="to_bf16",
    )(w)


def _cast_col_tiles_body(x_ref, o_ref):
    tc = o_ref.shape[3]
    for t in range(o_ref.shape[1]):
        o_ref[0, t] = x_ref[0, :, t * tc:(t + 1) * tc].astype(o_ref.dtype)


def _to_bf16_col_tiles(w, rows, tc):
    depth, r, n = w.shape
    nt = n // tc
    return pl.pallas_call(
        _cast_col_tiles_body,
        grid=(depth, r // rows),
        in_specs=[pl.BlockSpec((1, rows, n), lambda l, i: (l, i, 0))],
        out_specs=pl.BlockSpec((1, nt, rows, tc), lambda l, i: (l, 0, i, 0)),
        out_shape=jax.ShapeDtypeStruct((depth, nt, r, tc), BF16),
        compiler_params=_params(("parallel", "parallel")),
        name="to_bf16_col_tiles",
    )(w)


def _mod_body(c_ref, w_ref, b_ref, o_ref):
    c = c_ref[...]
    o_ref[0] = _bdot(_silu(c), w_ref[0]) + b_ref[0]


def _modulation(c_all, w_mod, b_mod):
    rows = c_all.shape[0]
    tn = 1024
    n = 6 * D_MODEL
    return pl.pallas_call(
        _mod_body,
        grid=(DEPTH, n // tn),
        in_specs=[pl.BlockSpec((rows, D_MODEL), lambda l, j: (0, 0)),
                  pl.BlockSpec((1, D_MODEL, tn), lambda l, j: (l, 0, j)),
                  pl.BlockSpec((1, 1, tn), lambda l, j: (l, 0, j))],
        out_specs=pl.BlockSpec((1, rows, tn), lambda l, j: (l, 0, j)),
        out_shape=jax.ShapeDtypeStruct((DEPTH, rows, n), F32),
        compiler_params=_params(("parallel", "parallel")),
        name="modulation",
    )(c_all, w_mod, b_mod.reshape(DEPTH, 1, n))


def _mod_spec(mod3, tm, k, grid_rank):
    per_token = mod3.shape[1] != 1
    lmb = tm if per_token else 1
    if grid_rank == 3:
        return pl.BlockSpec((1, lmb, D_MODEL), lambda b, i, j: (b, i if per_token else 0, k))
    return pl.BlockSpec((1, lmb, D_MODEL), lambda b, i: (b, i if per_token else 0, k))


def _inproj_body(x_ref, sc_ref, sh_ref, w_ref, wt_ref, o_ref, h_ref, *, n_main):
    j = pl.program_id(2)
    tm = x_ref.shape[1]
    per_token = sc_ref.shape[1] != 1

    @pl.when(j == 0)
    def _():
        step = min(tm, FFN_ROW_CHUNK)
        for r in range(0, tm, step):
            rows = slice(r, r + step)
            sc = sc_ref[0, rows, :] if per_token else sc_ref[0]
            sh = sh_ref[0, rows, :] if per_token else sh_ref[0]
            h = (x_ref[0, rows, :] * (1.0 + sc) + sh).astype(BF16)
            h_ref[rows, :] = h
            o_ref[0, rows, :] = jnp.dot(h, w_ref[0], preferred_element_type=F32)

    @pl.when(jnp.logical_and(j > 0, j < n_main))
    def _():
        o_ref[0] = jnp.dot(h_ref[...], w_ref[0], preferred_element_type=F32)

    @pl.when(j == n_main)
    def _():
        o_ref[0] = jnp.dot(h_ref[...], wt_ref[...], preferred_element_type=F32)


def _inproj(x, mod3, w_in, l, w_tail, tm, tn):
    B, L, _ = x.shape
    n_main = COL_SU // tn
    assert n_main * tn == COL_SU and w_tail.shape == (D_MODEL, tn) and (n_main + 1) * tn == N_PROJ
    return pl.pallas_call(
        functools.partial(_inproj_body, n_main=n_main),
        grid=(B, L // tm, n_main + 1),
        in_specs=[pl.BlockSpec((1, tm, D_MODEL), lambda b, i, j: (b, i, 0)),
                  _mod_spec(mod3, tm, 1, 3),
                  _mod_spec(mod3, tm, 0, 3),
                  pl.BlockSpec((1, D_MODEL, tn), lambda b, i, j: (l, 0, jnp.minimum(j, n_main - 1))),
                  pl.BlockSpec((D_MODEL, tn), lambda b, i, j: (0, 0))],
        out_specs=pl.BlockSpec((1, tm, tn), lambda b, i, j: (b, i, j)),
        out_shape=jax.ShapeDtypeStruct((B, L, N_PROJ), F32),
        scratch_shapes=[pltpu.VMEM((tm, D_MODEL), BF16)],
        compiler_params=_params(("parallel", "parallel", "arbitrary")),
        name="inproj",
    )(x, mod3, mod3, w_in, w_tail)


def _outproj_body(ro_ref, go_ref, y_ref, x_ref, g1_ref, wglu_ref, wout_ref, lnw_ref, lnb_ref, o_ref, *, row_chunk):
    tm = x_ref.shape[1]
    for r in range(0, tm, row_chunk):
        rows = slice(r, r + row_chunk)
        y = y_ref[0, rows, :]
        so = y * jax.nn.sigmoid(_bdot(y, wglu_ref[0]))
        acc = jnp.dot(ro_ref[0, rows, :], wout_ref[0, 0:RET_W, :], preferred_element_type=F32)
        acc += jnp.dot(go_ref[0, rows, :], wout_ref[0, RET_W:RET_W + GDN_W, :], preferred_element_type=F32)
        acc += jnp.dot(so.astype(BF16), wout_ref[0, RET_W + GDN_W:, :], preferred_element_type=F32)
        g1 = g1_ref[0] if g1_ref.shape[1] == 1 else g1_ref[0, rows, :]
        z = ALPHA * x_ref[0, rows, :] + g1 * acc
        o_ref[0, rows, :] = _layernorm(z, lnw_ref[...], lnb_ref[...])


def _outproj(ro, go, y, x, mod3, wglu, wout, l, lnw, lnb, tm):
    B, L, _ = x.shape
    return pl.pallas_call(
        functools.partial(_outproj_body, row_chunk=min(tm, OUTPROJ_ROW_CHUNK)),
        grid=(B, L // tm),
        in_specs=[pl.BlockSpec((1, tm, RET_W), lambda b, i: (b, i, 0)),
                  pl.BlockSpec((1, tm, GDN_W), lambda b, i: (b, i, 0)),
                  pl.BlockSpec((1, tm, SSM_W), lambda b, i: (b, i, 0)),
                  pl.BlockSpec((1, tm, D_MODEL), lambda b, i: (b, i, 0)),
                  _mod_spec(mod3, tm, 2, 2),
                  pl.BlockSpec((1, SSM_W, SSM_W), lambda b, i: (l, 0, 0)),
                  pl.BlockSpec((1, D_MODEL, D_MODEL), lambda b, i: (l, 0, 0)),
                  pl.BlockSpec((1, D_MODEL), lambda b, i: (0, 0)),
                  pl.BlockSpec((1, D_MODEL), lambda b, i: (0, 0))],
        out_specs=pl.BlockSpec((1, tm, D_MODEL), lambda b, i: (b, i, 0)),
        out_shape=jax.ShapeDtypeStruct((B, L, D_MODEL), F32),
        compiler_params=_params(("parallel", "parallel")),
        name="outproj_ln1",
    )(ro, go, y, x, mod3, wglu, wout, lnw, lnb)


def _ffn_body(x_ref, sc_ref, sh_ref, g2_ref, wg_ref, wu_ref, wo_ref, lnw_ref, lnb_ref, o_ref, h_ref):
    f = pl.program_id(2)
    last = pl.num_programs(2) - 1
    tm = x_ref.shape[1]
    chunks = [slice(r, r + min(tm, FFN_ROW_CHUNK)) for r in range(0, tm, min(tm, FFN_ROW_CHUNK))]
    per_token = sc_ref.shape[1] != 1

    def mod(ref, rows):
        return ref[0, rows, :] if per_token else ref[0]

    def contribution(h):
        gate = jnp.dot(h, wg_ref[0, 0], preferred_element_type=F32)
        up = jnp.dot(h, wu_ref[0, 0], preferred_element_type=F32)
        act = (_silu(gate) * up).astype(BF16)
        return jnp.dot(act, wo_ref[0], preferred_element_type=F32)

    @pl.when(f == 0)
    def _():
        for rows in chunks:
            h = (x_ref[0, rows, :] * (1.0 + mod(sc_ref, rows)) + mod(sh_ref, rows)).astype(BF16)
            h_ref[rows, :] = h
            o_ref[0, rows, :] = contribution(h)

    @pl.when(jnp.logical_and(f > 0, f < last))
    def _():
        o_ref[0] += contribution(h_ref[...])

    @pl.when(f == last)
    def _():
        for rows in chunks:
            acc = o_ref[0, rows, :] + contribution(h_ref[rows, :])
            z = ALPHA * x_ref[0, rows, :] + mod(g2_ref, rows) * acc
            o_ref[0, rows, :] = _layernorm(z, lnw_ref[...], lnb_ref[...])


def _ffn(x, mod3, w_in, w_out, l, lnw, lnb, tm, tf):
    B, L, _ = x.shape
    nf = D_FF // tf
    return pl.pallas_call(
        _ffn_body,
        grid=(B, L // tm, nf),
        in_specs=[pl.BlockSpec((1, tm, D_MODEL), lambda b, i, f: (b, i, 0)),
                  _mod_spec(mod3, tm, 4, 3),
                  _mod_spec(mod3, tm, 3, 3),
                  _mod_spec(mod3, tm, 5, 3),
                  pl.BlockSpec((1, 1, D_MODEL, tf), lambda b, i, f: (l, f, 0, 0)),
                  pl.BlockSpec((1, 1, D_MODEL, tf), lambda b, i, f: (l, f + nf, 0, 0)),
                  pl.BlockSpec((1, tf, D_MODEL), lambda b, i, f: (l, f, 0)),
                  pl.BlockSpec((1, D_MODEL), lambda b, i, f: (0, 0)),
                  pl.BlockSpec((1, D_MODEL), lambda b, i, f: (0, 0))],
        out_specs=pl.BlockSpec((1, tm, D_MODEL), lambda b, i, f: (b, i, 0)),
        out_shape=jax.ShapeDtypeStruct((B, L, D_MODEL), F32),
        scratch_shapes=[pltpu.VMEM((tm, D_MODEL), BF16)],
        compiler_params=_params(("parallel", "parallel", "arbitrary")),
        name="ffn_ln2",
    )(x, mod3, mod3, mod3, w_in, w_in, w_out, lnw, lnb)


def _rope_body(cp_ref, sp_ref, cs_ref, ss_ref):
    half = RET_DK // 2
    for c_ref, s_ref, base, step in ((cp_ref, sp_ref, 0, 1), (cs_ref, ss_ref, PAST_LEN, 0)):
        shape = c_ref.shape
        lane = lax.broadcasted_iota(jnp.int32, shape, 1)
        row = lax.broadcasted_iota(jnp.int32, shape, 0)
        freq = jnp.exp(jnp.where(lane < half, lane, lane - half).astype(F32) * (-math.log(ROPE_BASE) / half))
        ang = (row * step + base).astype(F32) * freq
        c_ref[...] = jnp.cos(ang)
        s = jnp.sin(ang)
        s_ref[...] = jnp.where(lane < RET_DK // 2, -s, s)


def _rope_tables(seq):
    return pl.pallas_call(
        _rope_body,
        out_shape=[jax.ShapeDtypeStruct((seq, RET_DK), F32), jax.ShapeDtypeStruct((seq, RET_DK), F32),
                   jax.ShapeDtypeStruct((8, RET_DK), F32), jax.ShapeDtypeStruct((8, RET_DK), F32)],
        name="rope_tables",
    )()


def _rope(x, cos, sin_signed):
    return x * cos + pltpu.roll(x, RET_DK // 2, 1) * sin_signed


def _ret_prompt_body(q_ref, k_ref, v_ref, g_ref, cos_ref, sin_ref, gnw_ref, o_ref, s_ref, S):
    c = pl.program_id(1)
    C = RET_CHUNK

    @pl.when(c == 0)
    def _():
        S[...] = jnp.zeros_like(S)

    cos = cos_ref[...]
    sin = sin_ref[...]
    row = lax.broadcasted_iota(jnp.int32, (C, C), 0).astype(F32)
    col = lax.broadcasted_iota(jnp.int32, (C, C), 1).astype(F32)
    diff = row - col
    for h in range(RET_HEADS):
        lg = RET_LOG_G[h]
        sl = slice(h * RET_DK, (h + 1) * RET_DK)
        dmask = jnp.exp(jnp.where(diff >= 0, lg * diff, -jnp.inf))
        k_decay = jnp.exp(lg * (C - 1 - row))
        q_decay = jnp.exp(lg * (row + 1.0))
        for bt in range(q_ref.shape[0]):
            q = _rope(q_ref[bt, :, sl], cos, sin)
            k = _rope(k_ref[bt, :, sl], cos, sin) * (RET_DK ** -0.5)
            v = v_ref[bt, :, sl]
            scores = _bdot_nt(q, k) * dmask
            intra = _bdot(scores, v)
            kv = _bdot_tn(k * k_decay, v)
            s_prev = S[bt, h]
            cross = _bdot(q * q_decay, s_prev)
            S[bt, h] = s_prev * math.exp(lg * C) + kv
            ro = intra + cross
            mu = jnp.mean(ro, axis=-1, keepdims=True)
            rc = ro - mu
            var = jnp.mean(rc * rc, axis=-1, keepdims=True)
            rn = rc * lax.rsqrt(var + LN_EPS) * gnw_ref[:, sl] * _silu(g_ref[bt, :, sl])
            o_ref[bt, :, sl] = rn.astype(BF16)

    @pl.when(c == pl.num_programs(1) - 1)
    def _():
        s_ref[...] = S[...]


def _ret_prompt(proj, cos, sin, gnw, nbt):
    B, L, _ = proj.shape
    C = RET_CHUNK
    blk = lambda k: pl.BlockSpec((nbt, C, RET_W), lambda b, c: (b, c, k))
    return pl.pallas_call(
        _ret_prompt_body,
        grid=(B // nbt, L // C),
        in_specs=[blk(0), blk(1), blk(2), blk(3),
                  pl.BlockSpec((C, RET_DK), lambda b, c: (c, 0)),
                  pl.BlockSpec((C, RET_DK), lambda b, c: (c, 0)),
                  pl.BlockSpec((1, RET_W), lambda b, c: (0, 0))],
        out_specs=[pl.BlockSpec((nbt, C, RET_W), lambda b, c: (b, c, 0)),
                   pl.BlockSpec((nbt, RET_HEADS, RET_DK, RET_DK), lambda b, c: (b, 0, 0, 0))],
        out_shape=[jax.ShapeDtypeStruct((B, L, RET_W), BF16),
                   jax.ShapeDtypeStruct((B, RET_HEADS, RET_DK, RET_DK), F32)],
        scratch_shapes=[pltpu.VMEM((nbt, RET_HEADS, RET_DK, RET_DK), F32)],
        compiler_params=_params(("parallel", "arbitrary")),
        name="ret_prompt",
    )(proj, proj, proj, proj, cos, sin, gnw)


def _ret_sample_body(q_ref, k_ref, v_ref, g_ref, cos_ref, sin_ref, gnw_ref, dec_ref, s0_ref, o_ref, s_ref, obuf):
    nb = q_ref.shape[0]
    cos = cos_ref[0:1, :]
    sin = sin_ref[0:1, :]
    gdec = dec_ref[0]
    q = _rope(q_ref[...], cos, sin)
    k = _rope(k_ref[...], cos, sin) * (RET_DK ** -0.5)
    v = v_ref[...]
    qk = jnp.sum(q * k, axis=1, keepdims=True)
    qT = q.T
    kT = k.T
    for b in range(nb):
        s0 = s0_ref[0, b, 0]
        qcol = qT[:, b:b + 1]
        kcol = kT[:, b:b + 1]
        vrow = v[b:b + 1, :]
        cross = gdec * jnp.sum(s0 * qcol, axis=0, keepdims=True)
        s_ref[0, b, 0] = s0 * gdec + kcol * vrow
        obuf[b:b + 1, :] = qk[b:b + 1, :] * vrow + cross
    ro = obuf[...]
    mu = jnp.mean(ro, axis=-1, keepdims=True)
    rc = ro - mu
    var = jnp.mean(rc * rc, axis=-1, keepdims=True)
    o_ref[...] = (rc * lax.rsqrt(var + LN_EPS) * gnw_ref[...] * _silu(g_ref[...])).astype(BF16)


def _state_phases(body, n_in, state_pos, carried):
    def phased(*refs):
        p = pl.program_id(0)

        @pl.when(p == 0)
        def _():
            body(*refs)

        @pl.when(p == 1)
        def _():
            refs[state_pos][...] = jnp.zeros(refs[state_pos].shape, F32)

    if carried is None:
        return phased, 2, [], [], {}

    def carried_body(*refs):
        phased(*refs[:n_in], *refs[n_in + 1:])

    return carried_body, 1, [pl.BlockSpec(memory_space=pl.ANY)], [carried], {n_in: 1}


def _ret_sample(proj2, cos, sin, gnw, state, l, carried):
    nb = proj2.shape[0]
    H = RET_HEADS
    dec = jnp.asarray(np.broadcast_to(np.exp(np.array(RET_LOG_G))[:, None, None], (H, 1, RET_DK)), F32)
    hh = lambda p, h: h * (1 - p) + (H - 1) * p
    blk = lambda k: pl.BlockSpec((nb, RET_DK), lambda p, h: (0, k * H + hh(p, h)))
    in_specs = [blk(0), blk(1), blk(2), blk(3),
                pl.BlockSpec((8, RET_DK), lambda p, h: (0, 0)),
                pl.BlockSpec((8, RET_DK), lambda p, h: (0, 0)),
                pl.BlockSpec((1, RET_DK), lambda p, h: (0, hh(p, h))),
                pl.BlockSpec((1, 1, RET_DK), lambda p, h: (hh(p, h), 0, 0)),
                pl.BlockSpec((1, nb, 1, RET_DK, RET_DK), lambda p, h: (l, 0, hh(p, h), 0, 0))]
    body, phases, extra_specs, extra_args, aliases = _state_phases(_ret_sample_body, len(in_specs),
                                                                   len(in_specs) + 1, carried)
    return pl.pallas_call(
        body,
        grid=(phases, H),
        in_specs=in_specs + extra_specs,
        out_specs=[pl.BlockSpec((nb, RET_DK), lambda p, h: (0, hh(p, h))),
                   pl.BlockSpec((1, nb, 1, RET_DK, RET_DK), lambda p, h: (l + p, 0, h, 0, 0))],
        out_shape=[jax.ShapeDtypeStruct((nb, RET_W), BF16),
                   jax.ShapeDtypeStruct(state.shape, F32)],
        scratch_shapes=[pltpu.VMEM((nb, RET_DK), F32)],
        input_output_aliases=aliases,
        compiler_params=_params(("arbitrary", "arbitrary")),
        name="ret_sample",
    )(proj2, proj2, proj2, proj2, cos, sin, gnw, dec, state, *extra_args)


def _gdn_prompt_body(dq_ref, dk_ref, dv_ref, dz_ref, ab_ref, cw_ref, alog_ref, dtb_ref, nw_ref,
                     o_ref, s_ref, conv_ref, xbuf, S):
    c = pl.program_id(1)
    C = GDN_CHUNK
    H = GDN_HEADS
    dk = GDN_DK
    nbt = dq_ref.shape[0]

    @pl.when(c == 0)
    def _():
        S[...] = jnp.zeros_like(S)
        xbuf[:, 0:8, :] = jnp.zeros((nbt, 8, GDN_CONV_C), F32)

    row = lax.broadcasted_iota(jnp.int32, (C, dk), 0)
    lane = lax.broadcasted_iota(jnp.int32, (C, dk), 1)
    tri = lane <= row
    strict = lane < row
    r3 =lax.broadcasted_iota(jnp.int32, (C, 3 * C), 0)
    c3 = lax.broadcasted_iota(jnp.int32, (C, 3 * C), 1) % C
    tril_b = jnp.where(r3 >= c3, 1.0, 0.0).astype(BF16)

    def cumsum_rows(x):
        hi = x.astype(BF16)
        r1 = x - hi.astype(F32)
        mid = r1.astype(BF16)
        lo = (r1 - mid.astype(F32)).astype(BF16)
        return jnp.dot(tril_b, jnp.concatenate([hi, mid, lo], axis=0), preferred_element_type=F32)

    def lanes(x, j):
        return jnp.broadcast_to(x[:, j:j + 1], (x.shape[0], dk))

    cw = cw_ref[...]
    zrow_b = jnp.zeros((C, dk), BF16)

    items = []
    for bt in range(nbt):
        xbuf[bt, 8:8 + C, 0:GDN_W] = dq_ref[bt]
        xbuf[bt, 8:8 + C, GDN_W:2 * GDN_W] = dk_ref[bt]
        xbuf[bt, 8:8 + C, 2 * GDN_W:3 * GDN_W] = dv_ref[bt]

        def conv(off, bt=bt):
            xe = xbuf[bt, :, off:off + dk]
            y = xe[8:8 + C] * cw[CONV_W - 1:CONV_W, off:off + dk]
            for i in range(CONV_W - 1):
                shifted = pltpu.roll(xe, CONV_W - 1 - i, 0)[8:8 + C]
                y = y + shifted * cw[i:i + 1, off:off + dk]
            return _silu(y)

        ab = ab_ref[bt]
        g_c = -jnp.exp(alog_ref[...]) * jax.nn.softplus(ab + dtb_ref[...])
        gc_c = cumsum_rows(g_c)
        beta_c = jax.nn.sigmoid(ab)
        egc_c = jnp.exp(gc_c)
        gl_c = gc_c[C - 1:C, :]
        kdec_c = jnp.exp(gl_c - gc_c)
        egl_c = jnp.exp(gl_c)
        gc_t = jnp.concatenate([gc_c, jnp.zeros_like(gc_c)], axis=0).T

        for h in range(H):
            sl = slice(h * dk, (h + 1) * dk)
            q = conv(h * dk)
            k = conv(GDN_W + h * dk)
            v = conv(2 * GDN_W + h * dk)
            q = q * lax.rsqrt(jnp.sum(q * q, axis=-1, keepdims=True) + L2_EPS) * (dk ** -0.5)
            k = k * lax.rsqrt(jnp.sum(k * k, axis=-1, keepdims=True) + L2_EPS)
            beta = lanes(beta_c, H + h)
            egc = lanes(egc_c, h)
            kb = k * beta
            kq = jnp.concatenate([kb, q], axis=0).astype(BF16)
            k2 = jnp.concatenate([k.astype(BF16), zrow_b], axis=0)
            kkqk = lax.dot_general(kq, k2, NT, preferred_element_type=F32)
            dgc = lanes(gc_c, h) - jnp.broadcast_to(gc_t[h:h + 1, :], (C, dk))
            decay = jnp.exp(jnp.where(tri, dgc, -jnp.inf))
            a = jnp.where(strict, kkqk[0:C] * decay, 0.0)
            attn = (kkqk[C:2 * C] * decay)[:, 0:C].astype(BF16)
            rhs = jnp.concatenate([v * beta, kb * egc], axis=1).astype(BF16)
            items.append(dict(bt=bt, h=h, a=a, rhs=rhs, attn=attn, qd=(q * egc).astype(BF16),
                              kd=(k * lanes(kdec_c, h)).astype(BF16), egl=lanes(egl_c, h)))

    pair_col = lane % C
    eye_pair = (pair_col == row).astype(F32)
    left = lane < C

    def level_mask(s):
        br = row // s
        return (pair_col // s) == jnp.where(br % 2 == 1, br - 1, -1)

    def block_diag(x):
        zero = jnp.zeros_like(x)
        return jnp.concatenate([jnp.where(left, x, zero), jnp.where(left, zero, x)], axis=0)

    pairs = []
    for i in range(0, len(items), 2):
        a_pair = items[i]["a"] + pltpu.roll(items[i + 1]["a"], C, 1)
        pairs.append(dict(a=a_pair, d=eye_pair - jnp.where(level_mask(1), a_pair, 0.0)))
    s_blk = 2
    while s_blk < C:
        mask = level_mask(s_blk)
        for p in pairs:
            d = p["d"]
            sub = block_diag(jnp.where(mask, p["a"], 0.0)).astype(BF16)
            g = jnp.dot(d.astype(BF16), sub, preferred_element_type=F32)
            p["d"] = d - jnp.dot(g.astype(BF16), block_diag(d).astype(BF16), preferred_element_type=F32)
        s_blk *= 2
    for i, p in enumerate(pairs):
        r0, r1 = items[2 * i]["rhs"], items[2 * i + 1]["rhs"]
        zero = jnp.zeros_like(r0)
        rr = jnp.concatenate([jnp.concatenate([r0, zero], axis=1), jnp.concatenate([zero, r1], axis=1)], axis=0)
        sol = jnp.dot(p["d"].astype(BF16), rr, preferred_element_type=F32)
        items[2 * i]["sol"] = sol[:, 0:2 * dk]
        items[2 * i + 1]["sol"] = sol[:, 2 * dk:4 * dk]

    for it in items:
        bt, h = it["bt"], it["h"]
        sl = slice(h * dk, (h + 1) * dk)
        u = it["sol"][:, 0:dk]
        w = it["sol"][:, dk:2 * dk]
        s_prev = S[bt, h]
        wq = jnp.concatenate([w.astype(BF16), it["qd"]], axis=0)
        r = jnp.dot(wq, s_prev.astype(BF16), preferred_element_type=F32)
        v_new = u - r[0:C]
        o = r[C:2 * C] + jnp.dot(it["attn"], v_new.astype(BF16), preferred_element_type=F32)
        S[bt, h] = s_prev * it["egl"] + lax.dot_general(it["kd"], v_new.astype(BF16), TN, preferred_element_type=F32)
        o = o * lax.rsqrt(jnp.mean(o * o, axis=-1, keepdims=True) + RMS_EPS) * nw_ref[...]
        o_ref[bt, :, sl] = (o * _silu(dz_ref[bt, :, sl])).astype(BF16)

    for bt in range(nbt):
        xbuf[bt, 0:8, :] = xbuf[bt, C:C + 8, :]

    @pl.when(c == pl.num_programs(1) - 1)
    def _():
        s_ref[...] = S[...]
        conv_ref[...] = xbuf[:, C + 8 - (CONV_W - 1):C + 8, :]


def _gdn_prompt(proj, cw, alog_bc, dtb_bc, nw, nbt):
    B, L, _ = proj.shape
    C = GDN_CHUNK
    H = GDN_HEADS
    blk = lambda k: pl.BlockSpec((nbt, C, GDN_W), lambda b, c: (b, c, k))
    full = lambda shape: pl.BlockSpec(shape, lambda b, c: tuple(0 for _ in shape))
    return pl.pallas_call(
        _gdn_prompt_body,
        grid=(B // nbt, L // C),
        in_specs=[blk(2), blk(3), blk(4), blk(5),
                  pl.BlockSpec((nbt, C, 128), lambda b, c: (b, c, COL_AB // 128)),
                  full((CONV_W, GDN_CONV_C)), full((1, 128)), full((1, 128)), full((1, GDN_DK))],
        out_specs=[pl.BlockSpec((nbt, C, GDN_W), lambda b, c: (b, c, 0)),
                   pl.BlockSpec((nbt, H, GDN_DK, GDN_DK), lambda b, c: (b, 0, 0, 0)),
                   pl.BlockSpec((nbt, CONV_W - 1, GDN_CONV_C), lambda b, c: (b, 0, 0))],
        out_shape=[jax.ShapeDtypeStruct((B, L, GDN_W), BF16),
                   jax.ShapeDtypeStruct((B, H, GDN_DK, GDN_DK), F32),
                   jax.ShapeDtypeStruct((B, CONV_W - 1, GDN_CONV_C), F32)],
        scratch_shapes=[pltpu.VMEM((nbt, C + 8, GDN_CONV_C), F32),
                        pltpu.VMEM((nbt, H, GDN_DK, GDN_DK), F32)],
        compiler_params=_params(("parallel", "arbitrary")),
        name="gdn_prompt",
    )(proj, proj, proj, proj, proj, cw, alog_bc, dtb_bc, nw)


def _gdn_sample_body(dq_ref, dk_ref, dv_ref, dz_ref, ab_ref, csq_ref, csk_ref, csv_ref, cwq_ref, cwk_ref, cwv_ref,
                     alog_ref, dtb_ref, nw_ref, s0_ref, o_ref, s_ref, nq_ref, nk_ref, nv_ref, obuf):
    h = pl.program_id(1)
    nb = dq_ref.shape[0]
    dk = GDN_DK

    def conv(x_ref, cs_ref, w_ref, n_ref):
        x = x_ref[...]
        y = cs_ref[0] * w_ref[0:1, :] + cs_ref[1] * w_ref[1:2, :] + cs_ref[2] * w_ref[2:3, :] + x * w_ref[3:4, :]
        n_ref[0] = cs_ref[1]
        n_ref[1] = cs_ref[2]
        n_ref[2] = x
        return _silu(y)

    q = conv(dq_ref, csq_ref, cwq_ref, nq_ref)
    k = conv(dk_ref, csk_ref, cwk_ref, nk_ref)
    v = conv(dv_ref, csv_ref, cwv_ref, nv_ref)
    q = q * lax.rsqrt(jnp.sum(q * q, axis=-1, keepdims=True) + L2_EPS) * (dk ** -0.5)
    k = k * lax.rsqrt(jnp.sum(k * k, axis=-1, keepdims=True) + L2_EPS)
    er = lax.broadcasted_iota(jnp.int32, (128, dk), 0)
    ab = ab_ref[...]
    a_bc = jnp.dot(ab, (er == h).astype(F32), precision=HI, preferred_element_type=F32)
    b_bc = jnp.dot(ab, (er == h + GDN_HEADS).astype(F32), precision=HI, preferred_element_type=F32)
    g = -jnp.exp(alog_ref[0]) * jax.nn.softplus(a_bc + dtb_ref[0])
    eg = jnp.exp(g)
    beta = jax.nn.sigmoid(b_bc)
    qk = jnp.sum(q * k, axis=1, keepdims=True)
    qT = q.T
    kT = k.T
    for b in range(nb):
        s0 = s0_ref[0, b, 0]
        qcol = qT[:, b:b + 1]
        kcol = kT[:, b:b + 1]
        ks = jnp.sum(s0 * kcol, axis=0, keepdims=True)
        qs = jnp.sum(s0 * qcol, axis=0, keepdims=True)
        bb = beta[b:b + 1, :]
        egb = eg[b:b + 1, :]
        v_new = bb * v[b:b + 1, :] - bb * egb * ks
        s_ref[0, b, 0] = s0 * egb + kcol * v_new
        obuf[b:b + 1, :] = egb * qs + qk[b:b + 1, :] * v_new
    o = obuf[...]
    o = o * lax.rsqrt(jnp.mean(o * o, axis=-1, keepdims=True) + RMS_EPS) * nw_ref[...]
    o_ref[...] = (o * _silu(dz_ref[...])).astype(BF16)


def _gdn_sample(proj2, cs_t, cw, alog_t, dtb_t, nw, state, l, carried):
    nb = proj2.shape[0]
    H = GDN_HEADS
    dk = GDN_DK
    hh = lambda p, h: h * (1 - p) + (H - 1) * p
    blk = lambda k: pl.BlockSpec((nb, dk), lambda p, h: (0, k * H + hh(p, h)))
    csblk = lambda k: pl.BlockSpec((CONV_W - 1, nb, dk), lambda p, h: (0, 0, k * H + hh(p, h)))
    cwblk = lambda k: pl.BlockSpec((CONV_W, dk), lambda p, h: (0, k * H + hh(p, h)))
    tab = pl.BlockSpec((1, 1, dk), lambda p, h: (hh(p, h), 0, 0))
    nblk = pl.BlockSpec((CONV_W - 1, nb, dk), lambda p, h: (0, 0, hh(p, h)))
    nshape = jax.ShapeDtypeStruct((CONV_W - 1, nb, GDN_W), F32)
    in_specs = [blk(2), blk(3), blk(4), blk(5),
                pl.BlockSpec((nb, 128), lambda p, h: (0, COL_AB // 128)),
                csblk(0), csblk(1), csblk(2), cwblk(0), cwblk(1), cwblk(2),
                tab, tab, pl.BlockSpec((1, dk), lambda p, h: (0, 0)),
                pl.BlockSpec((1, nb, 1, dk, dk), lambda p, h: (l, 0, hh(p, h), 0, 0))]
    body, phases, extra_specs, extra_args, aliases = _state_phases(_gdn_sample_body, len(in_specs),
                                                                   len(in_specs) + 1, carried)
    return pl.pallas_call(
        body,
        grid=(phases, H),
        in_specs=in_specs + extra_specs,
        out_specs=[pl.BlockSpec((nb, dk), lambda p, h: (0, hh(p, h))),
                   pl.BlockSpec((1, nb, 1, dk, dk), lambda p, h: (l + p, 0, h, 0, 0)), nblk, nblk, nblk],
        out_shape=[jax.ShapeDtypeStruct((nb, GDN_W), BF16), jax.ShapeDtypeStruct(state.shape, F32),
                   nshape, nshape, nshape],
        scratch_shapes=[pltpu.VMEM((nb, dk), F32)],
        input_output_aliases=aliases,
        compiler_params=_params(("arbitrary", "arbitrary")),
        name="gdn_sample",
    )(proj2, proj2, proj2, proj2, proj2, cs_t, cs_t, cs_t, cw, cw, cw, alog_t, dtb_t, nw, state, *extra_args)


def _ssm_prep_body(are_ref, aim_ref, ldt_ref, bre_ref, bim_ref, lre_ref, lim_ref, obre_ref, obim_ref):
    ar = are_ref[0]
    ai = aim_ref[0]
    dt = jnp.exp(ldt_ref[0])
    er = jnp.exp(ar * dt)
    lr = er * jnp.cos(ai * dt)
    li = er * jnp.sin(ai * dt)
    lre_ref[0] = lr
    lim_ref[0] = li
    xr = lr - 1.0
    den = ar * ar + ai * ai
    cr = ((xr * ar + li * ai) / den)[:, None, :]
    ci = ((li * ar - xr * ai) / den)[:, None, :]
    br = bre_ref[0]
    bi = bim_ref[0]
    obre_ref[0] = cr * br - ci * bi
    obim_ref[0] = cr * bi + ci * br


def _ssm_prep(a_re, a_im, log_dt, b_re, b_im):
    G, N, P = SSM_GROUPS, SSM_N, SSM_GROUP
    ldt = jnp.broadcast_to(log_dt[:, :, None], (DEPTH, G, N))
    bt_re = jnp.transpose(b_re, (0, 1, 3, 2))
    bt_im = jnp.transpose(b_im, (0, 1, 3, 2))
    s2 = pl.BlockSpec((1, G, N), lambda l: (l, 0, 0))
    s3 = pl.BlockSpec((1, G, P, N), lambda l: (l, 0, 0, 0))
    return pl.pallas_call(
        _ssm_prep_body,
        grid=(DEPTH,),
        in_specs=[s2, s2, s2, s3, s3],
        out_specs=[s2, s2, s3, s3],
        out_shape=[jax.ShapeDtypeStruct((DEPTH, G, N), F32), jax.ShapeDtypeStruct((DEPTH, G, N), F32),
                   jax.ShapeDtypeStruct((DEPTH, G, P, N), F32), jax.ShapeDtypeStruct((DEPTH, G, P, N), F32)],
        name="ssm_prep",
    )(a_re, a_im, ldt, bt_re, bt_im)


def _ssm_tile_blockdiag(re, im):
    x = jnp.stack([re, im]).reshape(2, SSM_TILES, SSM_TILE_G, SSM_GROUP, SSM_N)
    eye = jnp.eye(SSM_TILE_G, dtype=x.dtype)
    y = x[:, :, :, :, None, :] * eye[None, None, :, None, :, None]
    y = jnp.transpose(y, (1, 2, 3, 0, 4, 5))
    return y.reshape(SSM_TILES, SSM_TILE_G * SSM_GROUP, SSM_TILE_W)


def _ssm_state_to_tiles(re, im):
    B = re.shape[0]
    x = jnp.stack([re, im], axis=1).reshape(B, 2, SSM_TILES, SSM_HALF)
    return jnp.transpose(x, (0, 2, 1, 3)).reshape(B, SSM_TILES * SSM_TILE_W)


def _ssm_tiles_to_state(x):
    B = x.shape[0]
    x = jnp.transpose(x.reshape(B, SSM_TILES, 2, SSM_HALF), (0, 2, 1, 3)).reshape(B, 2, SSM_GROUPS, SSM_N)
    return x[:, 0], x[:, 1]


def _cmul(ar, ai, br, bi):
    return ar * br - ai * bi, ar * bi + ai * br


def _s5_prompt_body(u_ref, bt_ref, ct_ref, lam_ref, d_ref, y_ref, hl_ref, buf):
    L = u_ref.shape[1]
    nseg = L // SSM_SEG
    ns = SSM_HALF // 128
    bmat = bt_ref[0].astype(BF16)
    for i in range(nseg):
        bu = jnp.dot(u_ref[0, i * SSM_SEG:(i + 1) * SSM_SEG, :].astype(BF16), bmat, preferred_element_type=F32)
        for s in range(2 * ns):
            buf[s, i * SSM_PITCH:i * SSM_PITCH + SSM_SEG, :] = bu[:, s * 128:(s + 1) * 128]

    lam = lam_ref[0]
    a_re = [jnp.broadcast_to(lam[:, s * 128:(s + 1) * 128], (nseg, 128)) for s in range(ns)]
    a_im = [jnp.broadcast_to(lam[:, SSM_HALF + s * 128:SSM_HALF + (s + 1) * 128], (nseg, 128)) for s in range(ns)]

    def scan_step(t, carry):
        out = []
        for s in range(ns):
            hr, hi = carry[2 * s], carry[2 * s + 1]
            pr, pi = _cmul(a_re[s], a_im[s], hr, hi)
            hr = pr + buf[s, pl.ds(t, nseg, stride=SSM_PITCH), :]
            hi = pi + buf[ns + s, pl.ds(t, nseg, stride=SSM_PITCH), :]
            buf[s, pl.ds(t, nseg, stride=SSM_PITCH), :] = hr
            buf[ns + s, pl.ds(t, nseg, stride=SSM_PITCH), :] = hi
            out += [hr, hi]
        return tuple(out)

    zero = jnp.zeros((nseg, 128), F32)
    ends = lax.fori_loop(0, SSM_SEG, scan_step, tuple(zero for _ in range(2 * ns)), unroll=SSM_UNROLL)

    sub = lax.broadcasted_iota(jnp.int32, (nseg, 128), 0)
    carries = []
    for s in range(ns):
        pr, pi = a_re[s], a_im[s]
        for _ in range(int(math.log2(SSM_SEG))):
            pr, pi = _cmul(pr, pi, pr, pi)
        xr, xi = ends[2 * s], ends[2 * s + 1]
        shift = 1
        while shift < nseg:
            sr = jnp.where(sub >= shift, pltpu.roll(xr, shift, 0), 0.0)
            si = jnp.where(sub >= shift, pltpu.roll(xi, shift, 0), 0.0)
            mr, mi = _cmul(pr, pi, sr, si)
            xr, xi = xr + mr, xi + mi
            pr, pi = _cmul(pr, pi, pr, pi)
            shift *= 2
        hl_ref[0, 0, :, s * 128:(s + 1) * 128] = xr[nseg - 1:nseg, :]
        hl_ref[0, 0, :, SSM_HALF + s * 128:SSM_HALF + (s + 1) * 128] = xi[nseg - 1:nseg, :]
        carries += [jnp.where(sub >= 1, pltpu.roll(xr, 1, 0), 0.0), jnp.where(sub >= 1, pltpu.roll(xi, 1, 0), 0.0)]

    def fix_step(t, pw):
        out = []
        for s in range(ns):
            pr, pi = pw[2 * s], pw[2 * s + 1]
            fr, fi = _cmul(pr, pi, carries[2 * s], carries[2 * s + 1])
            buf[s, pl.ds(t, nseg, stride=SSM_PITCH), :] += fr
            buf[ns + s, pl.ds(t, nseg, stride=SSM_PITCH), :] += fi
            nr, ni = _cmul(pr, pi, a_re[s], a_im[s])
            out += [nr, ni]
        return tuple(out)

    pw0 = []
    for s in range(ns):
        pw0 += [a_re[s], a_im[s]]
    lax.fori_loop(0, SSM_SEG, fix_step, tuple(pw0), unroll=SSM_UNROLL)

    cmat = ct_ref[0]
    cmat = jnp.concatenate([cmat[:, 0:SSM_HALF], -cmat[:, SSM_HALF:]], axis=1).astype(BF16)
    for i in range(nseg):
        rows = slice(i * SSM_PITCH, i * SSM_PITCH + SSM_SEG)
        h = jnp.concatenate([buf[s, rows, :].astype(BF16) for s in range(2 * ns)], axis=1)
        y = lax.dot_general(h, cmat, NT, preferred_element_type=F32)
        u = u_ref[0, i * SSM_SEG:(i + 1) * SSM_SEG, :]
        y_ref[0, i * SSM_SEG:(i + 1) * SSM_SEG, :] = jax.nn.gelu(y + d_ref[...] * u)


def _s5_prompt(proj, bt, ct, lam, d):
    B, L, _ = proj.shape
    nseg = L // SSM_SEG
    tile = lambda shape: pl.BlockSpec(shape, lambda b, j: (j,) + tuple(0 for _ in shape[1:]))
    return pl.pallas_call(
        _s5_prompt_body,
        grid=(B, SSM_TILES),
        in_specs=[pl.BlockSpec((1, L, 128), lambda b, j: (b, 0, COL_SU // 128 + j)),
                  tile((1, 128, SSM_TILE_W)), tile((1, 128, SSM_TILE_W)), tile((1, 1, SSM_TILE_W)),
                  pl.BlockSpec((1, 128), lambda b, j: (0, j))],
        out_specs=[pl.BlockSpec((1, L, 128), lambda b, j: (b, 0, j)),
                   pl.BlockSpec((1, 1, 1, SSM_TILE_W), lambda b, j: (b, j, 0, 0))],
        out_shape=[jax.ShapeDtypeStruct((B, L, SSM_W), F32),
                   jax.ShapeDtypeStruct((B, SSM_TILES, 1, SSM_TILE_W), F32)],
        scratch_shapes=[pltpu.VMEM((SSM_TILE_W // 128, nseg * SSM_PITCH, 128), F32)],
        compiler_params=_params(("parallel", "parallel")),
        name="s5_prompt",
    )(proj, bt, ct, lam, d)


def _s5_sample_body(u_ref, bt_ref, ct_ref, lam_ref, d_ref, h0_ref, y_ref, h_ref):
    u = u_ref[...]
    bu = jnp.dot(u, bt_ref[0], precision=HI, preferred_element_type=F32)
    lam = lam_ref[0]
    lr, li = lam[:, 0:SSM_HALF], lam[:, SSM_HALF:]
    h0r, h0i = h0_ref[:, 0:SSM_HALF], h0_ref[:, SSM_HALF:]
    pr, pi = _cmul(lr, li, h0r, h0i)
    hr = pr + bu[:, 0:SSM_HALF]
    hi = pi + bu[:, SSM_HALF:]
    h_ref[:, 0:SSM_HALF] = hr
    h_ref[:, SSM_HALF:] = hi
    ct = ct_ref[0]
    y = lax.dot_general(hr, ct[:, 0:SSM_HALF], NT, precision=HI, preferred_element_type=F32)
    y -= lax.dot_general(hi, ct[:, SSM_HALF:], NT, precision=HI, preferred_element_type=F32)
    y_ref[...] = jax.nn.gelu(y + d_ref[...] * u)


def _s5_sample(proj2, bt, ct, lam, d, h0):
    nb = proj2.shape[0]
    tile = lambda shape: pl.BlockSpec(shape, lambda j: (j,) + tuple(0 for _ in shape[1:]))
    hblk = pl.BlockSpec((nb, SSM_TILE_W), lambda j: (0, j))
    return pl.pallas_call(
        _s5_sample_body,
        grid=(SSM_TILES,),
        in_specs=[pl.BlockSpec((nb, 128), lambda j: (0, COL_SU // 128 + j)),
                  tile((1, 128, SSM_TILE_W)), tile((1, 128, SSM_TILE_W)), tile((1, 1, SSM_TILE_W)),
                  pl.BlockSpec((1, 128), lambda j: (0, j)), hblk],
        out_specs=[pl.BlockSpec((nb, 128), lambda j: (0, j)), hblk],
        out_shape=[jax.ShapeDtypeStruct((nb, SSM_W), F32), jax.ShapeDtypeStruct(h0.shape, F32)],
        compiler_params=_params(("parallel",)),
        name="s5_sample",
    )(proj2, bt, ct, lam, d, h0)


def kernel(x_prompt, x_sample, c_prompt, c_sample, state_ret, state_gdn, state_conv, state_ssm_re, state_ssm_im, w_mod, b_mod, w_in, conv_w, ret_gn_w, gdn_a_log, gdn_dt_bias, gdn_norm_w, ssm_a_re, ssm_a_im, ssm_log_dt, ssm_b_re, ssm_b_im, ssm_c_re, ssm_c_im, ssm_d, ssm_w_glu, w_out, ln1_w, ln1_b, w_ffn_in, w_ffn_out, ln2_w, ln2_b):
    Bp, Lp, _ = x_prompt.shape
    Bs = x_sample.shape[0]
    assert x_sample.shape[1] == 1 and Bs % 8 == 0

    pad_rows = (-(Bp + Bs)) % 8
    c_all = jnp.concatenate([c_prompt, c_sample, jnp.zeros((pad_rows, D_MODEL), F32)], axis=0)
    mod_all = _modulation(c_all, w_mod, b_mod)
    lam_re, lam_im, bbar_re, bbar_im = _ssm_prep(ssm_a_re, ssm_a_im, ssm_log_dt, ssm_b_re, ssm_b_im)
    cos_p, sin_p, cos_s, sin_s = _rope_tables(Lp)

    w_in_b = w_in.astype(BF16)
    w_out_b = _to_bf16(w_out, rows=512)
    w_glu_b = ssm_w_glu.astype(BF16)
    w_ffi_b = _to_bf16_col_tiles(w_ffn_in, rows=256, tc=FFN_TF)
    w_ffo_b = _to_bf16(w_ffn_out, rows=512)

    yp = x_prompt
    ys = x_sample.reshape(1, Bs, D_MODEL)
    outs_p, outs_s = [], []
    ret_s = gdn_s = None
    for l in range(DEPTH):
        w_tail = jnp.concatenate(
            [w_in_b[l, :, COL_SU + 2 * GDN_HEADS:], w_in_b[l, :, COL_SU:COL_SU + 2 * GDN_HEADS],
             jnp.zeros((D_MODEL, N_PROJ - N_IN), BF16)], axis=1)
        mod_p = mod_all[l, :Bp][:, None, :]
        mod_s = mod_all[l, Bp:Bp + Bs][None]
        gnw = ret_gn_w[l][None, :]
        nw = gdn_norm_w[l][None, :]
        alog_bc = jnp.pad(gdn_a_log[l], (0, 128 - GDN_HEADS))[None, :]
        dtb_bc = jnp.pad(gdn_dt_bias[l], (0, 128 - GDN_HEADS))[None, :]
        alog_t = jnp.broadcast_to(gdn_a_log[l][:, None, None], (GDN_HEADS, 1, GDN_DK))
        dtb_t = jnp.broadcast_to(gdn_dt_bias[l][:, None, None], (GDN_HEADS, 1, GDN_DK))
        ssm_bt = _ssm_tile_blockdiag(bbar_re[l], bbar_im[l])
        ssm_ct = _ssm_tile_blockdiag(ssm_c_re[l], ssm_c_im[l])
        ssm_lam = jnp.concatenate([lam_re[l].reshape(SSM_TILES, 1, SSM_HALF),
                                   lam_im[l].reshape(SSM_TILES, 1, SSM_HALF)], axis=-1)
        ssm_dl = ssm_d[l][None, :]
        lnw1, lnb1 = ln1_w[l][None, :], ln1_b[l][None, :]
        lnw2, lnb2 = ln2_w[l][None, :], ln2_b[l][None, :]

        proj = _inproj(yp, mod_p, w_in_b, l, w_tail, tm=1024, tn=PROJ_TN)
        ro, ret_p = _ret_prompt(proj, cos_p, sin_p, gnw, nbt=4)
        go, gdn_p, conv_p = _gdn_prompt(proj, conv_w[l], alog_bc, dtb_bc, nw, nbt=4)
        so, hl = _s5_prompt(proj, ssm_bt, ssm_ct, ssm_lam, ssm_dl)
        re_p, im_p = _ssm_tiles_to_state(hl.reshape(Bp, SSM_TILES * SSM_TILE_W))
        x1 = _outproj(ro, go, so, yp, mod_p, w_glu_b, w_out_b, l, lnw1, lnb1, tm=512)
        yp = _ffn(x1, mod_p, w_ffi_b, w_ffo_b, l, lnw2, lnb2, tm=512, tf=FFN_TF)
        outs_p.append((ret_p, gdn_p, conv_p, re_p, im_p))

        proj = _inproj(ys, mod_s, w_in_b, l, w_tail, tm=Bs, tn=PROJ_TN)
        proj2 = proj.reshape(Bs, N_PROJ)
        ro, ret_s = _ret_sample(proj2, cos_s, sin_s, gnw, state_ret, l, ret_s)
        cs_t = jnp.transpose(state_conv[l], (1, 0, 2))
        go, gdn_s, nq, nk, nv = _gdn_sample(proj2, cs_t, conv_w[l], alog_t, dtb_t, nw, state_gdn, l, gdn_s)
        conv_s = jnp.transpose(jnp.concatenate([nq, nk, nv], axis=-1), (1, 0, 2))
        h0 = _ssm_state_to_tiles(state_ssm_re[l], state_ssm_im[l])
        so, hn = _s5_sample(proj2, ssm_bt, ssm_ct, ssm_lam, ssm_dl, h0)
        re_s, im_s = _ssm_tiles_to_state(hn)
        x1 = _outproj(ro[None], go[None], so[None], ys, mod_s, w_glu_b, w_out_b, l, lnw1, lnb1, tm=Bs)
        ys = _ffn(x1, mod_s, w_ffi_b, w_ffo_b, l, lnw2, lnb2, tm=Bs, tf=FFN_TF)
        outs_s.append((conv_s, re_s, im_s))

    ret_p, gdn_p, conv_p, re_p, im_p = [jnp.stack(t) for t in zip(*outs_p)]
    conv_s, re_s, im_s = [jnp.stack(t) for t in zip(*outs_s)]
    return (yp, ys.reshape(Bs, 1, D_MODEL), ret_p, gdn_p, conv_p, re_p, im_p,
            ret_s, gdn_s, conv_s, re_s, im_s)
```

```python
import functools
import math

import numpy as np
import jax
import jax.numpy as jnp
from jax import lax
from jax.experimental import pallas as pl
from jax.experimental.pallas import tpu as pltpu

F32 = jnp.float32
BF16 = jnp.bfloat16

D_MODEL = 2048
DEPTH = 2
PAST_LEN = 16384
RET_HEADS = 4
RET_DK = 128
RET_W = 512
RET_CHUNK = 128
ROPE_BASE = 10000.0
GDN_HEADS = 8
GDN_DK = 128
GDN_W = 1024
GDN_CHUNK = 64
CONV_W = 4
GDN_CONV_C = 3072
SSM_W = 512
SSM_GROUP = 16
SSM_GROUPS = 32
SSM_N = 64
D_FF = 5632
ALPHA = (2 * DEPTH) ** 0.25
LN_EPS = 1e-5
RMS_EPS = 1e-6
L2_EPS = 1e-6

COL_SU = 6144
COL_AB = 6656
N_IN = 6672
PROJ_TN = 768
N_PROJ = COL_SU + PROJ_TN

SSM_TILES = 4
SSM_TILE_G = 8
SSM_HALF = SSM_TILE_G * SSM_N
SSM_TILE_W = 2 * SSM_HALF
SSM_SEG = 256
SSM_PITCH = 260
SSM_UNROLL = 4

VMEM_LIMIT = 56 * 1024 * 1024
OUTPROJ_ROW_CHUNK = 256
FFN_ROW_CHUNK = 256
FFN_TF = 512

RET_LOG_G = [math.log1p(-(2.0 ** (-5.0 - h))) for h in range(RET_HEADS)]

NT = (((1,), (1,)), ((), ()))
TN = (((0,), (0,)), ((), ()))
HI = lax.Precision.HIGHEST


def _params(sem):
    return pltpu.CompilerParams(dimension_semantics=sem, vmem_limit_bytes=VMEM_LIMIT)


def _silu(x):
    return x * jax.nn.sigmoid(x)


def _bdot(a, b):
    return jnp.dot(a.astype(BF16), b.astype(BF16), preferred_element_type=F32)


def _hdot(a, b):
    return jnp.dot(a, b, precision=HI, preferred_element_type=F32)


def _bdot_nt(a, b):
    return lax.dot_general(a.astype(BF16), b.astype(BF16), NT, preferred_element_type=F32)


def _bdot_tn(a, b):
    return lax.dot_general(a.astype(BF16), b.astype(BF16), TN, preferred_element_type=F32)


def _layernorm(z, w, b):
    mu = jnp.mean(z, axis=-1, keepdims=True)
    zc = z - mu
    var = jnp.mean(zc * zc, axis=-1, keepdims=True)
    return zc * lax.rsqrt(var + LN_EPS) * w + b


def _cast_body(x_ref, o_ref):
    o_ref[...] = x_ref[...].astype(o_ref.dtype)


def _to_bf16(w, rows):
    depth, r, n = w.shape
    spec = pl.BlockSpec((1, rows, n), lambda l, i: (l, i, 0))
    return pl.pallas_call(
        _cast_body,
        grid=(depth, r // rows),
        in_specs=[spec],
        out_specs=spec,
        out_shape=jax.ShapeDtypeStruct(w.shape, BF16),
        compiler_params=_params(("parallel", "parallel")),
        name="to_bf16",
    )(w)


def _cast_col_tiles_body(x_ref, o_ref):
    tc = o_ref.shape[3]
    for t in range(o_ref.shape[1]):
        o_ref[0, t] = x_ref[0, :, t * tc:(t + 1) * tc].astype(o_ref.dtype)


def _to_bf16_col_tiles(w, rows, tc):
    depth, r, n = w.shape
    nt = n // tc
    return pl.pallas_call(
        _cast_col_tiles_body,
        grid=(depth, r // rows),
        in_specs=[pl.BlockSpec((1, rows, n), lambda l, i: (l, i, 0))],
        out_specs=pl.BlockSpec((1, nt, rows, tc), lambda l, i: (l, 0, i, 0)),
        out_shape=jax.ShapeDtypeStruct((depth, nt, r, tc), BF16),
        compiler_params=_params(("parallel", "parallel")),
        name="to_bf16_col_tiles",
    )(w)


def _mod_body(c_ref, w_ref, b_ref, o_ref):
    c = c_ref[...]
    o_ref[0] = _bdot(_silu(c), w_ref[0]) + b_ref[0]


def _modulation(c_all, w_mod, b_mod):
    rows = c_all.shape[0]
    tn = 1024
    n = 6 * D_MODEL
    return pl.pallas_call(
        _mod_body,
        grid=(DEPTH, n // tn),
        in_specs=[pl.BlockSpec((rows, D_MODEL), lambda l, j: (0, 0)),
                  pl.BlockSpec((1, D_MODEL, tn), lambda l, j: (l, 0, j)),
                  pl.BlockSpec((1, 1, tn), lambda l, j: (l, 0, j))],
        out_specs=pl.BlockSpec((1, rows, tn), lambda l, j: (l, 0, j)),
        out_shape=jax.ShapeDtypeStruct((DEPTH, rows, n), F32),
        compiler_params=_params(("parallel", "parallel")),
        name="modulation",
    )(c_all, w_mod, b_mod.reshape(DEPTH, 1, n))


def _mod_spec(mod3, tm, k, grid_rank):
    per_token = mod3.shape[1] != 1
    lmb = tm if per_token else 1
    if grid_rank == 3:
        return pl.BlockSpec((1, lmb, D_MODEL), lambda b, i, j: (b, i if per_token else 0, k))
    return pl.BlockSpec((1, lmb, D_MODEL), lambda b, i: (b, i if per_token else 0, k))


def _inproj_body(x_ref, sc_ref, sh_ref, w_ref, wt_ref, o_ref, h_ref, *, n_main):
    j = pl.program_id(2)
    tm = x_ref.shape[1]
    per_token = sc_ref.shape[1] != 1

    @pl.when(j == 0)
    def _():
        step = min(tm, FFN_ROW_CHUNK)
        for r in range(0, tm, step):
            rows = slice(r, r + step)
            sc = sc_ref[0, rows, :] if per_token else sc_ref[0]
            sh = sh_ref[0, rows, :] if per_token else sh_ref[0]
            h = (x_ref[0, rows, :] * (1.0 + sc) + sh).astype(BF16)
            h_ref[rows, :] = h
            o_ref[0, rows, :] = jnp.dot(h, w_ref[0], preferred_element_type=F32)

    @pl.when(jnp.logical_and(j > 0, j < n_main))
    def _():
        o_ref[0] = jnp.dot(h_ref[...], w_ref[0], preferred_element_type=F32)

    @pl.when(j == n_main)
    def _():
        o_ref[0] = jnp.dot(h_ref[...], wt_ref[...], preferred_element_type=F32)


def _inproj(x, mod3, w_in, l, w_tail, tm, tn):
    B, L, _ = x.shape
    n_main = COL_SU // tn
    assert n_main * tn == COL_SU and w_tail.shape == (D_MODEL, tn) and (n_main + 1) * tn == N_PROJ
    return pl.pallas_call(
        functools.partial(_inproj_body, n_main=n_main),
        grid=(B, L // tm, n_main + 1),
        in_specs=[pl.BlockSpec((1, tm, D_MODEL), lambda b, i, j: (b, i, 0)),
                  _mod_spec(mod3, tm, 1, 3),
                  _mod_spec(mod3, tm, 0, 3),
                  pl.BlockSpec((1, D_MODEL, tn), lambda b, i, j: (l, 0, jnp.minimum(j, n_main - 1))),
                  pl.BlockSpec((D_MODEL, tn), lambda b, i, j: (0, 0))],
        out_specs=pl.BlockSpec((1, tm, tn), lambda b, i, j: (b, i, j)),
        out_shape=jax.ShapeDtypeStruct((B, L, N_PROJ), F32),
        scratch_shapes=[pltpu.VMEM((tm, D_MODEL), BF16)],
        compiler_params=_params(("parallel", "parallel", "arbitrary")),
        name="inproj",
    )(x, mod3, mod3, w_in, w_tail)


def _outproj_body(ro_ref, go_ref, y_ref, x_ref, g1_ref, wglu_ref, wout_ref, lnw_ref, lnb_ref, o_ref, *, row_chunk):
    tm = x_ref.shape[1]
    for r in range(0, tm, row_chunk):
        rows = slice(r, r + row_chunk)
        y = y_ref[0, rows, :]
        so = y * jax.nn.sigmoid(_bdot(y, wglu_ref[0]))
        acc = jnp.dot(ro_ref[0, rows, :], wout_ref[0, 0:RET_W, :], preferred_element_type=F32)
        acc += jnp.dot(go_ref[0, rows, :], wout_ref[0, RET_W:RET_W + GDN_W, :], preferred_element_type=F32)
        acc += jnp.dot(so.astype(BF16), wout_ref[0, RET_W + GDN_W:, :], preferred_element_type=F32)
        g1 = g1_ref[0] if g1_ref.shape[1] == 1 else g1_ref[0, rows, :]
        z = ALPHA * x_ref[0, rows, :] + g1 * acc
        o_ref[0, rows, :] = _layernorm(z, lnw_ref[...], lnb_ref[...])


def _outproj(ro, go, y, x, mod3, wglu, wout, l, lnw, lnb, tm):
    B, L, _ = x.shape
    return pl.pallas_call(
        functools.partial(_outproj_body, row_chunk=min(tm, OUTPROJ_ROW_CHUNK)),
        grid=(B, L // tm),
        in_specs=[pl.BlockSpec((1, tm, RET_W), lambda b, i: (b, i, 0)),
                  pl.BlockSpec((1, tm, GDN_W), lambda b, i: (b, i, 0)),
                  pl.BlockSpec((1, tm, SSM_W), lambda b, i: (b, i, 0)),
                  pl.BlockSpec((1, tm, D_MODEL), lambda b, i: (b, i, 0)),
                  _mod_spec(mod3, tm, 2, 2),
                  pl.BlockSpec((1, SSM_W, SSM_W), lambda b, i: (l, 0, 0)),
                  pl.BlockSpec((1, D_MODEL, D_MODEL), lambda b, i: (l, 0, 0)),
                  pl.BlockSpec((1, D_MODEL), lambda b, i: (0, 0)),
                  pl.BlockSpec((1, D_MODEL), lambda b, i: (0, 0))],
        out_specs=pl.BlockSpec((1, tm, D_MODEL), lambda b, i: (b, i, 0)),
        out_shape=jax.ShapeDtypeStruct((B, L, D_MODEL), F32),
        compiler_params=_params(("parallel", "parallel")),
        name="outproj_ln1",
    )(ro, go, y, x, mod3, wglu, wout, lnw, lnb)


def _ffn_body(x_ref, sc_ref, sh_ref, g2_ref, wg_ref, wu_ref, wo_ref, lnw_ref, lnb_ref, o_ref, h_ref):
    f = pl.program_id(2)
    last = pl.num_programs(2) - 1
    tm = x_ref.shape[1]
    chunks = [slice(r, r + min(tm, FFN_ROW_CHUNK)) for r in range(0, tm, min(tm, FFN_ROW_CHUNK))]
    per_token = sc_ref.shape[1] != 1

    def mod(ref, rows):
        return ref[0, rows, :] if per_token else ref[0]

    def contribution(h):
        gate = jnp.dot(h, wg_ref[0, 0], preferred_element_type=F32)
        up = jnp.dot(h, wu_ref[0, 0], preferred_element_type=F32)
        act = (_silu(gate) * up).astype(BF16)
        return jnp.dot(act, wo_ref[0], preferred_element_type=F32)

    @pl.when(f == 0)
    def _():
        for rows in chunks:
            h = (x_ref[0, rows, :] * (1.0 + mod(sc_ref, rows)) + mod(sh_ref, rows)).astype(BF16)
            h_ref[rows, :] = h
            o_ref[0, rows, :] = contribution(h)

    @pl.when(jnp.logical_and(f > 0, f < last))
    def _():
        o_ref[0] += contribution(h_ref[...])

    @pl.when(f == last)
    def _():
        for rows in chunks:
            acc = o_ref[0, rows, :] + contribution(h_ref[rows, :])
            z = ALPHA * x_ref[0, rows, :] + mod(g2_ref, rows) * acc
            o_ref[0, rows, :] = _layernorm(z, lnw_ref[...], lnb_ref[...])


def _ffn(x, mod3, w_in, w_out, l, lnw, lnb, tm, tf):
    B, L, _ = x.shape
    nf = D_FF // tf
    return pl.pallas_call(
        _ffn_body,
        grid=(B, L // tm, nf),
        in_specs=[pl.BlockSpec((1, tm, D_MODEL), lambda b, i, f: (b, i, 0)),
                  _mod_spec(mod3, tm, 4, 3),
                  _mod_spec(mod3, tm, 3, 3),
                  _mod_spec(mod3, tm, 5, 3),
                  pl.BlockSpec((1, 1, D_MODEL, tf), lambda b, i, f: (l, f, 0, 0)),
                  pl.BlockSpec((1, 1, D_MODEL, tf), lambda b, i, f: (l, f + nf, 0, 0)),
                  pl.BlockSpec((1, tf, D_MODEL), lambda b, i, f: (l, f, 0)),
                  pl.BlockSpec((1, D_MODEL), lambda b, i, f: (0, 0)),
                  pl.BlockSpec((1, D_MODEL), lambda b, i, f: (0, 0))],
        out_specs=pl.BlockSpec((1, tm, D_MODEL), lambda b, i, f: (b, i, 0)),
        out_shape=jax.ShapeDtypeStruct((B, L, D_MODEL), F32),
        scratch_shapes=[pltpu.VMEM((tm, D_MODEL), BF16)],
        compiler_params=_params(("parallel", "parallel", "arbitrary")),
        name="ffn_ln2",
    )(x, mod3, mod3, mod3, w_in, w_in, w_out, lnw, lnb)


def _rope_body(cp_ref, sp_ref, cs_ref, ss_ref):
    half = RET_DK // 2
    for c_ref, s_ref, base, step in ((cp_ref, sp_ref, 0, 1), (cs_ref, ss_ref, PAST_LEN, 0)):
        shape = c_ref.shape
        lane = lax.broadcasted_iota(jnp.int32, shape, 1)
        row = lax.broadcasted_iota(jnp.int32, shape, 0)
        freq = jnp.exp(jnp.where(lane < half, lane, lane - half).astype(F32) * (-math.log(ROPE_BASE) / half))
        ang = (row * step + base).astype(F32) * freq
        c_ref[...] = jnp.cos(ang)
        s = jnp.sin(ang)
        s_ref[...] = jnp.where(lane < RET_DK // 2, -s, s)


def _rope_tables(seq):
    return pl.pallas_call(
        _rope_body,
        out_shape=[jax.ShapeDtypeStruct((seq, RET_DK), F32), jax.ShapeDtypeStruct((seq, RET_DK), F32),
                   jax.ShapeDtypeStruct((8, RET_DK), F32), jax.ShapeDtypeStruct((8, RET_DK), F32)],
        name="rope_tables",
    )()


def _rope(x, cos, sin_signed):
    return x * cos + pltpu.roll(x, RET_DK // 2, 1) * sin_signed


def _ret_prompt_body(q_ref, k_ref, v_ref, g_ref, cos_ref, sin_ref, gnw_ref, o_ref, s_ref, S):
    c = pl.program_id(1)
    C = RET_CHUNK

    @pl.when(c == 0)
    def _():
        S[...] = jnp.zeros_like(S)

    cos = cos_ref[...]
    sin = sin_ref[...]
    row = lax.broadcasted_iota(jnp.int32, (C, C), 0).astype(F32)
    col = lax.broadcasted_iota(jnp.int32, (C, C), 1).astype(F32)
    diff = row - col
    for h in range(RET_HEADS):
        lg = RET_LOG_G[h]
        sl = slice(h * RET_DK, (h + 1) * RET_DK)
        dmask = jnp.exp(jnp.where(diff >= 0, lg * diff, -jnp.inf))
        k_decay = jnp.exp(lg * (C - 1 - row))
        q_decay = jnp.exp(lg * (row + 1.0))
        for bt in range(q_ref.shape[0]):
            q = _rope(q_ref[bt, :, sl], cos, sin)
            k = _rope(k_ref[bt, :, sl], cos, sin) * (RET_DK ** -0.5)
            v = v_ref[bt, :, sl]
            scores = _bdot_nt(q, k) * dmask
            intra = _bdot(scores, v)
            kv = _bdot_tn(k * k_decay, v)
            s_prev = S[bt, h]
            cross = _bdot(q * q_decay, s_prev)
            S[bt, h] = s_prev * math.exp(lg * C) + kv
            ro = intra + cross
            mu = jnp.mean(ro, axis=-1, keepdims=True)
            rc = ro - mu
            var = jnp.mean(rc * rc, axis=-1, keepdims=True)
            rn = rc * lax.rsqrt(var + LN_EPS) * gnw_ref[:, sl] * _silu(g_ref[bt, :, sl])
            o_ref[bt, :, sl] = rn.astype(BF16)

    @pl.when(c == pl.num_programs(1) - 1)
    def _():
        s_ref[...] = S[...]


def _ret_prompt(proj, cos, sin, gnw, nbt):
    B, L, _ = proj.shape
    C = RET_CHUNK
    blk = lambda k: pl.BlockSpec((nbt, C, RET_W), lambda b, c: (b, c, k))
    return pl.pallas_call(
        _ret_prompt_body,
        grid=(B // nbt, L // C),
        in_specs=[blk(0), blk(1), blk(2), blk(3),
                  pl.BlockSpec((C, RET_DK), lambda b, c: (c, 0)),
                  pl.BlockSpec((C, RET_DK), lambda b, c: (c, 0)),
                  pl.BlockSpec((1, RET_W), lambda b, c: (0, 0))],
        out_specs=[pl.BlockSpec((nbt, C, RET_W), lambda b, c: (b, c, 0)),
                   pl.BlockSpec((nbt, RET_HEADS, RET_DK, RET_DK), lambda b, c: (b, 0, 0, 0))],
        out_shape=[jax.ShapeDtypeStruct((B, L, RET_W), BF16),
                   jax.ShapeDtypeStruct((B, RET_HEADS, RET_DK, RET_DK), F32)],
        scratch_shapes=[pltpu.VMEM((nbt, RET_HEADS, RET_DK, RET_DK), F32)],
        compiler_params=_params(("parallel", "arbitrary")),
        name="ret_prompt",
    )(proj, proj, proj, proj, cos, sin, gnw)


def _ret_sample_body(q_ref, k_ref, v_ref, g_ref, cos_ref, sin_ref, gnw_ref, dec_ref, s0_ref, o_ref, s_ref, obuf):
    nb = q_ref.shape[0]
    cos = cos_ref[0:1, :]
    sin = sin_ref[0:1, :]
    gdec = dec_ref[0]
    q = _rope(q_ref[...], cos, sin)
    k = _rope(k_ref[...], cos, sin) * (RET_DK ** -0.5)
    v = v_ref[...]
    qk = jnp.sum(q * k, axis=1, keepdims=True)
    qT = q.T
    kT = k.T
    for b in range(nb):
        s0 = s0_ref[0, b, 0]
        qcol = qT[:, b:b + 1]
        kcol = kT[:, b:b + 1]
        vrow = v[b:b + 1, :]
        cross = gdec * jnp.sum(s0 * qcol, axis=0, keepdims=True)
        s_ref[0, b, 0] = s0 * gdec + kcol * vrow
        obuf[b:b + 1, :] = qk[b:b + 1, :] * vrow + cross
    ro = obuf[...]
    mu = jnp.mean(ro, axis=-1, keepdims=True)
    rc = ro - mu
    var = jnp.mean(rc * rc, axis=-1, keepdims=True)
    o_ref[...] = (rc * lax.rsqrt(var + LN_EPS) * gnw_ref[...] * _silu(g_ref[...])).astype(BF16)


def _state_phases(body, n_in, state_pos, carried):
    def phased(*refs):
        p = pl.program_id(0)

        @pl.when(p == 0)
        def _():
            body(*refs)

        @pl.when(p == 1)
        def _():
            refs[state_pos][...] = jnp.zeros(refs[state_pos].shape, F32)

    if carried is None:
        return phased, 2, [], [], {}

    def carried_body(*refs):
        phased(*refs[:n_in], *refs[n_in + 1:])

    return carried_body, 1, [pl.BlockSpec(memory_space=pl.ANY)], [carried], {n_in: 1}


def _ret_sample(proj2, cos, sin, gnw, state, l, carried):
    nb = proj2.shape[0]
    H = RET_HEADS
    dec = jnp.asarray(np.broadcast_to(np.exp(np.array(RET_LOG_G))[:, None, None], (H, 1, RET_DK)), F32)
    hh = lambda p, h: h * (1 - p) + (H - 1) * p
    blk = lambda k: pl.BlockSpec((nb, RET_DK), lambda p, h: (0, k * H + hh(p, h)))
    in_specs = [blk(0), blk(1), blk(2), blk(3),
                pl.BlockSpec((8, RET_DK), lambda p, h: (0, 0)),
                pl.BlockSpec((8, RET_DK), lambda p, h: (0, 0)),
                pl.BlockSpec((1, RET_DK), lambda p, h: (0, hh(p, h))),
                pl.BlockSpec((1, 1, RET_DK), lambda p, h: (hh(p, h), 0, 0)),
                pl.BlockSpec((1, nb, 1, RET_DK, RET_DK), lambda p, h: (l, 0, hh(p, h), 0, 0))]
    body, phases, extra_specs, extra_args, aliases = _state_phases(_ret_sample_body, len(in_specs),
                                                                   len(in_specs) + 1, carried)
    return pl.pallas_call(
        body,
        grid=(phases, H),
        in_specs=in_specs + extra_specs,
        out_specs=[pl.BlockSpec((nb, RET_DK), lambda p, h: (0, hh(p, h))),
                   pl.BlockSpec((1, nb, 1, RET_DK, RET_DK), lambda p, h: (l + p, 0, h, 0, 0))],
        out_shape=[jax.ShapeDtypeStruct((nb, RET_W), BF16),
                   jax.ShapeDtypeStruct(state.shape, F32)],
        scratch_shapes=[pltpu.VMEM((nb, RET_DK), F32)],
        input_output_aliases=aliases,
        compiler_params=_params(("arbitrary", "arbitrary")),
        name="ret_sample",
    )(proj2, proj2, proj2, proj2, cos, sin, gnw, dec, state, *extra_args)


def _gdn_prompt_body(dq_ref, dk_ref, dv_ref, dz_ref, ab_ref, cw_ref, alog_ref, dtb_ref, nw_ref,
                     o_ref, s_ref, conv_ref, xbuf, S):
    c = pl.program_id(1)
    C = GDN_CHUNK
    H = GDN_HEADS
    dk = GDN_DK
    nbt = dq_ref.shape[0]

    @pl.when(c == 0)
    def _():
        S[...] = jnp.zeros_like(S)
        xbuf[:, 0:8, :] = jnp.zeros((nbt, 8, GDN_CONV_C), F32)

    row = lax.broadcasted_iota(jnp.int32, (C, dk), 0)
    lane = lax.broadcasted_iota(jnp.int32, (C, dk), 1)
    tri = lane <= row
    strict = lane < row
    r3 =lax.broadcasted_iota(jnp.int32, (C, 3 * C), 0)
    c3 = lax.broadcasted_iota(jnp.int32, (C, 3 * C), 1) % C
    tril_b = jnp.where(r3 >= c3, 1.0, 0.0).astype(BF16)

    def cumsum_rows(x):
        hi = x.astype(BF16)
        r1 = x - hi.astype(F32)
        mid = r1.astype(BF16)
        lo = (r1 - mid.astype(F32)).astype(BF16)
        return jnp.dot(tril_b, jnp.concatenate([hi, mid, lo], axis=0), preferred_element_type=F32)

    def lanes(x, j):
        return jnp.broadcast_to(x[:, j:j + 1], (x.shape[0], dk))

    cw = cw_ref[...]
    zrow_b = jnp.zeros((C, dk), BF16)

    items = []
    for bt in range(nbt):
        xbuf[bt, 8:8 + C, 0:GDN_W] = dq_ref[bt]
        xbuf[bt, 8:8 + C, GDN_W:2 * GDN_W] = dk_ref[bt]
        xbuf[bt, 8:8 + C, 2 * GDN_W:3 * GDN_W] = dv_ref[bt]

        def conv(off, bt=bt):
            xe = xbuf[bt, :, off:off + dk]
            y = xe[8:8 + C] * cw[CONV_W - 1:CONV_W, off:off + dk]
            for i in range(CONV_W - 1):
                shifted = pltpu.roll(xe, CONV_W - 1 - i, 0)[8:8 + C]
                y = y + shifted * cw[i:i + 1, off:off + dk]
            return _silu(y)

        ab = ab_ref[bt]
        g_c = -jnp.exp(alog_ref[...]) * jax.nn.softplus(ab + dtb_ref[...])
        gc_c = cumsum_rows(g_c)
        beta_c = jax.nn.sigmoid(ab)
        egc_c = jnp.exp(gc_c)
        gl_c = gc_c[C - 1:C, :]
        kdec_c = jnp.exp(gl_c - gc_c)
        egl_c = jnp.exp(gl_c)
        gc_t = jnp.concatenate([gc_c, jnp.zeros_like(gc_c)], axis=0).T

        for h in range(H):
            sl = slice(h * dk, (h + 1) * dk)
            q = conv(h * dk)
            k = conv(GDN_W + h * dk)
            v = conv(2 * GDN_W + h * dk)
            q = q * lax.rsqrt(jnp.sum(q * q, axis=-1, keepdims=True) + L2_EPS) * (dk ** -0.5)
            k = k * lax.rsqrt(jnp.sum(k * k, axis=-1, keepdims=True) + L2_EPS)
            beta = lanes(beta_c, H + h)
            egc = lanes(egc_c, h)
            kb = k * beta
            kq = jnp.concatenate([kb, q], axis=0).astype(BF16)
            k2 = jnp.concatenate([k.astype(BF16), zrow_b], axis=0)
            kkqk = lax.dot_general(kq, k2, NT, preferred_element_type=F32)
            dgc = lanes(gc_c, h) - jnp.broadcast_to(gc_t[h:h + 1, :], (C, dk))
            decay = jnp.exp(jnp.where(tri, dgc, -jnp.inf))
            a = jnp.where(strict, kkqk[0:C] * decay, 0.0)
            attn = (kkqk[C:2 * C] * decay)[:, 0:C].astype(BF16)
            rhs = jnp.concatenate([v * beta, kb * egc], axis=1).astype(BF16)
            items.append(dict(bt=bt, h=h, a=a, rhs=rhs, attn=attn, qd=(q * egc).astype(BF16),
                              kd=(k * lanes(kdec_c, h)).astype(BF16), egl=lanes(egl_c, h)))

    pair_col = lane % C
    eye_pair = (pair_col == row).astype(F32)
    left = lane < C

    def level_mask(s):
        br = row // s
        return (pair_col // s) == jnp.where(br % 2 == 1, br - 1, -1)

    def block_diag(x):
        zero = jnp.zeros_like(x)
        return jnp.concatenate([jnp.where(left, x, zero), jnp.where(left, zero, x)], axis=0)

    pairs = []
    for i in range(0, len(items), 2):
        a_pair = items[i]["a"] + pltpu.roll(items[i + 1]["a"], C, 1)
        pairs.append(dict(a=a_pair, d=eye_pair - jnp.where(level_mask(1), a_pair, 0.0)))
    s_blk = 2
    while s_blk < C:
        mask = level_mask(s_blk)
        for p in pairs:
            d = p["d"]
            sub = block_diag(jnp.where(mask, p["a"], 0.0)).astype(BF16)
            g = jnp.dot(d.astype(BF16), sub, preferred_element_type=F32)
            p["d"] = d - jnp.dot(g.astype(BF16), block_diag(d).astype(BF16), preferred_element_type=F32)
        s_blk *= 2
    for i, p in enumerate(pairs):
        r0, r1 = items[2 * i]["rhs"], items[2 * i + 1]["rhs"]
        zero = jnp.zeros_like(r0)
        rr = jnp.concatenate([jnp.concatenate([r0, zero], axis=1), jnp.concatenate([zero, r1], axis=1)], axis=0)
        sol = jnp.dot(p["d"].astype(BF16), rr, preferred_element_type=F32)
        items[2 * i]["sol"] = sol[:, 0:2 * dk]
        items[2 * i + 1]["sol"] = sol[:, 2 * dk:4 * dk]

    for it in items:
        bt, h = it["bt"], it["h"]
        sl = slice(h * dk, (h + 1) * dk)
        u = it["sol"][:, 0:dk]
        w = it["sol"][:, dk:2 * dk]
        s_prev = S[bt, h]
        wq = jnp.concatenate([w.astype(BF16), it["qd"]], axis=0)
        r = jnp.dot(wq, s_prev.astype(BF16), preferred_element_type=F32)
        v_new = u - r[0:C]
        o = r[C:2 * C] + jnp.dot(it["attn"], v_new.astype(BF16), preferred_element_type=F32)
        S[bt, h] = s_prev * it["egl"] + lax.dot_general(it["kd"], v_new.astype(BF16), TN, preferred_element_type=F32)
        o = o * lax.rsqrt(jnp.mean(o * o, axis=-1, keepdims=True) + RMS_EPS) * nw_ref[...]
        o_ref[bt, :, sl] = (o * _silu(dz_ref[bt, :, sl])).astype(BF16)

    for bt in range(nbt):
        xbuf[bt, 0:8, :] = xbuf[bt, C:C + 8, :]

    @pl.when(c == pl.num_programs(1) - 1)
    def _():
        s_ref[...] = S[...]
        conv_ref[...] = xbuf[:, C + 8 - (CONV_W - 1):C + 8, :]


def _gdn_prompt(proj, cw, alog_bc, dtb_bc, nw, nbt):
    B, L, _ = proj.shape
    C = GDN_CHUNK
    H = GDN_HEADS
    blk = lambda k: pl.BlockSpec((nbt, C, GDN_W), lambda b, c: (b, c, k))
    full = lambda shape: pl.BlockSpec(shape, lambda b, c: tuple(0 for _ in shape))
    return pl.pallas_call(
        _gdn_prompt_body,
        grid=(B // nbt, L // C),
        in_specs=[blk(2), blk(3), blk(4), blk(5),
                  pl.BlockSpec((nbt, C, 128), lambda b, c: (b, c, COL_AB // 128)),
                  full((CONV_W, GDN_CONV_C)), full((1, 128)), full((1, 128)), full((1, GDN_DK))],
        out_specs=[pl.BlockSpec((nbt, C, GDN_W), lambda b, c: (b, c, 0)),
                   pl.BlockSpec((nbt, H, GDN_DK, GDN_DK), lambda b, c: (b, 0, 0, 0)),
                   pl.BlockSpec((nbt, CONV_W - 1, GDN_CONV_C), lambda b, c: (b, 0, 0))],
        out_shape=[jax.ShapeDtypeStruct((B, L, GDN_W), BF16),
                   jax.ShapeDtypeStruct((B, H, GDN_DK, GDN_DK), F32),
                   jax.ShapeDtypeStruct((B, CONV_W - 1, GDN_CONV_C), F32)],
        scratch_shapes=[pltpu.VMEM((nbt, C + 8, GDN_CONV_C), F32),
                        pltpu.VMEM((nbt, H, GDN_DK, GDN_DK), F32)],
        compiler_params=_params(("parallel", "arbitrary")),
        name="gdn_prompt",
    )(proj, proj, proj, proj, proj, cw, alog_bc, dtb_bc, nw)


def _gdn_sample_body(dq_ref, dk_ref, dv_ref, dz_ref, ab_ref, csq_ref, csk_ref, csv_ref, cwq_ref, cwk_ref, cwv_ref,
                     alog_ref, dtb_ref, nw_ref, s0_ref, o_ref, s_ref, nq_ref, nk_ref, nv_ref, obuf):
    h = pl.program_id(1)
    nb = dq_ref.shape[0]
    dk = GDN_DK

    def conv(x_ref, cs_ref, w_ref, n_ref):
        x = x_ref[...]
        y = cs_ref[0] * w_ref[0:1, :] + cs_ref[1] * w_ref[1:2, :] + cs_ref[2] * w_ref[2:3, :] + x * w_ref[3:4, :]
        n_ref[0] = cs_ref[1]
        n_ref[1] = cs_ref[2]
        n_ref[2] = x
        return _silu(y)

    q = conv(dq_ref, csq_ref, cwq_ref, nq_ref)
    k = conv(dk_ref, csk_ref, cwk_ref, nk_ref)
    v = conv(dv_ref, csv_ref, cwv_ref, nv_ref)
    q = q * lax.rsqrt(jnp.sum(q * q, axis=-1, keepdims=True) + L2_EPS) * (dk ** -0.5)
    k = k * lax.rsqrt(jnp.sum(k * k, axis=-1, keepdims=True) + L2_EPS)
    er = lax.broadcasted_iota(jnp.int32, (128, dk), 0)
    ab = ab_ref[...]
    a_bc = jnp.dot(ab, (er == h).astype(F32), precision=HI, preferred_element_type=F32)
    b_bc = jnp.dot(ab, (er == h + GDN_HEADS).astype(F32), precision=HI, preferred_element_type=F32)
    g = -jnp.exp(alog_ref[0]) * jax.nn.softplus(a_bc + dtb_ref[0])
    eg = jnp.exp(g)
    beta = jax.nn.sigmoid(b_bc)
    qk = jnp.sum(q * k, axis=1, keepdims=True)
    qT = q.T
    kT = k.T
    for b in range(nb):
        s0 = s0_ref[0, b, 0]
        qcol = qT[:, b:b + 1]
        kcol = kT[:, b:b + 1]
        ks = jnp.sum(s0 * kcol, axis=0, keepdims=True)
        qs = jnp.sum(s0 * qcol, axis=0, keepdims=True)
        bb = beta[b:b + 1, :]
        egb = eg[b:b + 1, :]
        v_new = bb * v[b:b + 1, :] - bb * egb * ks
        s_ref[0, b, 0] = s0 * egb + kcol * v_new
        obuf[b:b + 1, :] = egb * qs + qk[b:b + 1, :] * v_new
    o = obuf[...]
    o = o * lax.rsqrt(jnp.mean(o * o, axis=-1, keepdims=True) + RMS_EPS) * nw_ref[...]
    o_ref[...] = (o * _silu(dz_ref[...])).astype(BF16)


def _gdn_sample(proj2, cs_t, cw, alog_t, dtb_t, nw, state, l, carried):
    nb = proj2.shape[0]
    H = GDN_HEADS
    dk = GDN_DK
    hh = lambda p, h: h * (1 - p) + (H - 1) * p
    blk = lambda k: pl.BlockSpec((nb, dk), lambda p, h: (0, k * H + hh(p, h)))
    csblk = lambda k: pl.BlockSpec((CONV_W - 1, nb, dk), lambda p, h: (0, 0, k * H + hh(p, h)))
    cwblk = lambda k: pl.BlockSpec((CONV_W, dk), lambda p, h: (0, k * H + hh(p, h)))
    tab = pl.BlockSpec((1, 1, dk), lambda p, h: (hh(p, h), 0, 0))
    nblk = pl.BlockSpec((CONV_W - 1, nb, dk), lambda p, h: (0, 0, hh(p, h)))
    nshape = jax.ShapeDtypeStruct((CONV_W - 1, nb, GDN_W), F32)
    in_specs = [blk(2), blk(3), blk(4), blk(5),
                pl.BlockSpec((nb, 128), lambda p, h: (0, COL_AB // 128)),
                csblk(0), csblk(1), csblk(2), cwblk(0), cwblk(1), cwblk(2),
                tab, tab, pl.BlockSpec((1, dk), lambda p, h: (0, 0)),
                pl.BlockSpec((1, nb, 1, dk, dk), lambda p, h: (l, 0, hh(p, h), 0, 0))]
    body, phases, extra_specs, extra_args, aliases = _state_phases(_gdn_sample_body, len(in_specs),
                                                                   len(in_specs) + 1, carried)
    return pl.pallas_call(
        body,
        grid=(phases, H),
        in_specs=in_specs + extra_specs,
        out_specs=[pl.BlockSpec((nb, dk), lambda p, h: (0, hh(p, h))),
                   pl.BlockSpec((1, nb, 1, dk, dk), lambda p, h: (l + p, 0, h, 0, 0)), nblk, nblk, nblk],
        out_shape=[jax.ShapeDtypeStruct((nb, GDN_W), BF16), jax.ShapeDtypeStruct(state.shape, F32),
                   nshape, nshape, nshape],
        scratch_shapes=[pltpu.VMEM((nb, dk), F32)],
        input_output_aliases=aliases,
        compiler_params=_params(("arbitrary", "arbitrary")),
        name="gdn_sample",
    )(proj2, proj2, proj2, proj2, proj2, cs_t, cs_t, cs_t, cw, cw, cw, alog_t, dtb_t, nw, state, *extra_args)


def _ssm_prep_body(are_ref, aim_ref, ldt_ref, bre_ref, bim_ref, lre_ref, lim_ref, obre_ref, obim_ref):
    ar = are_ref[0]
    ai = aim_ref[0]
    dt = jnp.exp(ldt_ref[0])
    er = jnp.exp(ar * dt)
    lr = er * jnp.cos(ai * dt)
    li = er * jnp.sin(ai * dt)
    lre_ref[0] = lr
    lim_ref[0] = li
    xr = lr - 1.0
    den = ar * ar + ai * ai
    cr = ((xr * ar + li * ai) / den)[:, None, :]
    ci = ((li * ar - xr * ai) / den)[:, None, :]
    br = bre_ref[0]
    bi = bim_ref[0]
    obre_ref[0] = cr * br - ci * bi
    obim_ref[0] = cr * bi + ci * br


def _ssm_prep(a_re, a_im, log_dt, b_re, b_im):
    G, N, P = SSM_GROUPS, SSM_N, SSM_GROUP
    ldt = jnp.broadcast_to(log_dt[:, :, None], (DEPTH, G, N))
    bt_re = jnp.transpose(b_re, (0, 1, 3, 2))
    bt_im = jnp.transpose(b_im, (0, 1, 3, 2))
    s2 = pl.BlockSpec((1, G, N), lambda l: (l, 0, 0))
    s3 = pl.BlockSpec((1, G, P, N), lambda l: (l, 0, 0, 0))
    return pl.pallas_call(
        _ssm_prep_body,
        grid=(DEPTH,),
        in_specs=[s2, s2, s2, s3, s3],
        out_specs=[s2, s2, s3, s3],
        out_shape=[jax.ShapeDtypeStruct((DEPTH, G, N), F32), jax.ShapeDtypeStruct((DEPTH, G, N), F32),
                   jax.ShapeDtypeStruct((DEPTH, G, P, N), F32), jax.ShapeDtypeStruct((DEPTH, G, P, N), F32)],
        name="ssm_prep",
    )(a_re, a_im, ldt, bt_re, bt_im)


def _ssm_tile_blockdiag(re, im):
    x = jnp.stack([re, im]).reshape(2, SSM_TILES, SSM_TILE_G, SSM_GROUP, SSM_N)
    eye = jnp.eye(SSM_TILE_G, dtype=x.dtype)
    y = x[:, :, :, :, None, :] * eye[None, None, :, None, :, None]
    y = jnp.transpose(y, (1, 2, 3, 0, 4, 5))
    return y.reshape(SSM_TILES, SSM_TILE_G * SSM_GROUP, SSM_TILE_W)


def _ssm_state_to_tiles(re, im):
    B = re.shape[0]
    x = jnp.stack([re, im], axis=1).reshape(B, 2, SSM_TILES, SSM_HALF)
    return jnp.transpose(x, (0, 2, 1, 3)).reshape(B, SSM_TILES * SSM_TILE_W)


def _ssm_tiles_to_state(x):
    B = x.shape[0]
    x = jnp.transpose(x.reshape(B, SSM_TILES, 2, SSM_HALF), (0, 2, 1, 3)).reshape(B, 2, SSM_GROUPS, SSM_N)
    return x[:, 0], x[:, 1]


def _cmul(ar, ai, br, bi):
    return ar * br - ai * bi, ar * bi + ai * br


def _s5_prompt_body(u_ref, bt_ref, ct_ref, lam_ref, d_ref, y_ref, hl_ref, buf, ptab):
    L = u_ref.shape[1]
    nseg = L // SSM_SEG
    ns = SSM_HALF // 128
    bmat = bt_ref[0].astype(BF16)
    for i in range(nseg):
        bu = jnp.dot(u_ref[0, i * SSM_SEG:(i + 1) * SSM_SEG, :].astype(BF16), bmat, preferred_element_type=F32)
        for s in range(2 * ns):
            buf[s, i * SSM_PITCH:i * SSM_PITCH + SSM_SEG, :] = bu[:, s * 128:(s + 1) * 128]

    lam = lam_ref[0]
    a_re = [jnp.broadcast_to(lam[:, s * 128:(s + 1) * 128], (nseg, 128)) for s in range(ns)]
    a_im = [jnp.broadcast_to(lam[:, SSM_HALF + s * 128:SSM_HALF + (s + 1) * 128], (nseg, 128)) for s in range(ns)]

    def scan_step(t, carry):
        out = []
        for s in range(ns):
            hr, hi = carry[2 * s], carry[2 * s + 1]
            pr, pi = _cmul(a_re[s], a_im[s], hr, hi)
            hr = pr + buf[s, pl.ds(t, nseg, stride=SSM_PITCH), :]
            hi = pi + buf[ns + s, pl.ds(t, nseg, stride=SSM_PITCH), :]
            buf[s, pl.ds(t, nseg, stride=SSM_PITCH), :] = hr
            buf[ns + s, pl.ds(t, nseg, stride=SSM_PITCH), :] = hi
            out += [hr, hi]
        return tuple(out)

    zero = jnp.zeros((nseg, 128), F32)
    ends = lax.fori_loop(0, SSM_SEG, scan_step, tuple(zero for _ in range(2 * ns)), unroll=SSM_UNROLL)

    sub = lax.broadcasted_iota(jnp.int32, (nseg, 128), 0)
    carries = []
    for s in range(ns):
        pr, pi = a_re[s], a_im[s]
        for _ in range(int(math.log2(SSM_SEG))):
            pr, pi = _cmul(pr, pi, pr, pi)
        xr, xi = ends[2 * s], ends[2 * s + 1]
        shift = 1
        while shift < nseg:
            sr = jnp.where(sub >= shift, pltpu.roll(xr, shift, 0), 0.0)
            si = jnp.where(sub >= shift, pltpu.roll(xi, shift, 0), 0.0)
            mr, mi = _cmul(pr, pi, sr, si)
            xr, xi = xr + mr, xi + mi
            pr, pi = _cmul(pr, pi, pr, pi)
            shift *= 2
        hl_ref[0, 0, :, s * 128:(s + 1) * 128] = xr[nseg - 1:nseg, :]
        hl_ref[0, 0, :, SSM_HALF + s * 128:SSM_HALF + (s + 1) * 128] = xi[nseg - 1:nseg, :]
        carries += [jnp.where(sub >= 1, pltpu.roll(xr, 1, 0), 0.0), jnp.where(sub >= 1, pltpu.roll(xi, 1, 0), 0.0)]

    tile_id = pl.program_id(1)

    @pl.when(pl.program_id(0) == 0)
    def _():
        lam_rows = [(lam[:, s * 128:(s + 1) * 128], lam[:, SSM_HALF + s * 128:SSM_HALF + (s + 1) * 128])
                    for s in range(ns)]

        def tab_step(t, pw):
            out = []
            for s in range(ns):
                pr, pi = pw[2 * s], pw[2 * s + 1]
                ptab[tile_id, s, pl.ds(t, 1), :] = pr
                ptab[tile_id, ns + s, pl.ds(t, 1), :] = pi
                out += list(_cmul(pr, pi, lam_rows[s][0], lam_rows[s][1]))
            return tuple(out)

        lax.fori_loop(0, SSM_SEG, tab_step, tuple(x for pair in lam_rows for x in pair), unroll=SSM_UNROLL)

    def fix_step(t, carry):
        for s in range(ns):
            pr = jnp.broadcast_to(ptab[tile_id, s, pl.ds(t, 1), :], (nseg, 128))
            pi = jnp.broadcast_to(ptab[tile_id, ns + s, pl.ds(t, 1), :], (nseg, 128))
            fr, fi = _cmul(pr, pi, carries[2 * s], carries[2 * s + 1])
            buf[s, pl.ds(t, nseg, stride=SSM_PITCH), :] += fr
            buf[ns + s, pl.ds(t, nseg, stride=SSM_PITCH), :] += fi
        return carry

    lax.fori_loop(0, SSM_SEG, fix_step, 0, unroll=SSM_UNROLL)

    cmat = ct_ref[0]
    cmat = jnp.concatenate([cmat[:, 0:SSM_HALF], -cmat[:, SSM_HALF:]], axis=1).astype(BF16)
    for i in range(nseg):
        rows = slice(i * SSM_PITCH, i * SSM_PITCH + SSM_SEG)
        h = jnp.concatenate([buf[s, rows, :].astype(BF16) for s in range(2 * ns)], axis=1)
        y = lax.dot_general(h, cmat, NT, preferred_element_type=F32)
        u = u_ref[0, i * SSM_SEG:(i + 1) * SSM_SEG, :]
        y_ref[0, i * SSM_SEG:(i + 1) * SSM_SEG, :] = jax.nn.gelu(y + d_ref[...] * u)


def _s5_prompt(proj, bt, ct, lam, d):
    B, L, _ = proj.shape
    nseg = L // SSM_SEG
    tile = lambda shape: pl.BlockSpec(shape, lambda b, j: (j,) + tuple(0 for _ in shape[1:]))
    return pl.pallas_call(
        _s5_prompt_body,
        grid=(B, SSM_TILES),
        in_specs=[pl.BlockSpec((1, L, 128), lambda b, j: (b, 0, COL_SU // 128 + j)),
                  tile((1, 128, SSM_TILE_W)), tile((1, 128, SSM_TILE_W)), tile((1, 1, SSM_TILE_W)),
                  pl.BlockSpec((1, 128), lambda b, j: (0, j))],
        out_specs=[pl.BlockSpec((1, L, 128), lambda b, j: (b, 0, j)),
                   pl.BlockSpec((1, 1, 1, SSM_TILE_W), lambda b, j: (b, j, 0, 0))],
        out_shape=[jax.ShapeDtypeStruct((B, L, SSM_W), F32),
                   jax.ShapeDtypeStruct((B, SSM_TILES, 1, SSM_TILE_W), F32)],
        scratch_shapes=[pltpu.VMEM((SSM_TILE_W // 128, nseg * SSM_PITCH, 128), F32),
                        pltpu.VMEM((SSM_TILES, SSM_TILE_W // 128, SSM_SEG, 128), F32)],
        compiler_params=_params(("arbitrary", "arbitrary")),
        name="s5_prompt",
    )(proj, bt, ct, lam, d)


def _s5_sample_body(u_ref, bt_ref, ct_ref, lam_ref, d_ref, h0_ref, y_ref, h_ref):
    u = u_ref[...]
    bu = jnp.dot(u, bt_ref[0], precision=HI, preferred_element_type=F32)
    lam = lam_ref[0]
    lr, li = lam[:, 0:SSM_HALF], lam[:, SSM_HALF:]
    h0r, h0i = h0_ref[:, 0:SSM_HALF], h0_ref[:, SSM_HALF:]
    pr, pi = _cmul(lr, li, h0r, h0i)
    hr = pr + bu[:, 0:SSM_HALF]
    hi = pi + bu[:, SSM_HALF:]
    h_ref[:, 0:SSM_HALF] = hr
    h_ref[:, SSM_HALF:] = hi
    ct = ct_ref[0]
    y = lax.dot_general(hr, ct[:, 0:SSM_HALF], NT, precision=HI, preferred_element_type=F32)
    y -= lax.dot_general(hi, ct[:, SSM_HALF:], NT, precision=HI, preferred_element_type=F32)
    y_ref[...] = jax.nn.gelu(y + d_ref[...] * u)


def _s5_sample(proj2, bt, ct, lam, d, h0):
    nb = proj2.shape[0]
    tile = lambda shape: pl.BlockSpec(shape, lambda j: (j,) + tuple(0 for _ in shape[1:]))
    hblk = pl.BlockSpec((nb, SSM_TILE_W), lambda j: (0, j))
    return pl.pallas_call(
        _s5_sample_body,
        grid=(SSM_TILES,),
        in_specs=[pl.BlockSpec((nb, 128), lambda j: (0, COL_SU // 128 + j)),
                  tile((1, 128, SSM_TILE_W)), tile((1, 128, SSM_TILE_W)), tile((1, 1, SSM_TILE_W)),
                  pl.BlockSpec((1, 128), lambda j: (0, j)), hblk],
        out_specs=[pl.BlockSpec((nb, 128), lambda j: (0, j)), hblk],
        out_shape=[jax.ShapeDtypeStruct((nb, SSM_W), F32), jax.ShapeDtypeStruct(h0.shape, F32)],
        compiler_params=_params(("parallel",)),
        name="s5_sample",
    )(proj2, bt, ct, lam, d, h0)


def kernel(x_prompt, x_sample, c_prompt, c_sample, state_ret, state_gdn, state_conv, state_ssm_re, state_ssm_im, w_mod, b_mod, w_in, conv_w, ret_gn_w, gdn_a_log, gdn_dt_bias, gdn_norm_w, ssm_a_re, ssm_a_im, ssm_log_dt, ssm_b_re, ssm_b_im, ssm_c_re, ssm_c_im, ssm_d, ssm_w_glu, w_out, ln1_w, ln1_b, w_ffn_in, w_ffn_out, ln2_w, ln2_b):
    Bp, Lp, _ = x_prompt.shape
    Bs = x_sample.shape[0]
    assert x_sample.shape[1] == 1 and Bs % 8 == 0

    pad_rows = (-(Bp + Bs)) % 8
    c_all = jnp.concatenate([c_prompt, c_sample, jnp.zeros((pad_rows, D_MODEL), F32)], axis=0)
    mod_all = _modulation(c_all, w_mod, b_mod)
    lam_re, lam_im, bbar_re, bbar_im = _ssm_prep(ssm_a_re, ssm_a_im, ssm_log_dt, ssm_b_re, ssm_b_im)
    cos_p, sin_p, cos_s, sin_s = _rope_tables(Lp)

    w_in_b = w_in.astype(BF16)
    w_out_b = _to_bf16(w_out, rows=512)
    w_glu_b = ssm_w_glu.astype(BF16)
    w_ffi_b = _to_bf16_col_tiles(w_ffn_in, rows=256, tc=FFN_TF)
    w_ffo_b = _to_bf16(w_ffn_out, rows=512)

    yp = x_prompt
    ys = x_sample.reshape(1, Bs, D_MODEL)
    outs_p, outs_s = [], []
    ret_s = gdn_s = None
    for l in range(DEPTH):
        w_tail = jnp.concatenate(
            [w_in_b[l, :, COL_SU + 2 * GDN_HEADS:], w_in_b[l, :, COL_SU:COL_SU + 2 * GDN_HEADS],
             jnp.zeros((D_MODEL, N_PROJ - N_IN), BF16)], axis=1)
        mod_p = mod_all[l, :Bp][:, None, :]
        mod_s = mod_all[l, Bp:Bp + Bs][None]
        gnw = ret_gn_w[l][None, :]
        nw = gdn_norm_w[l][None, :]
        alog_bc = jnp.pad(gdn_a_log[l], (0, 128 - GDN_HEADS))[None, :]
        dtb_bc = jnp.pad(gdn_dt_bias[l], (0, 128 - GDN_HEADS))[None, :]
        alog_t = jnp.broadcast_to(gdn_a_log[l][:, None, None], (GDN_HEADS, 1, GDN_DK))
        dtb_t = jnp.broadcast_to(gdn_dt_bias[l][:, None, None], (GDN_HEADS, 1, GDN_DK))
        ssm_bt = _ssm_tile_blockdiag(bbar_re[l], bbar_im[l])
        ssm_ct = _ssm_tile_blockdiag(ssm_c_re[l], ssm_c_im[l])
        ssm_lam = jnp.concatenate([lam_re[l].reshape(SSM_TILES, 1, SSM_HALF),
                                   lam_im[l].reshape(SSM_TILES, 1, SSM_HALF)], axis=-1)
        ssm_dl = ssm_d[l][None, :]
        lnw1, lnb1 = ln1_w[l][None, :], ln1_b[l][None, :]
        lnw2, lnb2 = ln2_w[l][None, :], ln2_b[l][None, :]

        proj = _inproj(yp, mod_p, w_in_b, l, w_tail, tm=1024, tn=PROJ_TN)
        ro, ret_p = _ret_prompt(proj, cos_p, sin_p, gnw, nbt=4)
        go, gdn_p, conv_p = _gdn_prompt(proj, conv_w[l], alog_bc, dtb_bc, nw, nbt=4)
        so, hl = _s5_prompt(proj, ssm_bt, ssm_ct, ssm_lam, ssm_dl)
        re_p, im_p = _ssm_tiles_to_state(hl.reshape(Bp, SSM_TILES * SSM_TILE_W))
        x1 = _outproj(ro, go, so, yp, mod_p, w_glu_b, w_out_b, l, lnw1, lnb1, tm=512)
        yp = _ffn(x1, mod_p, w_ffi_b, w_ffo_b, l, lnw2, lnb2, tm=512, tf=FFN_TF)
        outs_p.append((ret_p, gdn_p, conv_p, re_p, im_p))

        proj = _inproj(ys, mod_s, w_in_b, l, w_tail, tm=Bs, tn=PROJ_TN)
        proj2 = proj.reshape(Bs, N_PROJ)
        ro, ret_s = _ret_sample(proj2, cos_s, sin_s, gnw, state_ret, l, ret_s)
        cs_t = jnp.transpose(state_conv[l], (1, 0, 2))
        go, gdn_s, nq, nk, nv = _gdn_sample(proj2, cs_t, conv_w[l], alog_t, dtb_t, nw, state_gdn, l, gdn_s)
        conv_s = jnp.transpose(jnp.concatenate([nq, nk, nv], axis=-1), (1, 0, 2))
        h0 = _ssm_state_to_tiles(state_ssm_re[l], state_ssm_im[l])
        so, hn = _s5_sample(proj2, ssm_bt, ssm_ct, ssm_lam, ssm_dl, h0)
        re_s, im_s = _ssm_tiles_to_state(hn)
        x1 = _outproj(ro[None], go[None], so[None], ys, mod_s, w_glu_b, w_out_b, l, lnw1, lnb1, tm=Bs)
        ys = _ffn(x1, mod_s, w_ffi_b, w_ffo_b, l, lnw2, lnb2, tm=Bs, tf=FFN_TF)
        outs_s.append((conv_s, re_s, im_s))

    ret_p, gdn_p, conv_p, re_p, im_p = [jnp.stack(t) for t in zip(*outs_p)]
    conv_s, re_s, im_s = [jnp.stack(t) for t in zip(*outs_s)]
    return (yp, ys.reshape(Bs, 1, D_MODEL), ret_p, gdn_p, conv_p, re_p, im_p,
            ret_s, gdn_s, conv_s, re_s, im_s)
```

```python
import functools
import math

import numpy as np
import jax
import jax.numpy as jnp
from jax import lax
from jax.experimental import pallas as pl
from jax.experimental.pallas import tpu as pltpu

F32 = jnp.float32
BF16 = jnp.bfloat16

D_MODEL = 2048
DEPTH = 2
PAST_LEN = 16384
RET_HEADS = 4
RET_DK = 128
RET_W = 512
RET_CHUNK = 128
ROPE_BASE = 10000.0
GDN_HEADS = 8
GDN_DK = 128
GDN_W = 1024
GDN_CHUNK = 64
CONV_W = 4
GDN_CONV_C = 3072
SSM_W = 512
SSM_GROUP = 16
SSM_GROUPS = 32
SSM_N = 64
D_FF = 5632
ALPHA = (2 * DEPTH) ** 0.25
LN_EPS = 1e-5
RMS_EPS = 1e-6
L2_EPS = 1e-6

COL_SU = 6144
COL_AB = 6656
N_IN = 6672
PROJ_TN = 768
N_PROJ = COL_SU + PROJ_TN

SSM_TILES = 4
SSM_TILE_G = 8
SSM_HALF = SSM_TILE_G * SSM_N
SSM_TILE_W = 2 * SSM_HALF
SSM_SEG = 256
SSM_PITCH = 260
SSM_UNROLL = 4

VMEM_LIMIT = 56 * 1024 * 1024
OUTPROJ_ROW_CHUNK = 256
FFN_ROW_CHUNK = 256
FFN_TF = 512

RET_LOG_G = [math.log1p(-(2.0 ** (-5.0 - h))) for h in range(RET_HEADS)]

NT = (((1,), (1,)), ((), ()))
TN = (((0,), (0,)), ((), ()))
HI = lax.Precision.HIGHEST


def _params(sem):
    return pltpu.CompilerParams(dimension_semantics=sem, vmem_limit_bytes=VMEM_LIMIT)


def _silu(x):
    return x * jax.nn.sigmoid(x)


def _bdot(a, b):
    return jnp.dot(a.astype(BF16), b.astype(BF16), preferred_element_type=F32)


def _bdot_nt(a, b):
    return lax.dot_general(a.astype(BF16), b.astype(BF16), NT, preferred_element_type=F32)


def _bdot_tn(a, b):
    return lax.dot_general(a.astype(BF16), b.astype(BF16), TN, preferred_element_type=F32)


def _layernorm(z, w, b):
    mu = jnp.mean(z, axis=-1, keepdims=True)
    zc = z - mu
    var = jnp.mean(zc * zc, axis=-1, keepdims=True)
    return zc * lax.rsqrt(var + LN_EPS) * w + b


def _cast_body(x_ref, o_ref):
    o_ref[...] = x_ref[...].astype(o_ref.dtype)


def _to_bf16(w, rows):
    depth, r, n = w.shape
    spec = pl.BlockSpec((1, rows, n), lambda l, i: (l, i, 0))
    return pl.pallas_call(
        _cast_body,
        grid=(depth, r // rows),
        in_specs=[spec],
        out_specs=spec,
        out_shape=jax.ShapeDtypeStruct(w.shape, BF16),
        compiler_params=_params(("parallel", "parallel")),
        name="to_bf16",
    )(w)


def _cast_col_tiles_body(x_ref, o_ref):
    tc = o_ref.shape[3]
    for t in range(o_ref.shape[1]):
        o_ref[0, t] = x_ref[0, :, t * tc:(t + 1) * tc].astype(o_ref.dtype)


def _to_bf16_col_tiles(w, rows, tc):
    depth, r, n = w.shape
    nt = n // tc
    return pl.pallas_call(
        _cast_col_tiles_body,
        grid=(depth, r // rows),
        in_specs=[pl.BlockSpec((1, rows, n), lambda l, i: (l, i, 0))],
        out_specs=pl.BlockSpec((1, nt, rows, tc), lambda l, i: (l, 0, i, 0)),
        out_shape=jax.ShapeDtypeStruct((depth, nt, r, tc), BF16),
        compiler_params=_params(("parallel", "parallel")),
        name="to_bf16_col_tiles",
    )(w)


def _mod_body(c_ref, w_ref, b_ref, o_ref):
    c = c_ref[...]
    o_ref[0] = _bdot(_silu(c), w_ref[0]) + b_ref[0]


def _modulation(c_all, w_mod, b_mod):
    rows = c_all.shape[0]
    tn = 1024
    n = 6 * D_MODEL
    return pl.pallas_call(
        _mod_body,
        grid=(DEPTH, n // tn),
        in_specs=[pl.BlockSpec((rows, D_MODEL), lambda l, j: (0, 0)),
                  pl.BlockSpec((1, D_MODEL, tn), lambda l, j: (l, 0, j)),
                  pl.BlockSpec((1, 1, tn), lambda l, j: (l, 0, j))],
        out_specs=pl.BlockSpec((1, rows, tn), lambda l, j: (l, 0, j)),
        out_shape=jax.ShapeDtypeStruct((DEPTH, rows, n), F32),
        compiler_params=_params(("parallel", "parallel")),
        name="modulation",
    )(c_all, w_mod, b_mod.reshape(DEPTH, 1, n))


def _mod_spec(mod3, tm, k, grid_rank):
    per_token = mod3.shape[1] != 1
    lmb = tm if per_token else 1
    if grid_rank == 3:
        return pl.BlockSpec((1, lmb, D_MODEL), lambda b, i, j: (b, i if per_token else 0, k))
    return pl.BlockSpec((1, lmb, D_MODEL), lambda b, i: (b, i if per_token else 0, k))


def _inproj_body(x_ref, sc_ref, sh_ref, w_ref, wt_ref, o_ref, h_ref, *, n_main):
    j = pl.program_id(2)
    tm = x_ref.shape[1]
    per_token = sc_ref.shape[1] != 1

    @pl.when(j == 0)
    def _():
        step = min(tm, FFN_ROW_CHUNK)
        for r in range(0, tm, step):
            rows = slice(r, r + step)
            sc = sc_ref[0, rows, :] if per_token else sc_ref[0]
            sh = sh_ref[0, rows, :] if per_token else sh_ref[0]
            h = (x_ref[0, rows, :] * (1.0 + sc) + sh).astype(BF16)
            h_ref[rows, :] = h
            o_ref[0, rows, :] = jnp.dot(h, w_ref[0], preferred_element_type=F32)

    @pl.when(jnp.logical_and(j > 0, j < n_main))
    def _():
        o_ref[0] = jnp.dot(h_ref[...], w_ref[0], preferred_element_type=F32)

    @pl.when(j == n_main)
    def _():
        o_ref[0] = jnp.dot(h_ref[...], wt_ref[...], preferred_element_type=F32)


def _inproj(x, mod3, w_in, l, w_tail, tm, tn):
    B, L, _ = x.shape
    n_main = COL_SU // tn
    assert n_main * tn == COL_SU and w_tail.shape == (D_MODEL, tn) and (n_main + 1) * tn == N_PROJ
    return pl.pallas_call(
        functools.partial(_inproj_body, n_main=n_main),
        grid=(B, L // tm, n_main + 1),
        in_specs=[pl.BlockSpec((1, tm, D_MODEL), lambda b, i, j: (b, i, 0)),
                  _mod_spec(mod3, tm, 1, 3),
                  _mod_spec(mod3, tm, 0, 3),
                  pl.BlockSpec((1, D_MODEL, tn), lambda b, i, j: (l, 0, jnp.minimum(j, n_main - 1))),
                  pl.BlockSpec((D_MODEL, tn), lambda b, i, j: (0, 0))],
        out_specs=pl.BlockSpec((1, tm, tn), lambda b, i, j: (b, i, j)),
        out_shape=jax.ShapeDtypeStruct((B, L, N_PROJ), F32),
        scratch_shapes=[pltpu.VMEM((tm, D_MODEL), BF16)],
        compiler_params=_params(("parallel", "parallel", "arbitrary")),
        name="inproj",
    )(x, mod3, mod3, w_in, w_tail)


def _outproj_body(ro_ref, go_ref, y_ref, x_ref, g1_ref, wglu_ref, wout_ref, lnw_ref, lnb_ref, o_ref, *, row_chunk):
    tm = x_ref.shape[1]
    for r in range(0, tm, row_chunk):
        rows = slice(r, r + row_chunk)
        y = y_ref[0, rows, :]
        so = y * jax.nn.sigmoid(_bdot(y, wglu_ref[0]))
        acc = jnp.dot(ro_ref[0, rows, :], wout_ref[0, 0:RET_W, :], preferred_element_type=F32)
        acc += jnp.dot(go_ref[0, rows, :], wout_ref[0, RET_W:RET_W + GDN_W, :], preferred_element_type=F32)
        acc += jnp.dot(so.astype(BF16), wout_ref[0, RET_W + GDN_W:, :], preferred_element_type=F32)
        g1 = g1_ref[0] if g1_ref.shape[1] == 1 else g1_ref[0, rows, :]
        z = ALPHA * x_ref[0, rows, :] + g1 * acc
        o_ref[0, rows, :] = _layernorm(z, lnw_ref[...], lnb_ref[...])


def _outproj(ro, go, y, x, mod3, wglu, wout, l, lnw, lnb, tm):
    B, L, _ = x.shape
    return pl.pallas_call(
        functools.partial(_outproj_body, row_chunk=min(tm, OUTPROJ_ROW_CHUNK)),
        grid=(B, L // tm),
        in_specs=[pl.BlockSpec((1, tm, RET_W), lambda b, i: (b, i, 0)),
                  pl.BlockSpec((1, tm, GDN_W), lambda b, i: (b, i, 0)),
                  pl.BlockSpec((1, tm, SSM_W), lambda b, i: (b, i, 0)),
                  pl.BlockSpec((1, tm, D_MODEL), lambda b, i: (b, i, 0)),
                  _mod_spec(mod3, tm, 2, 2),
                  pl.BlockSpec((1, SSM_W, SSM_W), lambda b, i: (l, 0, 0)),
                  pl.BlockSpec((1, D_MODEL, D_MODEL), lambda b, i: (l, 0, 0)),
                  pl.BlockSpec((1, D_MODEL), lambda b, i: (0, 0)),
                  pl.BlockSpec((1, D_MODEL), lambda b, i: (0, 0))],
        out_specs=pl.BlockSpec((1, tm, D_MODEL), lambda b, i: (b, i, 0)),
        out_shape=jax.ShapeDtypeStruct((B, L, D_MODEL), F32),
        compiler_params=_params(("parallel", "parallel")),
        name="outproj_ln1",
    )(ro, go, y, x, mod3, wglu, wout, lnw, lnb)


def _ffn_body(x_ref, sc_ref, sh_ref, g2_ref, wg_ref, wu_ref, wo_ref, lnw_ref, lnb_ref, o_ref, h_ref):
    f = pl.program_id(2)
    last = pl.num_programs(2) - 1
    tm = x_ref.shape[1]
    chunks = [slice(r, r + min(tm, FFN_ROW_CHUNK)) for r in range(0, tm, min(tm, FFN_ROW_CHUNK))]
    per_token = sc_ref.shape[1] != 1

    def mod(ref, rows):
        return ref[0, rows, :] if per_token else ref[0]

    def contribution(h):
        gate = jnp.dot(h, wg_ref[0, 0], preferred_element_type=F32)
        up = jnp.dot(h, wu_ref[0, 0], preferred_element_type=F32)
        act = (_silu(gate) * up).astype(BF16)
        return jnp.dot(act, wo_ref[0], preferred_element_type=F32)

    @pl.when(f == 0)
    def _():
        for rows in chunks:
            h = (x_ref[0, rows, :] * (1.0 + mod(sc_ref, rows)) + mod(sh_ref, rows)).astype(BF16)
            h_ref[rows, :] = h
            o_ref[0, rows, :] = contribution(h)

    @pl.when(jnp.logical_and(f > 0, f < last))
    def _():
        o_ref[0] += contribution(h_ref[...])

    @pl.when(f == last)
    def _():
        for rows in chunks:
            acc = o_ref[0, rows, :] + contribution(h_ref[rows, :])
            z = ALPHA * x_ref[0, rows, :] + mod(g2_ref, rows) * acc
            o_ref[0, rows, :] = _layernorm(z, lnw_ref[...], lnb_ref[...])


def _ffn(x, mod3, w_in, w_out, l, lnw, lnb, tm, tf):
    B, L, _ = x.shape
    nf = D_FF // tf
    return pl.pallas_call(
        _ffn_body,
        grid=(B, L // tm, nf),
        in_specs=[pl.BlockSpec((1, tm, D_MODEL), lambda b, i, f: (b, i, 0)),
                  _mod_spec(mod3, tm, 4, 3),
                  _mod_spec(mod3, tm, 3, 3),
                  _mod_spec(mod3, tm, 5, 3),
                  pl.BlockSpec((1, 1, D_MODEL, tf), lambda b, i, f: (l, f, 0, 0)),
                  pl.BlockSpec((1, 1, D_MODEL, tf), lambda b, i, f: (l, f + nf, 0, 0)),
                  pl.BlockSpec((1, tf, D_MODEL), lambda b, i, f: (l, f, 0)),
                  pl.BlockSpec((1, D_MODEL), lambda b, i, f: (0, 0)),
                  pl.BlockSpec((1, D_MODEL), lambda b, i, f: (0, 0))],
        out_specs=pl.BlockSpec((1, tm, D_MODEL), lambda b, i, f: (b, i, 0)),
        out_shape=jax.ShapeDtypeStruct((B, L, D_MODEL), F32),
        scratch_shapes=[pltpu.VMEM((tm, D_MODEL), BF16)],
        compiler_params=_params(("parallel", "parallel", "arbitrary")),
        name="ffn_ln2",
    )(x, mod3, mod3, mod3, w_in, w_in, w_out, lnw, lnb)


def _rope_body(cp_ref, sp_ref, cs_ref, ss_ref):
    half = RET_DK // 2
    for c_ref, s_ref, base, step in ((cp_ref, sp_ref, 0, 1), (cs_ref, ss_ref, PAST_LEN, 0)):
        shape = c_ref.shape
        lane = lax.broadcasted_iota(jnp.int32, shape, 1)
        row = lax.broadcasted_iota(jnp.int32, shape, 0)
        freq = jnp.exp(jnp.where(lane < half, lane, lane - half).astype(F32) * (-math.log(ROPE_BASE) / half))
        ang = (row * step + base).astype(F32) * freq
        c_ref[...] = jnp.cos(ang)
        s = jnp.sin(ang)
        s_ref[...] = jnp.where(lane < RET_DK // 2, -s, s)


def _rope_tables(seq):
    return pl.pallas_call(
        _rope_body,
        out_shape=[jax.ShapeDtypeStruct((seq, RET_DK), F32), jax.ShapeDtypeStruct((seq, RET_DK), F32),
                   jax.ShapeDtypeStruct((8, RET_DK), F32), jax.ShapeDtypeStruct((8, RET_DK), F32)],
        name="rope_tables",
    )()


def _rope(x, cos, sin_signed):
    return x * cos + pltpu.roll(x, RET_DK // 2, 1) * sin_signed


def _ret_prompt_body(q_ref, k_ref, v_ref, g_ref, cos_ref, sin_ref, gnw_ref, o_ref, s_ref, S):
    c = pl.program_id(1)
    C = RET_CHUNK

    @pl.when(c == 0)
    def _():
        S[...] = jnp.zeros_like(S)

    cos = cos_ref[...]
    sin = sin_ref[...]
    row = lax.broadcasted_iota(jnp.int32, (C, C), 0).astype(F32)
    col = lax.broadcasted_iota(jnp.int32, (C, C), 1).astype(F32)
    diff = row - col
    for h in range(RET_HEADS):
        lg = RET_LOG_G[h]
        sl = slice(h * RET_DK, (h + 1) * RET_DK)
        dmask = jnp.exp(jnp.where(diff >= 0, lg * diff, -jnp.inf))
        k_decay = jnp.exp(lg * (C - 1 - row))
        q_decay = jnp.exp(lg * (row + 1.0))
        for bt in range(q_ref.shape[0]):
            q = _rope(q_ref[bt, :, sl], cos, sin)
            k = _rope(k_ref[bt, :, sl], cos, sin) * (RET_DK ** -0.5)
            v = v_ref[bt, :, sl]
            scores = _bdot_nt(q, k) * dmask
            intra = _bdot(scores, v)
            kv = _bdot_tn(k * k_decay, v)
            s_prev = S[bt, h]
            cross = _bdot(q * q_decay, s_prev)
            S[bt, h] = s_prev * math.exp(lg * C) + kv
            ro = intra + cross
            mu = jnp.mean(ro, axis=-1, keepdims=True)
            rc = ro - mu
            var = jnp.mean(rc * rc, axis=-1, keepdims=True)
            rn = rc * lax.rsqrt(var + LN_EPS) * gnw_ref[:, sl] * _silu(g_ref[bt, :, sl])
            o_ref[bt, :, sl] = rn.astype(BF16)

    @pl.when(c == pl.num_programs(1) - 1)
    def _():
        s_ref[...] = S[...]


def _ret_prompt(proj, cos, sin, gnw, nbt):
    B, L, _ = proj.shape
    C = RET_CHUNK
    blk = lambda k: pl.BlockSpec((nbt, C, RET_W), lambda b, c: (b, c, k))
    return pl.pallas_call(
        _ret_prompt_body,
        grid=(B // nbt, L // C),
        in_specs=[blk(0), blk(1), blk(2), blk(3),
                  pl.BlockSpec((C, RET_DK), lambda b, c: (c, 0)),
                  pl.BlockSpec((C, RET_DK), lambda b, c: (c, 0)),
                  pl.BlockSpec((1, RET_W), lambda b, c: (0, 0))],
        out_specs=[pl.BlockSpec((nbt, C, RET_W), lambda b, c: (b, c, 0)),
                   pl.BlockSpec((nbt, RET_HEADS, RET_DK, RET_DK), lambda b, c: (b, 0, 0, 0))],
        out_shape=[jax.ShapeDtypeStruct((B, L, RET_W), BF16),
                   jax.ShapeDtypeStruct((B, RET_HEADS, RET_DK, RET_DK), F32)],
        scratch_shapes=[pltpu.VMEM((nbt, RET_HEADS, RET_DK, RET_DK), F32)],
        compiler_params=_params(("parallel", "arbitrary")),
        name="ret_prompt",
    )(proj, proj, proj, proj, cos, sin, gnw)


def _ret_sample_body(q_ref, k_ref, v_ref, g_ref, cos_ref, sin_ref, gnw_ref, dec_ref, s0_ref, o_ref, s_ref, obuf):
    nb = q_ref.shape[0]
    cos = cos_ref[0:1, :]
    sin = sin_ref[0:1, :]
    gdec = dec_ref[0]
    q = _rope(q_ref[...], cos, sin)
    k = _rope(k_ref[...], cos, sin) * (RET_DK ** -0.5)
    v = v_ref[...]
    qk = jnp.sum(q * k, axis=1, keepdims=True)
    qT = q.T
    kT = k.T
    for b in range(nb):
        s0 = s0_ref[0, b, 0]
        qcol = qT[:, b:b + 1]
        kcol = kT[:, b:b + 1]
        vrow = v[b:b + 1, :]
        cross = gdec * jnp.sum(s0 * qcol, axis=0, keepdims=True)
        s_ref[0, b, 0] = s0 * gdec + kcol * vrow
        obuf[b:b + 1, :] = qk[b:b + 1, :] * vrow + cross
    ro = obuf[...]
    mu = jnp.mean(ro, axis=-1, keepdims=True)
    rc = ro - mu
    var = jnp.mean(rc * rc, axis=-1, keepdims=True)
    o_ref[...] = (rc * lax.rsqrt(var + LN_EPS) * gnw_ref[...] * _silu(g_ref[...])).astype(BF16)


def _state_phases(body, n_in, state_pos, carried):
    def phased(*refs):
        p = pl.program_id(0)

        @pl.when(p == 0)
        def _():
            body(*refs)

        @pl.when(p == 1)
        def _():
            refs[state_pos][...] = jnp.zeros(refs[state_pos].shape, F32)

    if carried is None:
        return phased, 2, [], [], {}

    def carried_body(*refs):
        phased(*refs[:n_in], *refs[n_in + 1:])

    return carried_body, 1, [pl.BlockSpec(memory_space=pl.ANY)], [carried], {n_in: 1}


def _ret_sample(proj2, cos, sin, gnw, state, l, carried):
    nb = proj2.shape[0]
    H = RET_HEADS
    dec = jnp.asarray(np.broadcast_to(np.exp(np.array(RET_LOG_G))[:, None, None], (H, 1, RET_DK)), F32)
    hh = lambda p, h: h * (1 - p) + (H - 1) * p
    blk = lambda k: pl.BlockSpec((nb, RET_DK), lambda p, h: (0, k * H + hh(p, h)))
    in_specs = [blk(0), blk(1), blk(2), blk(3),
                pl.BlockSpec((8, RET_DK), lambda p, h: (0, 0)),
                pl.BlockSpec((8, RET_DK), lambda p, h: (0, 0)),
                pl.BlockSpec((1, RET_DK), lambda p, h: (0, hh(p, h))),
                pl.BlockSpec((1, 1, RET_DK), lambda p, h: (hh(p, h), 0, 0)),
                pl.BlockSpec((1, nb, 1, RET_DK, RET_DK), lambda p, h: (l, 0, hh(p, h), 0, 0))]
    body, phases, extra_specs, extra_args, aliases = _state_phases(_ret_sample_body, len(in_specs),
                                                                   len(in_specs) + 1, carried)
    return pl.pallas_call(
        body,
        grid=(phases, H),
        in_specs=in_specs + extra_specs,
        out_specs=[pl.BlockSpec((nb, RET_DK), lambda p, h: (0, hh(p, h))),
                   pl.BlockSpec((1, nb, 1, RET_DK, RET_DK), lambda p, h: (l + p, 0, h, 0, 0))],
        out_shape=[jax.ShapeDtypeStruct((nb, RET_W), BF16),
                   jax.ShapeDtypeStruct(state.shape, F32)],
        scratch_shapes=[pltpu.VMEM((nb, RET_DK), F32)],
        input_output_aliases=aliases,
        compiler_params=_params(("arbitrary", "arbitrary")),
        name="ret_sample",
    )(proj2, proj2, proj2, proj2, cos, sin, gnw, dec, state, *extra_args)


def _gdn_prompt_body(dq_ref, dk_ref, dv_ref, dz_ref, ab_ref, cw_ref, alog_ref, dtb_ref, nw_ref,
                     o_ref, s_ref, conv_ref, xbuf, S):
    c = pl.program_id(1)
    C = GDN_CHUNK
    H = GDN_HEADS
    dk = GDN_DK
    nbt = dq_ref.shape[0]

    @pl.when(c == 0)
    def _():
        S[...] = jnp.zeros_like(S)
        xbuf[:, 0:8, :] = jnp.zeros((nbt, 8, GDN_CONV_C), F32)

    row = lax.broadcasted_iota(jnp.int32, (C, dk), 0)
    lane = lax.broadcasted_iota(jnp.int32, (C, dk), 1)
    tri = lane <= row
    strict = lane < row
    r3 =lax.broadcasted_iota(jnp.int32, (C, 3 * C), 0)
    c3 = lax.broadcasted_iota(jnp.int32, (C, 3 * C), 1) % C
    tril_b = jnp.where(r3 >= c3, 1.0, 0.0).astype(BF16)

    def cumsum_rows(x):
        hi = x.astype(BF16)
        r1 = x - hi.astype(F32)
        mid = r1.astype(BF16)
        lo = (r1 - mid.astype(F32)).astype(BF16)
        return jnp.dot(tril_b, jnp.concatenate([hi, mid, lo], axis=0), preferred_element_type=F32)

    def lanes(x, j):
        return jnp.broadcast_to(x[:, j:j + 1], (x.shape[0], dk))

    cw = cw_ref[...]
    zrow_b = jnp.zeros((C, dk), BF16)

    items = []
    for bt in range(nbt):
        xbuf[bt, 8:8 + C, 0:GDN_W] = dq_ref[bt]
        xbuf[bt, 8:8 + C, GDN_W:2 * GDN_W] = dk_ref[bt]
        xbuf[bt, 8:8 + C, 2 * GDN_W:3 * GDN_W] = dv_ref[bt]

        def conv(off, bt=bt):
            xe = xbuf[bt, :, off:off + dk]
            y = xe[8:8 + C] * cw[CONV_W - 1:CONV_W, off:off + dk]
            for i in range(CONV_W - 1):
                shifted = pltpu.roll(xe, CONV_W - 1 - i, 0)[8:8 + C]
                y = y + shifted * cw[i:i + 1, off:off + dk]
            return _silu(y)

        ab = ab_ref[bt]
        g_c = -jnp.exp(alog_ref[...]) * jax.nn.softplus(ab + dtb_ref[...])
        gc_c = cumsum_rows(g_c)
        beta_c = jax.nn.sigmoid(ab)
        egc_c = jnp.exp(gc_c)
        gl_c = gc_c[C - 1:C, :]
        kdec_c = jnp.exp(gl_c - gc_c)
        egl_c = jnp.exp(gl_c)
        gc_t = jnp.concatenate([gc_c, jnp.zeros_like(gc_c)], axis=0).T

        for h in range(H):
            sl = slice(h * dk, (h + 1) * dk)
            q = conv(h * dk)
            k = conv(GDN_W + h * dk)
            v = conv(2 * GDN_W + h * dk)
            q = q * lax.rsqrt(jnp.sum(q * q, axis=-1, keepdims=True) + L2_EPS) * (dk ** -0.5)
            k = k * lax.rsqrt(jnp.sum(k * k, axis=-1, keepdims=True) + L2_EPS)
            beta = lanes(beta_c, H + h)
            egc = lanes(egc_c, h)
            kb = k * beta
            kq = jnp.concatenate([kb, q], axis=0).astype(BF16)
            k2 = jnp.concatenate([k.astype(BF16), zrow_b], axis=0)
            kkqk = lax.dot_general(kq, k2, NT, preferred_element_type=F32)
            dgc = lanes(gc_c, h) - jnp.broadcast_to(gc_t[h:h + 1, :], (C, dk))
            decay = jnp.exp(jnp.where(tri, dgc, -jnp.inf))
            a = jnp.where(strict, kkqk[0:C] * decay, 0.0)
            attn = (kkqk[C:2 * C] * decay)[:, 0:C].astype(BF16)
            rhs = jnp.concatenate([v * beta, kb * egc], axis=1).astype(BF16)
            items.append(dict(bt=bt, h=h, a=a, rhs=rhs, attn=attn, qd=(q * egc).astype(BF16),
                              kd=(k * lanes(kdec_c, h)).astype(BF16), egl=lanes(egl_c, h)))

    pair_col = lane % C
    eye_pair = (pair_col == row).astype(F32)
    left = lane < C

    def level_mask(s):
        br = row // s
        return (pair_col // s) == jnp.where(br % 2 == 1, br - 1, -1)

    left_b = jnp.where(left, 1.0, 0.0).astype(BF16)
    right_b = jnp.where(left, 0.0, 1.0).astype(BF16)

    def block_diag(x_b):
        return jnp.concatenate([x_b * left_b, x_b * right_b], axis=0)

    pairs = []
    for i in range(0, len(items), 2):
        a_pair = items[i]["a"] + pltpu.roll(items[i + 1]["a"], C, 1)
        pairs.append(dict(a=a_pair.astype(BF16), d=eye_pair - jnp.where(level_mask(1), a_pair, 0.0)))
    s_blk = 2
    while s_blk < C:
        mask_b = jnp.where(level_mask(s_blk), 1.0, 0.0).astype(BF16)
        for p in pairs:
            d = p["d"]
            d_b = d.astype(BF16)
            g = jnp.dot(d_b, block_diag(p["a"] * mask_b), preferred_element_type=F32)
            p["d"] = d - jnp.dot(g.astype(BF16), block_diag(d_b), preferred_element_type=F32)
        s_blk *= 2
    for i, p in enumerate(pairs):
        r0, r1 = items[2 * i]["rhs"], items[2 * i + 1]["rhs"]
        zero = jnp.zeros_like(r0)
        rr = jnp.concatenate([jnp.concatenate([r0, zero], axis=1), jnp.concatenate([zero, r1], axis=1)], axis=0)
        sol = jnp.dot(p["d"].astype(BF16), rr, preferred_element_type=F32)
        items[2 * i]["sol"] = sol[:, 0:2 * dk]
        items[2 * i + 1]["sol"] = sol[:, 2 * dk:4 * dk]

    for it in items:
        bt, h = it["bt"], it["h"]
        sl = slice(h * dk, (h + 1) * dk)
        u = it["sol"][:, 0:dk]
        w = it["sol"][:, dk:2 * dk]
        s_prev = S[bt, h]
        wq = jnp.concatenate([w.astype(BF16), it["qd"]], axis=0)
        r = jnp.dot(wq, s_prev.astype(BF16), preferred_element_type=F32)
        v_new = u - r[0:C]
        o = r[C:2 * C] + jnp.dot(it["attn"], v_new.astype(BF16), preferred_element_type=F32)
        S[bt, h] = s_prev * it["egl"] + lax.dot_general(it["kd"], v_new.astype(BF16), TN, preferred_element_type=F32)
        o = o * lax.rsqrt(jnp.mean(o * o, axis=-1, keepdims=True) + RMS_EPS) * nw_ref[...]
        o_ref[bt, :, sl] = (o * _silu(dz_ref[bt, :, sl])).astype(BF16)

    for bt in range(nbt):
        xbuf[bt, 0:8, :] = xbuf[bt, C:C + 8, :]

    @pl.when(c == pl.num_programs(1) - 1)
    def _():
        s_ref[...] = S[...]
        conv_ref[...] = xbuf[:, C + 8 - (CONV_W - 1):C + 8, :]


def _gdn_prompt(proj, cw, alog_bc, dtb_bc, nw, nbt):
    B, L, _ = proj.shape
    C = GDN_CHUNK
    H = GDN_HEADS
    blk = lambda k: pl.BlockSpec((nbt, C, GDN_W), lambda b, c: (b, c, k))
    full = lambda shape: pl.BlockSpec(shape, lambda b, c: tuple(0 for _ in shape))
    return pl.pallas_call(
        _gdn_prompt_body,
        grid=(B // nbt, L // C),
        in_specs=[blk(2), blk(3), blk(4), blk(5),
                  pl.BlockSpec((nbt, C, 128), lambda b, c: (b, c, COL_AB // 128)),
                  full((CONV_W, GDN_CONV_C)), full((1, 128)), full((1, 128)), full((1, GDN_DK))],
        out_specs=[pl.BlockSpec((nbt, C, GDN_W), lambda b, c: (b, c, 0)),
                   pl.BlockSpec((nbt, H, GDN_DK, GDN_DK), lambda b, c: (b, 0, 0, 0)),
                   pl.BlockSpec((nbt, CONV_W - 1, GDN_CONV_C), lambda b, c: (b, 0, 0))],
        out_shape=[jax.ShapeDtypeStruct((B, L, GDN_W), BF16),
                   jax.ShapeDtypeStruct((B, H, GDN_DK, GDN_DK), F32),
                   jax.ShapeDtypeStruct((B, CONV_W - 1, GDN_CONV_C), F32)],
        scratch_shapes=[pltpu.VMEM((nbt, C + 8, GDN_CONV_C), F32),
                        pltpu.VMEM((nbt, H, GDN_DK, GDN_DK), F32)],
        compiler_params=_params(("parallel", "arbitrary")),
        name="gdn_prompt",
    )(proj, proj, proj, proj, proj, cw, alog_bc, dtb_bc, nw)


def _gdn_sample_body(dq_ref, dk_ref, dv_ref, dz_ref, ab_ref, csq_ref, csk_ref, csv_ref, cwq_ref, cwk_ref, cwv_ref,
                     alog_ref, dtb_ref, nw_ref, s0_ref, o_ref, s_ref, nq_ref, nk_ref, nv_ref, obuf):
    h = pl.program_id(1)
    nb = dq_ref.shape[0]
    dk = GDN_DK

    def conv(x_ref, cs_ref, w_ref, n_ref):
        x = x_ref[...]
        y = cs_ref[0] * w_ref[0:1, :] + cs_ref[1] * w_ref[1:2, :] + cs_ref[2] * w_ref[2:3, :] + x * w_ref[3:4, :]
        n_ref[0] = cs_ref[1]
        n_ref[1] = cs_ref[2]
        n_ref[2] = x
        return _silu(y)

    q = conv(dq_ref, csq_ref, cwq_ref, nq_ref)
    k = conv(dk_ref, csk_ref, cwk_ref, nk_ref)
    v = conv(dv_ref, csv_ref, cwv_ref, nv_ref)
    q = q * lax.rsqrt(jnp.sum(q * q, axis=-1, keepdims=True) + L2_EPS) * (dk ** -0.5)
    k = k * lax.rsqrt(jnp.sum(k * k, axis=-1, keepdims=True) + L2_EPS)
    er = lax.broadcasted_iota(jnp.int32, (128, dk), 0)
    ab = ab_ref[...]
    a_bc = jnp.dot(ab, (er == h).astype(F32), precision=HI, preferred_element_type=F32)
    b_bc = jnp.dot(ab, (er == h + GDN_HEADS).astype(F32), precision=HI, preferred_element_type=F32)
    g = -jnp.exp(alog_ref[0]) * jax.nn.softplus(a_bc + dtb_ref[0])
    eg = jnp.exp(g)
    beta = jax.nn.sigmoid(b_bc)
    qk = jnp.sum(q * k, axis=1, keepdims=True)
    qT = q.T
    kT = k.T
    for b in range(nb):
        s0 = s0_ref[0, b, 0]
        qcol = qT[:, b:b + 1]
        kcol = kT[:, b:b + 1]
        ks = jnp.sum(s0 * kcol, axis=0, keepdims=True)
        qs = jnp.sum(s0 * qcol, axis=0, keepdims=True)
        bb = beta[b:b + 1, :]
        egb = eg[b:b + 1, :]
        v_new = bb * v[b:b + 1, :] - bb * egb * ks
        s_ref[0, b, 0] = s0 * egb + kcol * v_new
        obuf[b:b + 1, :] = egb * qs + qk[b:b + 1, :] * v_new
    o = obuf[...]
    o = o * lax.rsqrt(jnp.mean(o * o, axis=-1, keepdims=True) + RMS_EPS) * nw_ref[...]
    o_ref[...] = (o * _silu(dz_ref[...])).astype(BF16)


def _gdn_sample(proj2, cs_t, cw, alog_t, dtb_t, nw, state, l, carried):
    nb = proj2.shape[0]
    H = GDN_HEADS
    dk = GDN_DK
    hh = lambda p, h: h * (1 - p) + (H - 1) * p
    blk = lambda k: pl.BlockSpec((nb, dk), lambda p, h: (0, k * H + hh(p, h)))
    csblk = lambda k: pl.BlockSpec((CONV_W - 1, nb, dk), lambda p, h: (0, 0, k * H + hh(p, h)))
    cwblk = lambda k: pl.BlockSpec((CONV_W, dk), lambda p, h: (0, k * H + hh(p, h)))
    tab = pl.BlockSpec((1, 1, dk), lambda p, h: (hh(p, h), 0, 0))
    nblk = pl.BlockSpec((CONV_W - 1, nb, dk), lambda p, h: (0, 0, hh(p, h)))
    nshape = jax.ShapeDtypeStruct((CONV_W - 1, nb, GDN_W), F32)
    in_specs = [blk(2), blk(3), blk(4), blk(5),
                pl.BlockSpec((nb, 128), lambda p, h: (0, COL_AB // 128)),
                csblk(0), csblk(1), csblk(2), cwblk(0), cwblk(1), cwblk(2),
                tab, tab, pl.BlockSpec((1, dk), lambda p, h: (0, 0)),
                pl.BlockSpec((1, nb, 1, dk, dk), lambda p, h: (l, 0, hh(p, h), 0, 0))]
    body, phases, extra_specs, extra_args, aliases = _state_phases(_gdn_sample_body, len(in_specs),
                                                                   len(in_specs) + 1, carried)
    return pl.pallas_call(
        body,
        grid=(phases, H),
        in_specs=in_specs + extra_specs,
        out_specs=[pl.BlockSpec((nb, dk), lambda p, h: (0, hh(p, h))),
                   pl.BlockSpec((1, nb, 1, dk, dk), lambda p, h: (l + p, 0, h, 0, 0)), nblk, nblk, nblk],
        out_shape=[jax.ShapeDtypeStruct((nb, GDN_W), BF16), jax.ShapeDtypeStruct(state.shape, F32),
                   nshape, nshape, nshape],
        scratch_shapes=[pltpu.VMEM((nb, dk), F32)],
        input_output_aliases=aliases,
        compiler_params=_params(("arbitrary", "arbitrary")),
        name="gdn_sample",
    )(proj2, proj2, proj2, proj2, proj2, cs_t, cs_t, cs_t, cw, cw, cw, alog_t, dtb_t, nw, state, *extra_args)


def _ssm_prep_body(are_ref, aim_ref, ldt_ref, bre_ref, bim_ref, lre_ref, lim_ref, obre_ref, obim_ref):
    ar = are_ref[0]
    ai = aim_ref[0]
    dt = jnp.exp(ldt_ref[0])
    er = jnp.exp(ar * dt)
    lr = er * jnp.cos(ai * dt)
    li = er * jnp.sin(ai * dt)
    lre_ref[0] = lr
    lim_ref[0] = li
    xr = lr - 1.0
    den = ar * ar + ai * ai
    cr = ((xr * ar + li * ai) / den)[:, None, :]
    ci = ((li * ar - xr * ai) / den)[:, None, :]
    br = bre_ref[0]
    bi = bim_ref[0]
    obre_ref[0] = cr * br - ci * bi
    obim_ref[0] = cr * bi + ci * br


def _ssm_prep(a_re, a_im, log_dt, b_re, b_im):
    G, N, P = SSM_GROUPS, SSM_N, SSM_GROUP
    ldt = jnp.broadcast_to(log_dt[:, :, None], (DEPTH, G, N))
    bt_re = jnp.transpose(b_re, (0, 1, 3, 2))
    bt_im = jnp.transpose(b_im, (0, 1, 3, 2))
    s2 = pl.BlockSpec((1, G, N), lambda l: (l, 0, 0))
    s3 = pl.BlockSpec((1, G, P, N), lambda l: (l, 0, 0, 0))
    return pl.pallas_call(
        _ssm_prep_body,
        grid=(DEPTH,),
        in_specs=[s2, s2, s2, s3, s3],
        out_specs=[s2, s2, s3, s3],
        out_shape=[jax.ShapeDtypeStruct((DEPTH, G, N), F32), jax.ShapeDtypeStruct((DEPTH, G, N), F32),
                   jax.ShapeDtypeStruct((DEPTH, G, P, N), F32), jax.ShapeDtypeStruct((DEPTH, G, P, N), F32)],
        name="ssm_prep",
    )(a_re, a_im, ldt, bt_re, bt_im)


def _ssm_tile_blockdiag(re, im):
    x = jnp.stack([re, im]).reshape(2, SSM_TILES, SSM_TILE_G, SSM_GROUP, SSM_N)
    eye = jnp.eye(SSM_TILE_G, dtype=x.dtype)
    y = x[:, :, :, :, None, :] * eye[None, None, :, None, :, None]
    y = jnp.transpose(y, (1, 2, 3, 0, 4, 5))
    return y.reshape(SSM_TILES, SSM_TILE_G * SSM_GROUP, SSM_TILE_W)


def _ssm_state_to_tiles(re, im):
    B = re.shape[0]
    x = jnp.stack([re, im], axis=1).reshape(B, 2, SSM_TILES, SSM_HALF)
    return jnp.transpose(x, (0, 2, 1, 3)).reshape(B, SSM_TILES * SSM_TILE_W)


def _ssm_tiles_to_state(x):
    B = x.shape[0]
    x = jnp.transpose(x.reshape(B, SSM_TILES, 2, SSM_HALF), (0, 2, 1, 3)).reshape(B, 2, SSM_GROUPS, SSM_N)
    return x[:, 0], x[:, 1]


def _cmul(ar, ai, br, bi):
    return ar * br - ai * bi, ar * bi + ai * br


def _s5_prompt_body(u_ref, bt_ref, ct_ref, lam_ref, d_ref, y_ref, hl_ref, buf, ptab):
    L = u_ref.shape[1]
    nseg = L // SSM_SEG
    ns = SSM_HALF // 128
    bmat = bt_ref[0].astype(BF16)
    for i in range(nseg):
        bu = jnp.dot(u_ref[0, i * SSM_SEG:(i + 1) * SSM_SEG, :].astype(BF16), bmat, preferred_element_type=F32)
        for s in range(2 * ns):
            buf[s, i * SSM_PITCH:i * SSM_PITCH + SSM_SEG, :] = bu[:, s * 128:(s + 1) * 128]

    lam = lam_ref[0]
    a_re = [jnp.broadcast_to(lam[:, s * 128:(s + 1) * 128], (nseg, 128)) for s in range(ns)]
    a_im = [jnp.broadcast_to(lam[:, SSM_HALF + s * 128:SSM_HALF + (s + 1) * 128], (nseg, 128)) for s in range(ns)]

    def scan_step(t, carry):
        out = []
        for s in range(ns):
            hr, hi = carry[2 * s], carry[2 * s + 1]
            pr, pi = _cmul(a_re[s], a_im[s], hr, hi)
            hr = pr + buf[s, pl.ds(t, nseg, stride=SSM_PITCH), :]
            hi = pi + buf[ns + s, pl.ds(t, nseg, stride=SSM_PITCH), :]
            buf[s, pl.ds(t, nseg, stride=SSM_PITCH), :] = hr
            buf[ns + s, pl.ds(t, nseg, stride=SSM_PITCH), :] = hi
            out += [hr, hi]
        return tuple(out)

    zero = jnp.zeros((nseg, 128), F32)
    ends = lax.fori_loop(0, SSM_SEG, scan_step, tuple(zero for _ in range(2 * ns)), unroll=SSM_UNROLL)

    sub = lax.broadcasted_iota(jnp.int32, (nseg, 128), 0)
    carries = []
    for s in range(ns):
        pr, pi = a_re[s], a_im[s]
        for _ in range(int(math.log2(SSM_SEG))):
            pr, pi = _cmul(pr, pi, pr, pi)
        xr, xi = ends[2 * s], ends[2 * s + 1]
        shift = 1
        while shift < nseg:
            sr = jnp.where(sub >= shift, pltpu.roll(xr, shift, 0), 0.0)
            si = jnp.where(sub >= shift, pltpu.roll(xi, shift, 0), 0.0)
            mr, mi = _cmul(pr, pi, sr, si)
            xr, xi = xr + mr, xi + mi
            pr, pi = _cmul(pr, pi, pr, pi)
            shift *= 2
        hl_ref[0, 0, :, s * 128:(s + 1) * 128] = xr[nseg - 1:nseg, :]
        hl_ref[0, 0, :, SSM_HALF + s * 128:SSM_HALF + (s + 1) * 128] = xi[nseg - 1:nseg, :]
        carries += [jnp.where(sub >= 1, pltpu.roll(xr, 1, 0), 0.0), jnp.where(sub >= 1, pltpu.roll(xi, 1, 0), 0.0)]

    tile_id = pl.program_id(1)

    @pl.when(pl.program_id(0) == 0)
    def _():
        lam_rows = [(lam[:, s * 128:(s + 1) * 128], lam[:, SSM_HALF + s * 128:SSM_HALF + (s + 1) * 128])
                    for s in range(ns)]

        def tab_step(t, pw):
            out = []
            for s in range(ns):
                pr, pi = pw[2 * s], pw[2 * s + 1]
                ptab[tile_id, s, pl.ds(t, 1), :] = pr
                ptab[tile_id, ns + s, pl.ds(t, 1), :] = pi
                out += list(_cmul(pr, pi, lam_rows[s][0], lam_rows[s][1]))
            return tuple(out)

        lax.fori_loop(0, SSM_SEG, tab_step, tuple(x for pair in lam_rows for x in pair), unroll=SSM_UNROLL)

    def fix_step(t, carry):
        for s in range(ns):
            pr = jnp.broadcast_to(ptab[tile_id, s, pl.ds(t, 1), :], (nseg, 128))
            pi = jnp.broadcast_to(ptab[tile_id, ns + s, pl.ds(t, 1), :], (nseg, 128))
            fr, fi = _cmul(pr, pi, carries[2 * s], carries[2 * s + 1])
            buf[s, pl.ds(t, nseg, stride=SSM_PITCH), :] += fr
            buf[ns + s, pl.ds(t, nseg, stride=SSM_PITCH), :] += fi
        return carry

    lax.fori_loop(0, SSM_SEG, fix_step, 0, unroll=SSM_UNROLL)

    cmat = ct_ref[0]
    cmat = jnp.concatenate([cmat[:, 0:SSM_HALF], -cmat[:, SSM_HALF:]], axis=1).astype(BF16)
    for i in range(nseg):
        rows = slice(i * SSM_PITCH, i * SSM_PITCH + SSM_SEG)
        h = jnp.concatenate([buf[s, rows, :].astype(BF16) for s in range(2 * ns)], axis=1)
        y = lax.dot_general(h, cmat, NT, preferred_element_type=F32)
        u = u_ref[0, i * SSM_SEG:(i + 1) * SSM_SEG, :]
        y_ref[0, i * SSM_SEG:(i + 1) * SSM_SEG, :] = jax.nn.gelu(y + d_ref[...] * u)


def _s5_prompt(proj, bt, ct, lam, d):
    B, L, _ = proj.shape
    nseg = L // SSM_SEG
    tile = lambda shape: pl.BlockSpec(shape, lambda b, j: (j,) + tuple(0 for _ in shape[1:]))
    return pl.pallas_call(
        _s5_prompt_body,
        grid=(B, SSM_TILES),
        in_specs=[pl.BlockSpec((1, L, 128), lambda b, j: (b, 0, COL_SU // 128 + j)),
                  tile((1, 128, SSM_TILE_W)), tile((1, 128, SSM_TILE_W)), tile((1, 1, SSM_TILE_W)),
                  pl.BlockSpec((1, 128), lambda b, j: (0, j))],
        out_specs=[pl.BlockSpec((1, L, 128), lambda b, j: (b, 0, j)),
                   pl.BlockSpec((1, 1, 1, SSM_TILE_W), lambda b, j: (b, j, 0, 0))],
        out_shape=[jax.ShapeDtypeStruct((B, L, SSM_W), F32),
                   jax.ShapeDtypeStruct((B, SSM_TILES, 1, SSM_TILE_W), F32)],
        scratch_shapes=[pltpu.VMEM((SSM_TILE_W // 128, nseg * SSM_PITCH, 128), F32),
                        pltpu.VMEM((SSM_TILES, SSM_TILE_W // 128, SSM_SEG, 128), F32)],
        compiler_params=_params(("arbitrary", "arbitrary")),
        name="s5_prompt",
    )(proj, bt, ct, lam, d)


def _s5_sample_body(u_ref, bt_ref, ct_ref, lam_ref, d_ref, h0_ref, y_ref, h_ref):
    u = u_ref[...]
    bu = jnp.dot(u, bt_ref[0], precision=HI, preferred_element_type=F32)
    lam = lam_ref[0]
    lr, li = lam[:, 0:SSM_HALF], lam[:, SSM_HALF:]
    h0r, h0i = h0_ref[:, 0:SSM_HALF], h0_ref[:, SSM_HALF:]
    pr, pi = _cmul(lr, li, h0r, h0i)
    hr = pr + bu[:, 0:SSM_HALF]
    hi = pi + bu[:, SSM_HALF:]
    h_ref[:, 0:SSM_HALF] = hr
    h_ref[:, SSM_HALF:] = hi
    ct = ct_ref[0]
    y = lax.dot_general(hr, ct[:, 0:SSM_HALF], NT, precision=HI, preferred_element_type=F32)
    y -= lax.dot_general(hi, ct[:, SSM_HALF:], NT, precision=HI, preferred_element_type=F32)
    y_ref[...] = jax.nn.gelu(y + d_ref[...] * u)


def _s5_sample(proj2, bt, ct, lam, d, h0):
    nb = proj2.shape[0]
    tile = lambda shape: pl.BlockSpec(shape, lambda j: (j,) + tuple(0 for _ in shape[1:]))
    hblk = pl.BlockSpec((nb, SSM_TILE_W), lambda j: (0, j))
    return pl.pallas_call(
        _s5_sample_body,
        grid=(SSM_TILES,),
        in_specs=[pl.BlockSpec((nb, 128), lambda j: (0, COL_SU // 128 + j)),
                  tile((1, 128, SSM_TILE_W)), tile((1, 128, SSM_TILE_W)), tile((1, 1, SSM_TILE_W)),
                  pl.BlockSpec((1, 128), lambda j: (0, j)), hblk],
        out_specs=[pl.BlockSpec((nb, 128), lambda j: (0, j)), hblk],
        out_shape=[jax.ShapeDtypeStruct((nb, SSM_W), F32), jax.ShapeDtypeStruct(h0.shape, F32)],
        compiler_params=_params(("parallel",)),
        name="s5_sample",
    )(proj2, bt, ct, lam, d, h0)


def kernel(x_prompt, x_sample, c_prompt, c_sample, state_ret, state_gdn, state_conv, state_ssm_re, state_ssm_im, w_mod, b_mod, w_in, conv_w, ret_gn_w, gdn_a_log, gdn_dt_bias, gdn_norm_w, ssm_a_re, ssm_a_im, ssm_log_dt, ssm_b_re, ssm_b_im, ssm_c_re, ssm_c_im, ssm_d, ssm_w_glu, w_out, ln1_w, ln1_b, w_ffn_in, w_ffn_out, ln2_w, ln2_b):
    Bp, Lp, _ = x_prompt.shape
    Bs = x_sample.shape[0]
    assert x_sample.shape[1] == 1 and Bs % 8 == 0

    pad_rows = (-(Bp + Bs)) % 8
    c_all = jnp.concatenate([c_prompt, c_sample, jnp.zeros((pad_rows, D_MODEL), F32)], axis=0)
    mod_all = _modulation(c_all, w_mod, b_mod)
    lam_re, lam_im, bbar_re, bbar_im = _ssm_prep(ssm_a_re, ssm_a_im, ssm_log_dt, ssm_b_re, ssm_b_im)
    cos_p, sin_p, cos_s, sin_s = _rope_tables(Lp)

    w_in_b = w_in.astype(BF16)
    w_out_b = _to_bf16(w_out, rows=512)
    w_glu_b = ssm_w_glu.astype(BF16)
    w_ffi_b = _to_bf16_col_tiles(w_ffn_in, rows=256, tc=FFN_TF)
    w_ffo_b = _to_bf16(w_ffn_out, rows=512)

    yp = x_prompt
    ys = x_sample.reshape(1, Bs, D_MODEL)
    outs_p, outs_s = [], []
    ret_s = gdn_s = None
    for l in range(DEPTH):
        w_tail = jnp.concatenate(
            [w_in_b[l, :, COL_SU + 2 * GDN_HEADS:], w_in_b[l, :, COL_SU:COL_SU + 2 * GDN_HEADS],
             jnp.zeros((D_MODEL, N_PROJ - N_IN), BF16)], axis=1)
        mod_p = mod_all[l, :Bp][:, None, :]
        mod_s = mod_all[l, Bp:Bp + Bs][None]
        gnw = ret_gn_w[l][None, :]
        nw = gdn_norm_w[l][None, :]
        alog_bc = jnp.pad(gdn_a_log[l], (0, 128 - GDN_HEADS))[None, :]
        dtb_bc = jnp.pad(gdn_dt_bias[l], (0, 128 - GDN_HEADS))[None, :]
        alog_t = jnp.broadcast_to(gdn_a_log[l][:, None, None], (GDN_HEADS, 1, GDN_DK))
        dtb_t = jnp.broadcast_to(gdn_dt_bias[l][:, None, None], (GDN_HEADS, 1, GDN_DK))
        ssm_bt = _ssm_tile_blockdiag(bbar_re[l], bbar_im[l])
        ssm_ct = _ssm_tile_blockdiag(ssm_c_re[l], ssm_c_im[l])
        ssm_lam = jnp.concatenate([lam_re[l].reshape(SSM_TILES, 1, SSM_HALF),
                                   lam_im[l].reshape(SSM_TILES, 1, SSM_HALF)], axis=-1)
        ssm_dl = ssm_d[l][None, :]
        lnw1, lnb1 = ln1_w[l][None, :], ln1_b[l][None, :]
        lnw2, lnb2 = ln2_w[l][None, :], ln2_b[l][None, :]

        proj = _inproj(yp, mod_p, w_in_b, l, w_tail, tm=1024, tn=PROJ_TN)
        ro, ret_p = _ret_prompt(proj, cos_p, sin_p, gnw, nbt=4)
        go, gdn_p, conv_p = _gdn_prompt(proj, conv_w[l], alog_bc, dtb_bc, nw, nbt=4)
        so, hl = _s5_prompt(proj, ssm_bt, ssm_ct, ssm_lam, ssm_dl)
        re_p, im_p = _ssm_tiles_to_state(hl.reshape(Bp, SSM_TILES * SSM_TILE_W))
        x1 = _outproj(ro, go, so, yp, mod_p, w_glu_b, w_out_b, l, lnw1, lnb1, tm=512)
        yp = _ffn(x1, mod_p, w_ffi_b, w_ffo_b, l, lnw2, lnb2, tm=512, tf=FFN_TF)
        outs_p.append((ret_p, gdn_p, conv_p, re_p, im_p))

        proj = _inproj(ys, mod_s, w_in_b, l, w_tail, tm=Bs, tn=PROJ_TN)
        proj2 = proj.reshape(Bs, N_PROJ)
        ro, ret_s = _ret_sample(proj2, cos_s, sin_s, gnw, state_ret, l, ret_s)
        cs_t = jnp.transpose(state_conv[l], (1, 0, 2))
        go, gdn_s, nq, nk, nv = _gdn_sample(proj2, cs_t, conv_w[l], alog_t, dtb_t, nw, state_gdn, l, gdn_s)
        conv_s = jnp.transpose(jnp.concatenate([nq, nk, nv], axis=-1), (1, 0, 2))
        h0 = _ssm_state_to_tiles(state_ssm_re[l], state_ssm_im[l])
        so, hn = _s5_sample(proj2, ssm_bt, ssm_ct, ssm_lam, ssm_dl, h0)
        re_s, im_s = _ssm_tiles_to_state(hn)
        x1 = _outproj(ro[None], go[None], so[None], ys, mod_s, w_glu_b, w_out_b, l, lnw1, lnb1, tm=Bs)
        ys = _ffn(x1, mod_s, w_ffi_b, w_ffo_b, l, lnw2, lnb2, tm=Bs, tf=FFN_TF)
        outs_s.append((conv_s, re_s, im_s))

    ret_p, gdn_p, conv_p, re_p, im_p = [jnp.stack(t) for t in zip(*outs_p)]
    conv_s, re_s, im_s = [jnp.stack(t) for t in zip(*outs_s)]
    return (yp, ys.reshape(Bs, 1, D_MODEL), ret_p, gdn_p, conv_p, re_p, im_p,
            ret_s, gdn_s, conv_s, re_s, im_s)
```
